```python
import jax, jax.numpy as jnp
from jax import lax
import numpy as np

D_MODEL = 2048
BATCH = 4
SEQ = 2048
DEPTH = 1
DEC_BATCH = 128
DEC_SEQ = 8
PAST_LEN = 16384
PAGE_SIZE = 128

POOL_WINDOWS = (2, 4, 8, 16)
POOL_GROUPS = 4
POOL_GROUP_WIDTH = D_MODEL // 8
POOL_WIDTH = POOL_GROUPS * POOL_GROUP_WIDTH
POOL_BUF = 15
MLSTM_HEADS = 4
MLSTM_HEAD_DIM = D_MODEL // 4
MLSTM_WIDTH = MLSTM_HEADS * MLSTM_HEAD_DIM
MLSTM_CHUNK = 128
MEM_TOKENS = 256
MEM_HEADS = 4
MEM_HEAD_DIM = D_MODEL // 8
MEM_WIDTH = MEM_HEADS * MEM_HEAD_DIM
N_BRANCHES = 3
EPS = 1e-6
IN_SIZES = (POOL_WIDTH, POOL_WIDTH,
            MLSTM_WIDTH, MLSTM_WIDTH, MLSTM_WIDTH, MLSTM_WIDTH, MLSTM_WIDTH, MLSTM_HEADS, MLSTM_HEADS,
            MEM_WIDTH, MEM_WIDTH,
            D_MODEL, D_MODEL, D_MODEL)
N_IN = sum(IN_SIZES)

kernel_name = "hybrid_pool_mlstm_memory_decode_step"


def rmsnorm(x, g):
    xf = x.astype(jnp.float32)
    y = xf * lax.rsqrt(jnp.mean(xf * xf, axis=-1, keepdims=True) + EPS) * g.astype(jnp.float32)
    return y.astype(x.dtype)


def pool_mixer(v, buf, start):
    B, L, _ = v.shape
    u = jnp.concatenate([buf.astype(v.dtype), v], axis=1)
    cs = jnp.cumsum(u.astype(jnp.float32), axis=1)
    cs = jnp.concatenate([jnp.zeros((B, 1, POOL_WIDTH), jnp.float32), cs], axis=1)
    pos = start + jnp.arange(L)
    outs = []
    for g, w in enumerate(POOL_WINDOWS):
        sl = slice(g * POOL_GROUP_WIDTH, (g + 1) * POOL_GROUP_WIDTH)
        s = cs[:, POOL_BUF + 1:POOL_BUF + 1 + L, sl] - cs[:, POOL_BUF + 1 - w:POOL_BUF + 1 - w + L, sl]
        cnt = jnp.minimum(w, pos + 1).astype(jnp.float32)
        outs.append(s / cnt[None, :, None])
    p = jnp.concatenate(outs, axis=-1) - v.astype(jnp.float32)
    return p, u[:, -POOL_BUF:]


def mlstm_chunk_step(carry, inp):
    C0, n0, m0 = carry
    q, k, v, ig, lf = inp
    c = q.shape[2]
    b = jnp.cumsum(lf, axis=-1)
    tril = jnp.tril(jnp.ones((c, c), dtype=bool))
    dlog = jnp.where(tril, b[..., :, None] - b[..., None, :] + ig[..., None, :], -jnp.inf)
    inter = b + m0[..., None]
    m = jnp.maximum(inter, jnp.max(dlog, axis=-1))
    dw = jnp.exp(dlog - m[..., None])
    inter_w = jnp.exp(inter - m)
    s = jnp.einsum('bhtd,bhsd->bhts', q, k) * dw
    num = jnp.einsum('bhts,bhsd->bhtd', s, v) + inter_w[..., None] * jnp.einsum('bhvk,bhtk->bhtv', C0, q)
    den = jnp.sum(s, axis=-1) + inter_w * jnp.einsum('bhk,bhtk->bht', n0, q)
    h = num / jnp.maximum(jnp.abs(den), jnp.exp(-m))[..., None]
    m_new = m[..., -1]
    w = jnp.exp(b[..., -1:] - b + ig - m_new[..., None])
    decay = jnp.exp(b[..., -1] + m0 - m_new)
    C_new = decay[..., None, None] * C0 + jnp.einsum('bhs,bhsv,bhsk->bhvk', w, v, k)
    n_new = decay[..., None] * n0 + jnp.einsum('bhs,bhsk->bhk', w, k)
    return (C_new, n_new, m_new), h


def mlstm(q, k, v, i_pre, f_pre, C0, n0, m0):
    B, L, _ = q.shape
    c = min(MLSTM_CHUNK, L)
    nc = L // c

    def heads(t):
        return t.astype(jnp.float32).reshape(B, nc, c, MLSTM_HEADS, MLSTM_HEAD_DIM).transpose(1, 0, 3, 2, 4)

    def gates(t):
        return t.astype(jnp.float32).reshape(B, nc, c, MLSTM_HEADS).transpose(1, 0, 3, 2)

    qh, vh = heads(q), heads(v)
    kh = heads(k) * (MLSTM_HEAD_DIM ** -0.5)
    ih = gates(i_pre)
    lf = jax.nn.log_sigmoid(gates(f_pre))
    carry0 = (C0.astype(jnp.float32), n0.astype(jnp.float32), m0.astype(jnp.float32))
    (C, n, m), hs = lax.scan(mlstm_chunk_step, carry0, (qh, kh, vh, ih, lf))
    h = hs.transpose(1, 0, 3, 2, 4).reshape(B, L, MLSTM_WIDTH)
    return h, C.astype(C0.dtype), n.astype(n0.dtype), m.astype(m0.dtype)


def mem_kv(mem, g_mem, w_mem_kv):
    B = mem.shape[0]
    kv = rmsnorm(mem, g_mem) @ w_mem_kv
    k, v = jnp.split(kv, 2, axis=-1)
    return (k.reshape(B, MEM_TOKENS, MEM_HEADS, MEM_HEAD_DIM),
            v.reshape(B, MEM_TOKENS, MEM_HEADS, MEM_HEAD_DIM))


def cross_attn(q, mem_k, mem_v):
    B, L, _ = q.shape
    qh = q.reshape(B, L, MEM_HEADS, MEM_HEAD_DIM)
    s = jnp.einsum('blhd,bmhd->bhlm', qh, mem_k.astype(q.dtype)).astype(jnp.float32) * (MEM_HEAD_DIM ** -0.5)
    a = jax.nn.softmax(s, axis=-1).astype(q.dtype)
    o = jnp.einsum('bhlm,bmhd->blhd', a, mem_v.astype(q.dtype))
    return o.reshape(B, L, MEM_WIDTH)


def mixer_layer(x, start, pool_buf, C0, n0, m0, mem_k, mem_v, g_pre, g_post, w_in, b_mlstm_i, b_mlstm_f,
                w_pool_grp, pool_scale, w_br_pool, w_br_mlstm, w_br_mem, w_out):
    B, L, _ = x.shape
    dt = x.dtype
    h = rmsnorm(x, g_pre)
    proj = h @ w_in
    idx = np.cumsum(IN_SIZES)[:-1].tolist()
    (pv, pz, q, k, v, o, z, ig, fg, cq, cz, ga, gb, gc) = jnp.split(proj, idx, axis=-1)
    p, new_buf = pool_mixer(pv, pool_buf, start)
    pa = jnp.einsum('blgc,gcd->blgd', p.reshape(B, L, POOL_GROUPS, POOL_GROUP_WIDTH),
                    w_pool_grp.astype(jnp.float32)).reshape(B, L, POOL_WIDTH) * pool_scale.astype(jnp.float32)
    ya = (pa.astype(dt) * jax.nn.silu(pz)) @ w_br_pool
    hb, C, n, m = mlstm(q, k, v, ig + b_mlstm_i, fg + b_mlstm_f, C0, n0, m0)
    yb = (jax.nn.sigmoid(o) * hb.astype(dt) * jax.nn.silu(z)) @ w_br_mlstm
    yc = (cross_attn(cq, mem_k, mem_v) * jax.nn.silu(cz)) @ w_br_mem
    merged = jax.nn.sigmoid(ga) * ya + jax.nn.sigmoid(gb) * yb + jax.nn.sigmoid(gc) * yc
    y = x + rmsnorm(merged @ w_out, g_post)
    return y, new_buf, C, n, m


def setup_inputs(seed: int = 0) -> dict:
    key = jax.random.key(seed)
    ks = jax.random.split(key, 24)
    f32 = jnp.float32
    nrm = lambda k, s, sc=1.0: jax.random.normal(k, s, f32) * sc
    Dp, Db = DEPTH, DEC_BATCH
    return {
        "x_prompt": nrm(ks[0], (BATCH, SEQ, D_MODEL)),
        "x_sample": nrm(ks[1], (DEC_BATCH, DEC_SEQ, D_MODEL)),
        "state_pool": nrm(ks[2], (Dp, Db, POOL_BUF, POOL_WIDTH)),
        "state_mlstm_C": nrm(ks[3], (Dp, Db, MLSTM_HEADS, MLSTM_HEAD_DIM, MLSTM_HEAD_DIM)),
        "state_mlstm_n": nrm(ks[4], (Dp, Db, MLSTM_HEADS, MLSTM_HEAD_DIM)),
        "state_mlstm_m": nrm(ks[5], (Dp, Db, MLSTM_HEADS)),
        "cache_mem_k": nrm(ks[6], (Dp, Db, MEM_TOKENS, MEM_HEADS, MEM_HEAD_DIM)),
        "cache_mem_v": nrm(ks[7], (Dp, Db, MEM_TOKENS, MEM_HEADS, MEM_HEAD_DIM)),
        "mem_prompt": nrm(ks[8], (BATCH, MEM_TOKENS, D_MODEL)),
        "g_pre": 1.0 + nrm(ks[9], (Dp, D_MODEL), 0.1),
        "g_post": 1.0 + nrm(ks[10], (Dp, D_MODEL), 0.1),
        "w_in": nrm(ks[11], (Dp, D_MODEL, N_IN), D_MODEL ** -0.5),
        "b_mlstm_i": nrm(ks[12], (Dp, MLSTM_HEADS), 0.1),
        "b_mlstm_f": jnp.linspace(3.0, 6.0, MLSTM_HEADS, dtype=f32)[None, :] + nrm(ks[13], (Dp, MLSTM_HEADS), 0.1),
        "w_pool_grp": nrm(ks[14], (Dp, POOL_GROUPS, POOL_GROUP_WIDTH, POOL_GROUP_WIDTH), POOL_GROUP_WIDTH ** -0.5),
        "pool_scale": 1.0 + nrm(ks[15], (Dp, POOL_WIDTH), 0.1),
        "g_mem": 1.0 + nrm(ks[16], (Dp, D_MODEL), 0.1),
        "w_mem_kv": nrm(ks[17], (Dp, D_MODEL, 2 * MEM_WIDTH), D_MODEL ** -0.5),
        "w_br_pool": nrm(ks[18], (Dp, POOL_WIDTH, D_MODEL), POOL_WIDTH ** -0.5),
        "w_br_mlstm": nrm(ks[19], (Dp, MLSTM_WIDTH, D_MODEL), MLSTM_WIDTH ** -0.5),
        "w_br_mem": nrm(ks[20], (Dp, MEM_WIDTH, D_MODEL), MEM_WIDTH ** -0.5),
        "w_out": nrm(ks[21], (Dp, D_MODEL, D_MODEL), D_MODEL ** -0.5),
    }


def reference(x_prompt, x_sample, state_pool, state_mlstm_C, state_mlstm_n, state_mlstm_m, cache_mem_k,
              cache_mem_v, mem_prompt, g_pre, g_post, w_in, b_mlstm_i, b_mlstm_f, w_pool_grp, pool_scale,
              g_mem, w_mem_kv, w_br_pool, w_br_mlstm, w_br_mem, w_out):
    xp, xs = x_prompt, x_sample
    Bp = x_prompt.shape[0]
    dt = x_prompt.dtype
    pool_p, C_p, n_p, m_p, mk_p, mv_p = [], [], [], [], [], []
    pool_s, C_s, n_s, m_s = [], [], [], []
    for l in range(DEPTH):
        wl = (g_pre[l], g_post[l], w_in[l], b_mlstm_i[l], b_mlstm_f[l], w_pool_grp[l], pool_scale[l],
              w_br_pool[l], w_br_mlstm[l], w_br_mem[l], w_out[l])
        mk, mv = mem_kv(mem_prompt, g_mem[l], w_mem_kv[l])
        xp, pb, C, n, m = mixer_layer(
            xp, 0,
            jnp.zeros((Bp, POOL_BUF, POOL_WIDTH), dt),
            jnp.zeros((Bp, MLSTM_HEADS, MLSTM_HEAD_DIM, MLSTM_HEAD_DIM), dt),
            jnp.zeros((Bp, MLSTM_HEADS, MLSTM_HEAD_DIM), dt),
            jnp.zeros((Bp, MLSTM_HEADS), dt),
            mk, mv, *wl)
        pool_p.append(pb); C_p.append(C); n_p.append(n); m_p.append(m); mk_p.append(mk); mv_p.append(mv)
        xs, pb, C, n, m = mixer_layer(
            xs, PAST_LEN, state_pool[l], state_mlstm_C[l], state_mlstm_n[l], state_mlstm_m[l],
            cache_mem_k[l], cache_mem_v[l], *wl)
        pool_s.append(pb); C_s.append(C); n_s.append(n); m_s.append(m)
    return (xp, xs,
            jnp.stack(pool_p), jnp.stack(C_p), jnp.stack(n_p), jnp.stack(m_p), jnp.stack(mk_p), jnp.stack(mv_p),
            jnp.stack(pool_s), jnp.stack(C_s), jnp.stack(n_s), jnp.stack(m_s))
```

```python
import functools

import jax
import jax.numpy as jnp
from jax import lax
from jax.experimental import pallas as pl
from jax.experimental.pallas import tpu as pltpu

F32 = jnp.float32
BF16 = jnp.bfloat16

D_MODEL = 2048
PAST_LEN = 16384
POOL_WINDOWS = (2, 4, 8, 16)
POOL_GROUP_WIDTH = 256
POOL_WIDTH = 1024
POOL_BUF = 15
HEADS = 4
MLSTM_HEAD_DIM = 512
MLSTM_WIDTH = 2048
MLSTM_CHUNK = 128
MEM_TOKENS = 256
MEM_HEAD_DIM = 256
MEM_WIDTH = 1024
EPS = 1e-6
N_GATE_COLS = 2 * HEADS
GATE_COL0 = 2 * POOL_WIDTH + 5 * MLSTM_WIDTH
LANES = 128
COL_PV, COL_PZ = 0, 1024
COL_Q, COL_K, COL_V, COL_O, COL_Z = 2048, 4096, 6144, 8192, 10240
COL_CQ, COL_CZ = 12288, 13312
COL_GA, COL_GB, COL_GC = 14336, 16384, 18432
N_PROJ = 20480
VMEM_LIMIT = 56 * 1024 * 1024


def _cparams(sem):
    return pltpu.CompilerParams(dimension_semantics=sem, vmem_limit_bytes=VMEM_LIMIT)


def _sigmoid(x):
    return 1.0 / (1.0 + jnp.exp(-x))


def _silu(x):
    return x * _sigmoid(x)


def _norm_kernel(x_ref, g_ref, o_ref):
    x = x_ref[...]
    ms = jnp.mean(x * x, axis=-1, keepdims=True)
    o_ref[...] = (x * lax.rsqrt(ms + EPS) * g_ref[...]).astype(o_ref.dtype)


def _rmsnorm_bf16(x2d, g_row, tm):
    t, d = x2d.shape
    return pl.pallas_call(
        _norm_kernel,
        grid=(t // tm,),
        in_specs=[pl.BlockSpec((tm, d), lambda i: (i, 0)),
                  pl.BlockSpec((1, d), lambda i: (0, 0))],
        out_specs=pl.BlockSpec((tm, d), lambda i: (i, 0)),
        out_shape=jax.ShapeDtypeStruct((t, d), BF16),
        compiler_params=_cparams(("parallel",)),
        name="rmsnorm",
    )(x2d, g_row)


def _inproj_kernel(h_ref, w_ref, wg_ref, o_ref, g_ref):
    o_ref[...] = jnp.dot(h_ref[...], w_ref[...], preferred_element_type=F32)

    @pl.when(pl.program_id(1) == 0)
    def _():
        g_ref[...] = jnp.dot(h_ref[...], wg_ref[...], preferred_element_type=F32)


def _inproj(h, w, wg, tm, tn):
    t, d = h.shape
    n = w.shape[1]
    return pl.pallas_call(
        _inproj_kernel,
        grid=(t // tm, n // tn),
        in_specs=[pl.BlockSpec((tm, d), lambda i, j: (i, 0)),
                  pl.BlockSpec((d, tn), lambda i, j: (0, j)),
                  pl.BlockSpec((d, LANES), lambda i, j: (0, 0))],
        out_specs=[pl.BlockSpec((tm, tn), lambda i, j: (i, j)),
                   pl.BlockSpec((tm, LANES), lambda i, j: (i, 0))],
        out_shape=[jax.ShapeDtypeStruct((t, n), F32),
                   jax.ShapeDtypeStruct((t, LANES), F32)],
        compiler_params=_cparams(("parallel", "arbitrary")),
        name="inproj",
    )(h, w, wg)


def _gate_kernel(graw_ref, bias_ref, m0_ref, g_ref, gt_ref, *, seq_len):
    x = graw_ref[...] + bias_ref[...]
    lf = jnp.minimum(x, 0.0) - jnp.log1p(jnp.exp(-jnp.abs(x)))
    row = lax.broadcasted_iota(jnp.int32, (LANES, LANES), 0)
    col = lax.broadcasted_iota(jnp.int32, (LANES, LANES), 1)
    same = col <= row
    if seq_len < LANES:
        shift = seq_len.bit_length() - 1
        same = same & (lax.shift_right_logical(col, shift) == lax.shift_right_logical(row, shift))
    cums = jnp.dot(same.astype(F32), lf, precision=lax.Precision.HIGHEST, preferred_element_type=F32)
    g = jnp.where(col < HEADS, x, jnp.where(col < 2 * HEADS, cums, m0_ref[...]))
    g_ref[...] = g
    gt_ref[...] = g.T[:8, :]


def _gate_prep(graw, bias_row, m0_rows, seq_len):
    t = graw.shape[0]
    return pl.pallas_call(
        functools.partial(_gate_kernel, seq_len=seq_len),
        grid=(t // LANES,),
        in_specs=[pl.BlockSpec((LANES, LANES), lambda i: (i, 0)),
                  pl.BlockSpec((1, LANES), lambda i: (0, 0)),
                  pl.BlockSpec((LANES, LANES), lambda i: (i, 0))],
        out_specs=[pl.BlockSpec((LANES, LANES), lambda i: (i, 0)),
                   pl.BlockSpec((8, LANES), lambda i: (0, i))],
        out_shape=[jax.ShapeDtypeStruct((t, LANES), F32),
                   jax.ShapeDtypeStruct((8, t), F32)],
        compiler_params=_cparams(("parallel",)),
        name="gate_prep",
    )(graw, bias_row, m0_rows)


def _pool_kernel(pv_ref, pz_ref, buf_ref, wg_ref, scale_ref, xa_ref, nbuf_ref, s_ref, p_ref,
                 *, nseq, tm, nt, start):
    j = pl.program_id(1)

    def one_seq(b, r0):
        if nt > 1:
            @pl.when(j == 0)
            def _():
                s_ref[1:16, :] = buf_ref[b]
        else:
            s_ref[1:16, :] = buf_ref[b]
        s_ref[16:16 + tm, :] = pv_ref[pl.ds(r0, tm), :]
        pos = start + j * tm + lax.broadcasted_iota(jnp.int32, (tm, 1), 0)
        for g, w in enumerate(POOL_WINDOWS):
            cs = slice(g * POOL_GROUP_WIDTH, (g + 1) * POOL_GROUP_WIDTH)
            acc = s_ref[16:16 + tm, cs]
            for i in range(1, w):
                acc = acc + s_ref[16 - i:16 - i + tm, cs]
            cnt = jnp.minimum(w, pos + 1).astype(F32)
            p_ref[pl.ds(r0, tm), cs] = acc / cnt - s_ref[16:16 + tm, cs]
        tail = s_ref[tm + 1:tm + 16, :]
        nbuf_ref[b] = tail
        if nt > 1:
            s_ref[1:16, :] = tail

    if nseq == 1:
        one_seq(0, 0)
    else:
        def body(b, carry):
            one_seq(b, pl.multiple_of(b * tm, 8))
            return carry
        lax.fori_loop(0, nseq, body, 0)

    for g in range(len(POOL_WINDOWS)):
        cs = slice(g * POOL_GROUP_WIDTH, (g + 1) * POOL_GROUP_WIDTH)
        pa = jnp.dot(p_ref[:, cs].astype(BF16), wg_ref[g], preferred_element_type=F32) * scale_ref[:, cs]
        xa_ref[:, cs] = (pa * _silu(pz_ref[:, cs])).astype(xa_ref.dtype)


def _pool_branch(proj, buf, w_grp, scale_row, nb, seq_len, nseq, tm, start):
    assert nseq == 1 or tm == seq_len
    nt = seq_len // tm
    rows = nseq * tm
    t = nb * seq_len
    pvb, pzb = COL_PV // POOL_WIDTH, COL_PZ // POOL_WIDTH
    return pl.pallas_call(
        functools.partial(_pool_kernel, nseq=nseq, tm=tm, nt=nt, start=start),
        grid=(nb // nseq, nt),
        in_specs=[pl.BlockSpec((rows, POOL_WIDTH), lambda i, j: (i * nt + j, pvb)),
                  pl.BlockSpec((rows, POOL_WIDTH), lambda i, j: (i * nt + j, pzb)),
                  pl.BlockSpec((nseq, POOL_BUF, POOL_WIDTH), lambda i, j: (i, 0, 0)),
                  pl.BlockSpec((4, POOL_GROUP_WIDTH, POOL_GROUP_WIDTH), lambda i, j: (0, 0, 0)),
                  pl.BlockSpec((1, POOL_WIDTH), lambda i, j: (0, 0))],
        out_specs=[pl.BlockSpec((rows, POOL_WIDTH), lambda i, j: (i * nt + j, 0)),
                   pl.BlockSpec((nseq, POOL_BUF, POOL_WIDTH), lambda i, j: (i, 0, 0))],
        out_shape=[jax.ShapeDtypeStruct((t, POOL_WIDTH), BF16),
                   jax.ShapeDtypeStruct((nb, POOL_BUF, POOL_WIDTH), F32)],
        scratch_shapes=[pltpu.VMEM((16 + tm, POOL_WIDTH), F32),
                        pltpu.VMEM((rows, POOL_WIDTH), F32)],
        compiler_params=_cparams(("arbitrary", "arbitrary")),
        name="pool_branch",
    )(proj, proj, buf, w_grp, scale_row)


def _memkv_kernel(m_ref, g_ref, w_ref, k_ref, v_ref):
    x = m_ref[...]
    ms = jnp.mean(x * x, axis=-1, keepdims=True)
    h = (x * lax.rsqrt(ms + EPS) * g_ref[...]).astype(BF16)
    kv = jnp.dot(h, w_ref[...], preferred_element_type=F32)
    k_ref[...] = kv[:, :MEM_WIDTH]
    v_ref[...] = kv[:, MEM_WIDTH:]


def _mem_kv(mem2d, g_row, w_bf, tm):
    t, d = mem2d.shape
    return pl.pallas_call(
        _memkv_kernel,
        grid=(t // tm,),
        in_specs=[pl.BlockSpec((tm, d), lambda i: (i, 0)),
                  pl.BlockSpec((1, d), lambda i: (0, 0)),
                  pl.BlockSpec((d, 2 * MEM_WIDTH), lambda i: (0, 0))],
        out_specs=[pl.BlockSpec((tm, MEM_WIDTH), lambda i: (i, 0)),
                   pl.BlockSpec((tm, MEM_WIDTH), lambda i: (i, 0))],
        out_shape=[jax.ShapeDtypeStruct((t, MEM_WIDTH), F32),
                   jax.ShapeDtypeStruct((t, MEM_WIDTH), F32)],
        compiler_params=_cparams(("parallel",)),
        name="mem_kv",
    )(mem2d, g_row, w_bf)


def _attn_kernel(cq_ref, cz_ref, k_ref, v_ref, xc_ref, *, nseq, tm):
    scale = MEM_HEAD_DIM ** -0.5
    for b in range(nseq):
        rs = slice(b * tm, (b + 1) * tm)
        for h in range(HEADS):
            cs = slice(h * MEM_HEAD_DIM, (h + 1) * MEM_HEAD_DIM)
            q = cq_ref[rs, cs].astype(BF16)
            k = k_ref[b, :, cs].astype(BF16)
            v = v_ref[b, :, cs].astype(BF16)
            s = lax.dot_general(q, k, (((1,), (1,)), ((), ())), preferred_element_type=F32) * scale
            p = jnp.exp(s - jnp.max(s, axis=-1, keepdims=True))
            a = p / jnp.sum(p, axis=-1, keepdims=True)
            o = jnp.dot(a.astype(BF16), v, preferred_element_type=F32)
            xc_ref[rs, cs] = (o * _silu(cz_ref[rs, cs])).astype(xc_ref.dtype)


def _mem_attn(proj, mem_k, mem_v, nb, seq_len, nseq, tm):
    assert nseq == 1 or tm == seq_len
    nt = seq_len // tm
    rows = nseq * tm
    t = nb * seq_len
    cqb, czb = COL_CQ // MEM_WIDTH, COL_CZ // MEM_WIDTH
    return pl.pallas_call(
        functools.partial(_attn_kernel, nseq=nseq, tm=tm),
        grid=(nb // nseq, nt),
        in_specs=[pl.BlockSpec((rows, MEM_WIDTH), lambda i, j: (i * nt + j, cqb)),
                  pl.BlockSpec((rows, MEM_WIDTH), lambda i, j: (i * nt + j, czb)),
                  pl.BlockSpec((nseq, MEM_TOKENS, MEM_WIDTH), lambda i, j: (i, 0, 0)),
                  pl.BlockSpec((nseq, MEM_TOKENS, MEM_WIDTH), lambda i, j: (i, 0, 0))],
        out_specs=pl.BlockSpec((rows, MEM_WIDTH), lambda i, j: (i * nt + j, 0)),
        out_shape=jax.ShapeDtypeStruct((t, MEM_WIDTH), BF16),
        compiler_params=_cparams(("parallel", "arbitrary")),
        name="mem_attn",
    )(proj, proj, mem_k, mem_v)


def _mlstm_intra(q_bf, ks_bf, v_bf, ig_row, b_row, b_col, m0_col, mask):
    dlog = jnp.where(mask, b_col - b_row + ig_row, -jnp.inf)
    inter = b_col + m0_col
    m_col = jnp.maximum(inter, jnp.max(dlog, axis=1, keepdims=True))
    dw = jnp.exp(dlog - m_col)
    inter_w = jnp.exp(inter - m_col)
    s = lax.dot_general(q_bf, ks_bf, (((1,), (1,)), ((), ())), preferred_element_type=F32) * dw
    num = jnp.dot(s.astype(BF16), v_bf, preferred_element_type=F32)
    den = jnp.sum(s, axis=1, keepdims=True)
    return num, den, m_col, inter_w


def _mlstm_prompt_kernel(q_ref, k_ref, v_ref, o_ref, z_ref, g_ref, gt_ref,
                         xb_ref, c_ref, n_ref, mcol_ref, m_scr):
    @pl.when(pl.program_id(1) == 0)
    def _():
        c_ref[...] = jnp.zeros_like(c_ref)
        n_ref[...] = jnp.zeros_like(n_ref)
        m_scr[...] = jnp.zeros_like(m_scr)

    cl = MLSTM_CHUNK
    row = lax.broadcasted_iota(jnp.int32, (cl, cl), 0)
    col = lax.broadcasted_iota(jnp.int32, (cl, cl), 1)
    mask = col <= row
    kscale = MLSTM_HEAD_DIM ** -0.5
    for h in range(HEADS):
        cs = slice(h * MLSTM_HEAD_DIM, (h + 1) * MLSTM_HEAD_DIM)
        ig_col = g_ref[:, h:h + 1]
        b_col = g_ref[:, HEADS + h:HEADS + h + 1]
        ig_row = gt_ref[h:h + 1, :]
        b_row = gt_ref[HEADS + h:HEADS + h + 1, :]
        m0 = m_scr[h:h + 1, 0:1]
        qf = q_ref[:, cs]
        kf = k_ref[:, cs] * kscale
        vf = v_ref[:, cs]
        q_bf, ks_bf = qf.astype(BF16), kf.astype(BF16)
        num, den, m_col, inter_w = _mlstm_intra(q_bf, ks_bf, vf.astype(BF16), ig_row, b_row, b_col, m0, mask)
        c0 = c_ref[0, h]
        n0 = n_ref[0, h:h + 1, :]
        num = num + inter_w * lax.dot_general(q_bf, c0.astype(BF16), (((1,), (1,)), ((), ())),
                                              preferred_element_type=F32)
        den = den + inter_w * jnp.sum(qf * n0, axis=1, keepdims=True)
        hval = num / jnp.maximum(jnp.abs(den), jnp.exp(-m_col))
        xb_ref[:, cs] = (_sigmoid(o_ref[:, cs]) * hval * _silu(z_ref[:, cs])).astype(xb_ref.dtype)
        m_new = m_col[cl - 1:cl, :]
        b_last = b_col[cl - 1:cl, :]
        w_col = jnp.exp(b_last - b_col + ig_col - m_new)
        decay = jnp.exp(b_last + m0 - m_new)
        dc = lax.dot_general((w_col * vf).astype(BF16), ks_bf, (((0,), (0,)), ((), ())),
                             preferred_element_type=F32)
        c_ref[0, h] = decay * c0 + dc
        n_ref[0, h:h + 1, :] = decay * n0 + jnp.sum(w_col * kf, axis=0, keepdims=True)
        m_scr[h:h + 1, :] = jnp.broadcast_to(m_new, (1, LANES))
        mcol_ref[:, h:h + 1] = m_col


def _mlstm_prompt(proj, g, gt, nb, seq_len):
    cl = MLSTM_CHUNK
    nc = seq_len // cl
    t = nb * seq_len
    w = MLSTM_WIDTH

    def colspec(col0):
        return pl.BlockSpec((cl, w), lambda b, c: (b * nc + c, col0 // w))

    return pl.pallas_call(
        _mlstm_prompt_kernel,
        grid=(nb, nc),
        in_specs=[colspec(COL_Q), colspec(COL_K), colspec(COL_V), colspec(COL_O), colspec(COL_Z),
                  pl.BlockSpec((cl, LANES), lambda b, c: (b * nc + c, 0)),
                  pl.BlockSpec((8, cl), lambda b, c: (0, b * nc + c))],
        out_specs=[pl.BlockSpec((cl, w), lambda b, c: (b * nc + c, 0)),
                   pl.BlockSpec((1, HEADS, MLSTM_HEAD_DIM, MLSTM_HEAD_DIM), lambda b, c: (b, 0, 0, 0)),
                   pl.BlockSpec((1, HEADS, MLSTM_HEAD_DIM), lambda b, c: (b, 0, 0)),
                   pl.BlockSpec((cl, LANES), lambda b, c: (b * nc + c, 0))],
        out_shape=[jax.ShapeDtypeStruct((t, w), BF16),
                   jax.ShapeDtypeStruct((nb, HEADS, MLSTM_HEAD_DIM, MLSTM_HEAD_DIM), F32),
                   jax.ShapeDtypeStruct((nb, HEADS, MLSTM_HEAD_DIM), F32),
                   jax.ShapeDtypeStruct((t, LANES), F32)],
        scratch_shapes=[pltpu.VMEM((8, LANES), F32)],
        compiler_params=_cparams(("arbitrary", "arbitrary")),
        name="mlstm_prompt",
    )(proj, proj, proj, proj, proj, g, gt)


def _mlstm_sample_kernel(q_ref, k_ref, v_ref, o_ref, z_ref, g_ref, gt_ref, c0_ref, n0_ref,
                         xb_ref, c_ref, n_ref, mcol_ref, num_scr, den_scr, iw_scr, *, seq_len):
    bl = pl.program_id(1)
    tl = LANES
    kscale = MLSTM_HEAD_DIM ** -0.5
    shift = seq_len.bit_length() - 1

    @pl.when(bl == 0)
    def _():
        row = lax.broadcasted_iota(jnp.int32, (tl, tl), 0)
        col = lax.broadcasted_iota(jnp.int32, (tl, tl), 1)
        mask = (col <= row) & (lax.shift_right_logical(col, shift) == lax.shift_right_logical(row, shift))
        for h in range(HEADS):
            cs = slice(h * MLSTM_HEAD_DIM, (h + 1) * MLSTM_HEAD_DIM)
            b_col = g_ref[:, HEADS + h:HEADS + h + 1]
            m0_col = g_ref[:, 2 * HEADS + h:2 * HEADS + h + 1]
            ig_row = gt_ref[h:h + 1, :]
            b_row = gt_ref[HEADS + h:HEADS + h + 1, :]
            q_bf = q_ref[:, cs].astype(BF16)
            ks_bf = (k_ref[:, cs] * kscale).astype(BF16)
            num, den, m_col, inter_w = _mlstm_intra(q_bf, ks_bf, v_ref[:, cs].astype(BF16),
                                                    ig_row, b_row, b_col, m0_col, mask)
            num_scr[:, cs] = num
            den_scr[:, h:h + 1] = den
            iw_scr[:, h:h + 1] = inter_w
            mcol_ref[:, h:h + 1] = m_col

    r0 = pl.multiple_of(bl * seq_len, 8)
    rows = pl.ds(r0, seq_len)
    last = pl.ds(r0 + seq_len - 1, 1)
    for h in range(HEADS):
        cs = slice(h * MLSTM_HEAD_DIM, (h + 1) * MLSTM_HEAD_DIM)
        qf = q_ref[rows, cs]
        kf = k_ref[rows, cs] * kscale
        vf = v_ref[rows, cs]
        ig_col = g_ref[rows, h:h + 1]
        b_col = g_ref[rows, HEADS + h:HEADS + h + 1]
        b_last = g_ref[last, HEADS + h:HEADS + h + 1]
        m0 = g_ref[last, 2 * HEADS + h:2 * HEADS + h + 1]
        m_col = mcol_ref[rows, h:h + 1]
        m_new = mcol_ref[last, h:h + 1]
        inter_w = iw_scr[rows, h:h + 1]
        c0 = c0_ref[0, h]
        n0 = n0_ref[0, h:h + 1, :]
        num = num_scr[rows, cs] + inter_w * lax.dot_general(
            qf.astype(BF16), c0.astype(BF16), (((1,), (1,)), ((), ())), preferred_element_type=F32)
        den = den_scr[rows, h:h + 1] + inter_w * jnp.sum(qf * n0, axis=1, keepdims=True)
        hval = num / jnp.maximum(jnp.abs(den), jnp.exp(-m_col))
        xb_ref[rows, cs] = (_sigmoid(o_ref[rows, cs]) * hval * _silu(z_ref[rows, cs])).astype(xb_ref.dtype)
        w_col = jnp.exp(b_last - b_col + ig_col - m_new)
        decay = jnp.exp(b_last + m0 - m_new)
        dc = lax.dot_general((w_col * vf).astype(BF16), kf.astype(BF16), (((0,), (0,)), ((), ())),
                             preferred_element_type=F32)
        c_ref[0, h] = decay * c0 + dc
        n_ref[0, h:h + 1, :] = decay * n0 + jnp.sum(w_col * kf, axis=0, keepdims=True)


def _mlstm_sample(proj, g, gt, c0, n0, nb, seq_len):
    tl = LANES
    per = tl // seq_len
    ng = nb // per
    t = nb * seq_len
    w = MLSTM_WIDTH
    d = MLSTM_HEAD_DIM

    def colspec(col0):
        return pl.BlockSpec((tl, w), lambda i, b: (i, col0 // w))

    state4 = pl.BlockSpec((1, HEADS, d, d), lambda i, b: (i * per + b, 0, 0, 0))
    state3 = pl.BlockSpec((1, HEADS, d), lambda i, b: (i * per + b, 0, 0))
    return pl.pallas_call(
        functools.partial(_mlstm_sample_kernel, seq_len=seq_len),
        grid=(ng, per),
        in_specs=[colspec(COL_Q), colspec(COL_K), colspec(COL_V), colspec(COL_O), colspec(COL_Z),
                  pl.BlockSpec((tl, LANES), lambda i, b: (i, 0)),
                  pl.BlockSpec((8, tl), lambda i, b: (0, i)),
                  state4, state3],
        out_specs=[pl.BlockSpec((tl, w), lambda i, b: (i, 0)), state4, state3,
                   pl.BlockSpec((tl, LANES), lambda i, b: (i, 0))],
        out_shape=[jax.ShapeDtypeStruct((t, w), F32),
                   jax.ShapeDtypeStruct((nb, HEADS, d, d), F32),
                   jax.ShapeDtypeStruct((nb, HEADS, d), F32),
                   jax.ShapeDtypeStruct((t, LANES), F32)],
        scratch_shapes=[pltpu.VMEM((tl, w), F32), pltpu.VMEM((tl, LANES), F32), pltpu.VMEM((tl, LANES), F32)],
        compiler_params=_cparams(("arbitrary", "arbitrary")),
        name="mlstm_sample",
    )(proj, proj, proj, proj, proj, g, gt, c0, n0)


def _final_kernel(xa_ref, xb_ref, xc_ref, ga_ref, gb_ref, gc_ref, x_ref, gpost_ref,
                  wa_ref, wb_ref, wc_ref, wo_ref, y_ref):
    ya = jnp.dot(xa_ref[...].astype(BF16), wa_ref[...], preferred_element_type=F32)
    merged = _sigmoid(ga_ref[...]) * ya
    yb = jnp.dot(xb_ref[...].astype(BF16), wb_ref[...], preferred_element_type=F32)
    merged = merged + _sigmoid(gb_ref[...]) * yb
    yc = jnp.dot(xc_ref[...].astype(BF16), wc_ref[...], preferred_element_type=F32)
    merged = merged + _sigmoid(gc_ref[...]) * yc
    out = jnp.dot(merged.astype(BF16), wo_ref[...], preferred_element_type=F32)
    ms = jnp.mean(out * out, axis=-1, keepdims=True)
    y_ref[...] = x_ref[...] + out * lax.rsqrt(ms + EPS) * gpost_ref[...]


def _final(xa, xb, xc, proj, x2d, gpost_row, wa, wb, wc, wo, tm):
    t, d = x2d.shape

    def rowspec(width, colblk=0):
        return pl.BlockSpec((tm, width), lambda i: (i, colblk))

    def resident(shape):
        return pl.BlockSpec(shape, lambda i: (0, 0), pipeline_mode=pl.Buffered(1))

    return pl.pallas_call(
        _final_kernel,
        grid=(t // tm,),
        in_specs=[rowspec(POOL_WIDTH), rowspec(MLSTM_WIDTH), rowspec(MEM_WIDTH),
                  rowspec(d, COL_GA // d), rowspec(d, COL_GB // d), rowspec(d, COL_GC // d),
                  rowspec(d), resident((1, d)),
                  resident(wa.shape), resident(wb.shape), resident(wc.shape), resident(wo.shape)],
        out_specs=rowspec(d),
        out_shape=jax.ShapeDtypeStruct((t, d), F32),
        compiler_params=_cparams(("parallel",)),
        name="merge_out",
    )(xa, xb, xc, proj, proj, proj, x2d, gpost_row, wa, wb, wc, wo)


def _mixer_group(x, pool_buf, c0, n0, m0, mem_k, mem_v, wts, start):
    nb, seq_len, d = x.shape
    t = nb * seq_len
    x2d = x.reshape(t, d)
    h = _rmsnorm_bf16(x2d, wts["g_pre"], tm=512)
    proj, graw = _inproj(h, wts["w_in"], wts["w_gate"], tm=min(t, 1024), tn=1024)

    fresh = c0 is None
    if fresh:
        m0_rows = jnp.zeros((t, LANES), F32)
    else:
        m0_rows = jnp.pad(jnp.repeat(m0, seq_len, axis=0), ((0, 0), (2 * HEADS, LANES - 3 * HEADS)))
    g, gt = _gate_prep(graw, wts["gate_bias"], m0_rows, min(seq_len, MLSTM_CHUNK))

    if seq_len >= 512:
        xa, new_buf = _pool_branch(proj, pool_buf, wts["w_pool_grp"], wts["pool_scale"], nb, seq_len,
                                   nseq=1, tm=512, start=start)
        xc = _mem_attn(proj, mem_k, mem_v, nb, seq_len, nseq=1, tm=512)
    else:
        xa, new_buf = _pool_branch(proj, pool_buf, wts["w_pool_grp"], wts["pool_scale"], nb, seq_len,
                                   nseq=32, tm=seq_len, start=start)
        xc = _mem_attn(proj, mem_k, mem_v, nb, seq_len, nseq=4, tm=seq_len)

    if fresh:
        xb, c_new, n_new, mcol = _mlstm_prompt(proj, g, gt, nb, seq_len)
    else:
        xb, c_new, n_new, mcol = _mlstm_sample(proj, g, gt, c0, n0, nb, seq_len)
    m_new = mcol.reshape(nb, seq_len, LANES)[:, seq_len - 1, :HEADS]

    y = _final(xa, xb, xc, proj, x2d, wts["g_post"], wts["w_br_pool"], wts["w_br_mlstm"],
               wts["w_br_mem"], wts["w_out"], tm=256)
    return y.reshape(nb, seq_len, d), new_buf, c_new, n_new, m_new


def kernel(x_prompt, x_sample, state_pool, state_mlstm_C, state_mlstm_n, state_mlstm_m, cache_mem_k,
           cache_mem_v, mem_prompt, g_pre, g_post, w_in, b_mlstm_i, b_mlstm_f, w_pool_grp, pool_scale,
           g_mem, w_mem_kv, w_br_pool, w_br_mlstm, w_br_mem, w_out):
    nbp = x_prompt.shape[0]
    l = 0
    w_in_l = w_in[l]
    gate_end = GATE_COL0 + N_GATE_COLS
    wts = {
        "g_pre": g_pre[l][None, :],
        "g_post": g_post[l][None, :],
        "w_in": jnp.concatenate([w_in_l[:, :GATE_COL0], w_in_l[:, gate_end:]], axis=1).astype(BF16),
        "w_gate": jnp.pad(w_in_l[:, GATE_COL0:gate_end], ((0, 0), (0, LANES - N_GATE_COLS))).astype(BF16),
        "gate_bias": jnp.pad(jnp.concatenate([b_mlstm_i[l], b_mlstm_f[l]]), (0, LANES - N_GATE_COLS))[None, :],
        "w_pool_grp": w_pool_grp[l].astype(BF16),
        "pool_scale": pool_scale[l][None, :],
        "w_br_pool": w_br_pool[l].astype(BF16),
        "w_br_mlstm": w_br_mlstm[l].astype(BF16),
        "w_br_mem": w_br_mem[l].astype(BF16),
        "w_out": w_out[l].astype(BF16),
    }
    mk2d, mv2d = _mem_kv(mem_prompt.reshape(nbp * MEM_TOKENS, D_MODEL), g_mem[l][None, :],
                         w_mem_kv[l].astype(BF16), tm=256)
    mk3, mv3 = mk2d.reshape(nbp, MEM_TOKENS, MEM_WIDTH), mv2d.reshape(nbp, MEM_TOKENS, MEM_WIDTH)
    yp, pool_p, c_p, n_p, m_p = _mixer_group(
        x_prompt, jnp.zeros((nbp, POOL_BUF, POOL_WIDTH), F32), None, None, None, mk3, mv3, wts, 0)
    nbs = x_sample.shape[0]
    ys, pool_s, c_s, n_s, m_s = _mixer_group(
        x_sample, state_pool[l], state_mlstm_C[l], state_mlstm_n[l], state_mlstm_m[l],
        cache_mem_k[l].reshape(nbs, MEM_TOKENS, MEM_WIDTH), cache_mem_v[l].reshape(nbs, MEM_TOKENS, MEM_WIDTH),
        wts, PAST_LEN)
    kv_shape = (1, nbp, MEM_TOKENS, HEADS, MEM_HEAD_DIM)
    return (yp, ys, pool_p[None], c_p[None], n_p[None], m_p[None],
            mk2d.reshape(kv_shape), mv2d.reshape(kv_shape),
            pool_s[None], c_s[None], n_s[None], m_s[None])
```

```python
import functools

import jax
import jax.numpy as jnp
from jax import lax
from jax.experimental import pallas as pl
from jax.experimental.pallas import tpu as pltpu

F32 = jnp.float32
BF16 = jnp.bfloat16

D_MODEL = 2048
PAST_LEN = 16384
POOL_WINDOWS = (2, 4, 8, 16)
POOL_GROUP_WIDTH = 256
POOL_WIDTH = 1024
POOL_BUF = 15
HEADS = 4
MLSTM_HEAD_DIM = 512
MLSTM_WIDTH = 2048
MLSTM_CHUNK = 128
MEM_TOKENS = 256
MEM_HEAD_DIM = 256
MEM_WIDTH = 1024
EPS = 1e-6
N_GATE_COLS = 2 * HEADS
GATE_COL0 = 2 * POOL_WIDTH + 5 * MLSTM_WIDTH
LANES = 128
COL_PV, COL_PZ = 0, 1024
COL_Q, COL_K, COL_V, COL_O, COL_Z = 2048, 4096, 6144, 8192, 10240
COL_CQ, COL_CZ = 0, 1024
COL_GA, COL_GB, COL_GC = 2048, 4096, 6144
N_PROJ_A = GATE_COL0
N_PROJ_B = 2 * MEM_WIDTH + 3 * D_MODEL
VMEM_LIMIT = 56 * 1024 * 1024


def _cparams(sem):
    return pltpu.CompilerParams(dimension_semantics=sem, vmem_limit_bytes=VMEM_LIMIT)


def _sigmoid(x):
    return 1.0 / (1.0 + jnp.exp(-x))


def _silu(x):
    return x * _sigmoid(x)


def _norm_kernel(x_ref, g_ref, o_ref):
    x = x_ref[...]
    ms = jnp.mean(x * x, axis=-1, keepdims=True)
    o_ref[...] = (x * lax.rsqrt(ms + EPS) * g_ref[...]).astype(o_ref.dtype)


def _rmsnorm_bf16(x2d, g_row, tm):
    t, d = x2d.shape
    return pl.pallas_call(
        _norm_kernel,
        grid=(t // tm,),
        in_specs=[pl.BlockSpec((tm, d), lambda i: (i, 0)),
                  pl.BlockSpec((1, d), lambda i: (0, 0))],
        out_specs=pl.BlockSpec((tm, d), lambda i: (i, 0)),
        out_shape=jax.ShapeDtypeStruct((t, d), BF16),
        compiler_params=_cparams(("parallel",)),
        name="rmsnorm",
    )(x2d, g_row)


def _inproj_kernel(h_ref, w_ref, o_ref, wbf_ref):
    @pl.when(pl.program_id(1) == 0)
    def _():
        wbf_ref[...] = w_ref[...].astype(BF16)

    o_ref[...] = jnp.dot(h_ref[...], wbf_ref[...], preferred_element_type=F32)


def _inproj(h, w, n_cols, tm, tn):
    t, d = h.shape
    if w.ndim == 3:
        w_spec = pl.BlockSpec((None, d, tn), lambda j, i: (0, 0, j))
    else:
        w_spec = pl.BlockSpec((d, tn), lambda j, i: (0, j))
    return pl.pallas_call(
        _inproj_kernel,
        grid=(n_cols // tn, t // tm),
        in_specs=[pl.BlockSpec((tm, d), lambda j, i: (i, 0)), w_spec],
        out_specs=pl.BlockSpec((tm, tn), lambda j, i: (i, j)),
        out_shape=jax.ShapeDtypeStruct((t, n_cols), F32),
        scratch_shapes=[pltpu.VMEM((d, tn), BF16)],
        compiler_params=_cparams(("arbitrary", "arbitrary")),
        name="inproj",
    )(h, w)


GATE_TILE = 512


def _gate_kernel(h_ref, wg_ref, bias_ref, m0_ref, g_ref, gt_ref, *, seq_len):
    row = lax.broadcasted_iota(jnp.int32, (LANES, LANES), 0)
    col = lax.broadcasted_iota(jnp.int32, (LANES, LANES), 1)
    same = col <= row
    if seq_len < LANES:
        shift = seq_len.bit_length() - 1
        same = same & (lax.shift_right_logical(col, shift) == lax.shift_right_logical(row, shift))
    same = same.astype(F32)
    for r in range(GATE_TILE // LANES):
        rs = slice(r * LANES, (r + 1) * LANES)
        x = jnp.dot(h_ref[rs, :], wg_ref[...], preferred_element_type=F32) + bias_ref[...]
        lf = jnp.minimum(x, 0.0) - jnp.log1p(jnp.exp(-jnp.abs(x)))
        cums = jnp.dot(same, lf, precision=lax.Precision.HIGHEST, preferred_element_type=F32)
        g = jnp.where(col < HEADS, x, jnp.where(col < 2 * HEADS, cums, m0_ref[rs, :]))
        g_ref[rs, :] = g
        gt_ref[:, rs] = g.T[:8, :]


def _gate_prep(h, wg, bias_row, m0_rows, seq_len):
    t, d = h.shape
    tg = GATE_TILE
    return pl.pallas_call(
        functools.partial(_gate_kernel, seq_len=seq_len),
        grid=(t // tg,),
        in_specs=[pl.BlockSpec((tg, d), lambda i: (i, 0)),
                  pl.BlockSpec((d, LANES), lambda i: (0, 0)),
                  pl.BlockSpec((1, LANES), lambda i: (0, 0)),
                  pl.BlockSpec((tg, LANES), lambda i: (i, 0))],
        out_specs=[pl.BlockSpec((tg, LANES), lambda i: (i, 0)),
                   pl.BlockSpec((8, tg), lambda i: (0, i))],
        out_shape=[jax.ShapeDtypeStruct((t, LANES), F32),
                   jax.ShapeDtypeStruct((8, t), F32)],
        compiler_params=_cparams(("parallel",)),
        name="gate_prep",
    )(h, wg, bias_row, m0_rows)


def _pool_kernel(pv_ref, pz_ref, buf_ref, wg_ref, scale_ref, xa_ref, nbuf_ref, s_ref, p_ref,
                 *, nseq, tm, nt, start):
    j = pl.program_id(1)

    def one_seq(b, r0):
        if nt > 1:
            @pl.when(j == 0)
            def _():
                s_ref[1:16, :] = buf_ref[b]
        else:
            s_ref[1:16, :] = buf_ref[b]
        s_ref[16:16 + tm, :] = pv_ref[pl.ds(r0, tm), :]
        pos = start + j * tm + lax.broadcasted_iota(jnp.int32, (tm, 1), 0)
        for g, w in enumerate(POOL_WINDOWS):
            cs = slice(g * POOL_GROUP_WIDTH, (g + 1) * POOL_GROUP_WIDTH)
            acc = s_ref[16:16 + tm, cs]
            for i in range(1, w):
                acc = acc + s_ref[16 - i:16 - i + tm, cs]
            cnt = jnp.minimum(w, pos + 1).astype(F32)
            p_ref[pl.ds(r0, tm), cs] = acc / cnt - s_ref[16:16 + tm, cs]
        tail = s_ref[tm + 1:tm + 16, :]
        nbuf_ref[b] = tail
        if nt > 1:
            s_ref[1:16, :] = tail

    if nseq == 1:
        one_seq(0, 0)
    else:
        def body(b, carry):
            one_seq(b, pl.multiple_of(b * tm, 8))
            return carry
        lax.fori_loop(0, nseq, body, 0)

    for g in range(len(POOL_WINDOWS)):
        cs = slice(g * POOL_GROUP_WIDTH, (g + 1) * POOL_GROUP_WIDTH)
        pa = jnp.dot(p_ref[:, cs].astype(BF16), wg_ref[g], preferred_element_type=F32) * scale_ref[:, cs]
        xa_ref[:, cs] = (pa * _silu(pz_ref[:, cs])).astype(xa_ref.dtype)


def _pool_branch(proj, buf, w_grp, scale_row, nb, seq_len, nseq, tm, start):
    assert nseq == 1 or tm == seq_len
    nt = seq_len // tm
    rows = nseq * tm
    t = nb * seq_len
    pvb, pzb = COL_PV // POOL_WIDTH, COL_PZ // POOL_WIDTH
    return pl.pallas_call(
        functools.partial(_pool_kernel, nseq=nseq, tm=tm, nt=nt, start=start),
        grid=(nb // nseq, nt),
        in_specs=[pl.BlockSpec((rows, POOL_WIDTH), lambda i, j: (i * nt + j, pvb)),
                  pl.BlockSpec((rows, POOL_WIDTH), lambda i, j: (i * nt + j, pzb)),
                  pl.BlockSpec((nseq, POOL_BUF, POOL_WIDTH), lambda i, j: (i, 0, 0)),
                  pl.BlockSpec((4, POOL_GROUP_WIDTH, POOL_GROUP_WIDTH), lambda i, j: (0, 0, 0)),
                  pl.BlockSpec((1, POOL_WIDTH), lambda i, j: (0, 0))],
        out_specs=[pl.BlockSpec((rows, POOL_WIDTH), lambda i, j: (i * nt + j, 0)),
                   pl.BlockSpec((nseq, POOL_BUF, POOL_WIDTH), lambda i, j: (i, 0, 0))],
        out_shape=[jax.ShapeDtypeStruct((t, POOL_WIDTH), BF16),
                   jax.ShapeDtypeStruct((nb, POOL_BUF, POOL_WIDTH), F32)],
        scratch_shapes=[pltpu.VMEM((16 + tm, POOL_WIDTH), F32),
                        pltpu.VMEM((rows, POOL_WIDTH), F32)],
        compiler_params=_cparams(("arbitrary", "arbitrary")),
        name="pool_branch",
    )(proj, proj, buf, w_grp, scale_row)


def _memkv_kernel(m_ref, g_ref, w_ref, k_ref, v_ref):
    x = m_ref[...]
    ms = jnp.mean(x * x, axis=-1, keepdims=True)
    h = (x * lax.rsqrt(ms + EPS) * g_ref[...]).astype(BF16)
    kv = jnp.dot(h, w_ref[...], preferred_element_type=F32)
    k_ref[...] = kv[:, :MEM_WIDTH]
    v_ref[...] = kv[:, MEM_WIDTH:]


def _mem_kv(mem2d, g_row, w_bf, tm):
    t, d = mem2d.shape
    return pl.pallas_call(
        _memkv_kernel,
        grid=(t // tm,),
        in_specs=[pl.BlockSpec((tm, d), lambda i: (i, 0)),
                  pl.BlockSpec((1, d), lambda i: (0, 0)),
                  pl.BlockSpec((d, 2 * MEM_WIDTH), lambda i: (0, 0))],
        out_specs=[pl.BlockSpec((tm, MEM_WIDTH), lambda i: (i, 0)),
                   pl.BlockSpec((tm, MEM_WIDTH), lambda i: (i, 0))],
        out_shape=[jax.ShapeDtypeStruct((t, MEM_WIDTH), F32),
                   jax.ShapeDtypeStruct((t, MEM_WIDTH), F32)],
        compiler_params=_cparams(("parallel",)),
        name="mem_kv",
    )(mem2d, g_row, w_bf)


def _attn_kernel(cq_ref, cz_ref, k_ref, v_ref, xc_ref, *, nseq, tm):
    scale = MEM_HEAD_DIM ** -0.5
    for b in range(nseq):
        rs = slice(b * tm, (b + 1) * tm)
        for h in range(HEADS):
            cs = slice(h * MEM_HEAD_DIM, (h + 1) * MEM_HEAD_DIM)
            q = cq_ref[rs, cs].astype(BF16)
            k = k_ref[b, :, cs].astype(BF16)
            v = v_ref[b, :, cs].astype(BF16)
            s = lax.dot_general(q, k, (((1,), (1,)), ((), ())), preferred_element_type=F32) * scale
            p = jnp.exp(s - jnp.max(s, axis=-1, keepdims=True))
            a = p / jnp.sum(p, axis=-1, keepdims=True)
            o = jnp.dot(a.astype(BF16), v, preferred_element_type=F32)
            xc_ref[rs, cs] = (o * _silu(cz_ref[rs, cs])).astype(xc_ref.dtype)


def _mem_attn(proj, mem_k, mem_v, nb, seq_len, nseq, tm):
    assert nseq == 1 or tm == seq_len
    nt = seq_len // tm
    rows = nseq * tm
    t = nb * seq_len
    cqb, czb = COL_CQ // MEM_WIDTH, COL_CZ // MEM_WIDTH
    return pl.pallas_call(
        functools.partial(_attn_kernel, nseq=nseq, tm=tm),
        grid=(nb // nseq, nt),
        in_specs=[pl.BlockSpec((rows, MEM_WIDTH), lambda i, j: (i * nt + j, cqb)),
                  pl.BlockSpec((rows, MEM_WIDTH), lambda i, j: (i * nt + j, czb)),
                  pl.BlockSpec((nseq, MEM_TOKENS, MEM_WIDTH), lambda i, j: (i, 0, 0)),
                  pl.BlockSpec((nseq, MEM_TOKENS, MEM_WIDTH), lambda i, j: (i, 0, 0))],
        out_specs=pl.BlockSpec((rows, MEM_WIDTH), lambda i, j: (i * nt + j, 0)),
        out_shape=jax.ShapeDtypeStruct((t, MEM_WIDTH), BF16),
        compiler_params=_cparams(("parallel", "arbitrary")),
        name="mem_attn",
    )(proj, proj, mem_k, mem_v)


def _attn_short_kernel(cq_ref, cz_ref, k_ref, v_ref, xc_ref, *, nseq, tm):
    scale = MEM_HEAD_DIM ** -0.5
    nr, nc = tm * HEADS, MEM_TOKENS * HEADS
    row_head = lax.broadcasted_iota(jnp.int32, (nr, nc), 0) // tm
    col_head = lax.broadcasted_iota(jnp.int32, (nr, nc), 1) % HEADS
    same_head = row_head == col_head
    for b in range(nseq):
        rs = slice(b * tm, (b + 1) * tm)
        kf = k_ref[b].reshape(nc, MEM_HEAD_DIM).astype(BF16)
        vf = v_ref[b].reshape(nc, MEM_HEAD_DIM).astype(BF16)
        q = jnp.concatenate([cq_ref[rs, h * MEM_HEAD_DIM:(h + 1) * MEM_HEAD_DIM] for h in range(HEADS)], axis=0)
        s = lax.dot_general(q.astype(BF16), kf, (((1,), (1,)), ((), ())), preferred_element_type=F32) * scale
        s = jnp.where(same_head, s, -jnp.inf)
        p = jnp.exp(s - jnp.max(s, axis=-1, keepdims=True))
        a = p / jnp.sum(p, axis=-1, keepdims=True)
        o = jnp.dot(a.astype(BF16), vf, preferred_element_type=F32)
        for h in range(HEADS):
            cs = slice(h * MEM_HEAD_DIM, (h + 1) * MEM_HEAD_DIM)
            xc_ref[rs, cs] = o[h * tm:(h + 1) * tm, :] * _silu(cz_ref[rs, cs])


def _mem_attn_short(proj, mem_k, mem_v, nb, seq_len, nseq):
    rows = nseq * seq_len
    t = nb * seq_len
    cqb, czb = COL_CQ // MEM_WIDTH, COL_CZ // MEM_WIDTH
    kv_spec = pl.BlockSpec((nseq, MEM_TOKENS, HEADS, MEM_HEAD_DIM), lambda i: (i, 0, 0, 0))
    return pl.pallas_call(
        functools.partial(_attn_short_kernel, nseq=nseq, tm=seq_len),
        grid=(nb // nseq,),
        in_specs=[pl.BlockSpec((rows, MEM_WIDTH), lambda i: (i, cqb)),
                  pl.BlockSpec((rows, MEM_WIDTH), lambda i: (i, czb)),
                  kv_spec, kv_spec],
        out_specs=pl.BlockSpec((rows, MEM_WIDTH), lambda i: (i, 0)),
        out_shape=jax.ShapeDtypeStruct((t, MEM_WIDTH), F32),
        compiler_params=_cparams(("parallel",)),
        name="mem_attn_short",
    )(proj, proj, mem_k, mem_v)


def _mlstm_intra(q_bf, ks_bf, v_bf, ig_row, b_row, b_col, m0_col, mask):
    dlog = jnp.where(mask, b_col - b_row + ig_row, -jnp.inf)
    inter = b_col + m0_col
    m_col = jnp.maximum(inter, jnp.max(dlog, axis=1, keepdims=True))
    dw = jnp.exp(dlog - m_col)
    inter_w = jnp.exp(inter - m_col)
    s = lax.dot_general(q_bf, ks_bf, (((1,), (1,)), ((), ())), preferred_element_type=F32) * dw
    num = jnp.dot(s.astype(BF16), v_bf, preferred_element_type=F32)
    den = jnp.sum(s, axis=1, keepdims=True)
    return num, den, m_col, inter_w


def _mlstm_prompt_kernel(q_ref, k_ref, v_ref, o_ref, z_ref, g_ref, gt_ref,
                         xb_ref, c_ref, n_ref, mcol_ref, m_scr):
    @pl.when(pl.program_id(1) == 0)
    def _():
        c_ref[...] = jnp.zeros_like(c_ref)
        n_ref[...] = jnp.zeros_like(n_ref)
        m_scr[...] = jnp.zeros_like(m_scr)

    cl = MLSTM_CHUNK
    row = lax.broadcasted_iota(jnp.int32, (cl, cl), 0)
    col = lax.broadcasted_iota(jnp.int32, (cl, cl), 1)
    mask = col <= row
    kscale = MLSTM_HEAD_DIM ** -0.5
    for h in range(HEADS):
        cs = slice(h * MLSTM_HEAD_DIM, (h + 1) * MLSTM_HEAD_DIM)
        ig_col = g_ref[:, h:h + 1]
        b_col = g_ref[:, HEADS + h:HEADS + h + 1]
        ig_row = gt_ref[h:h + 1, :]
        b_row = gt_ref[HEADS + h:HEADS + h + 1, :]
        m0 = m_scr[h:h + 1, 0:1]
        qf = q_ref[:, cs]
        kf = k_ref[:, cs] * kscale
        vf = v_ref[:, cs]
        q_bf, ks_bf = qf.astype(BF16), kf.astype(BF16)
        num, den, m_col, inter_w = _mlstm_intra(q_bf, ks_bf, vf.astype(BF16), ig_row, b_row, b_col, m0, mask)
        c0 = c_ref[0, h]
        n0 = n_ref[0, h:h + 1, :]
        num = num + inter_w * lax.dot_general(q_bf, c0.astype(BF16), (((1,), (1,)), ((), ())),
                                              preferred_element_type=F32)
        den = den + inter_w * jnp.sum(qf * n0, axis=1, keepdims=True)
        hval = num / jnp.maximum(jnp.abs(den), jnp.exp(-m_col))
        xb_ref[:, cs] = (_sigmoid(o_ref[:, cs]) * hval * _silu(z_ref[:, cs])).astype(xb_ref.dtype)
        m_new = m_col[cl - 1:cl, :]
        b_last = b_col[cl - 1:cl, :]
        w_col = jnp.exp(b_last - b_col + ig_col - m_new)
        decay = jnp.exp(b_last + m0 - m_new)
        dc = lax.dot_general((w_col * vf).astype(BF16), ks_bf, (((0,), (0,)), ((), ())),
                             preferred_element_type=F32)
        c_ref[0, h] = decay * c0 + dc
        n_ref[0, h:h + 1, :] = decay * n0 + jnp.sum(w_col * kf, axis=0, keepdims=True)
        m_scr[h:h + 1, :] = jnp.broadcast_to(m_new, (1, LANES))
        mcol_ref[:, h:h + 1] = m_col


def _mlstm_prompt(proj, g, gt, nb, seq_len):
    cl = MLSTM_CHUNK
    nc = seq_len // cl
    t = nb * seq_len
    w = MLSTM_WIDTH

    def colspec(col0):
        return pl.BlockSpec((cl, w), lambda b, c: (b * nc + c, col0 // w))

    return pl.pallas_call(
        _mlstm_prompt_kernel,
        grid=(nb, nc),
        in_specs=[colspec(COL_Q), colspec(COL_K), colspec(COL_V), colspec(COL_O), colspec(COL_Z),
                  pl.BlockSpec((cl, LANES), lambda b, c: (b * nc + c, 0)),
                  pl.BlockSpec((8, cl), lambda b, c: (0, b * nc + c))],
        out_specs=[pl.BlockSpec((cl, w), lambda b, c: (b * nc + c, 0)),
                   pl.BlockSpec((1, HEADS, MLSTM_HEAD_DIM, MLSTM_HEAD_DIM), lambda b, c: (b, 0, 0, 0)),
                   pl.BlockSpec((1, HEADS, MLSTM_HEAD_DIM), lambda b, c: (b, 0, 0)),
                   pl.BlockSpec((cl, LANES), lambda b, c: (b * nc + c, 0))],
        out_shape=[jax.ShapeDtypeStruct((t, w), BF16),
                   jax.ShapeDtypeStruct((nb, HEADS, MLSTM_HEAD_DIM, MLSTM_HEAD_DIM), F32),
                   jax.ShapeDtypeStruct((nb, HEADS, MLSTM_HEAD_DIM), F32),
                   jax.ShapeDtypeStruct((t, LANES), F32)],
        scratch_shapes=[pltpu.VMEM((8, LANES), F32)],
        compiler_params=_cparams(("arbitrary", "arbitrary")),
        name="mlstm_prompt",
    )(proj, proj, proj, proj, proj, g, gt)


def _mlstm_sample_kernel(q_ref, k_ref, v_ref, o_ref, z_ref, g_ref, gt_ref, c0_ref, n0_ref,
                         xb_ref, c_ref, n_ref, mcol_ref, num_scr, den_scr, iw_scr, *, seq_len):
    bl = pl.program_id(1)
    tl = LANES
    kscale = MLSTM_HEAD_DIM ** -0.5
    shift = seq_len.bit_length() - 1

    @pl.when(bl == 0)
    def _():
        row = lax.broadcasted_iota(jnp.int32, (tl, tl), 0)
        col = lax.broadcasted_iota(jnp.int32, (tl, tl), 1)
        mask = (col <= row) & (lax.shift_right_logical(col, shift) == lax.shift_right_logical(row, shift))
        for h in range(HEADS):
            cs = slice(h * MLSTM_HEAD_DIM, (h + 1) * MLSTM_HEAD_DIM)
            b_col = g_ref[:, HEADS + h:HEADS + h + 1]
            m0_col = g_ref[:, 2 * HEADS + h:2 * HEADS + h + 1]
            ig_row = gt_ref[h:h + 1, :]
            b_row = gt_ref[HEADS + h:HEADS + h + 1, :]
            q_bf = q_ref[:, cs].astype(BF16)
            ks_bf = (k_ref[:, cs] * kscale).astype(BF16)
            num, den, m_col, inter_w = _mlstm_intra(q_bf, ks_bf, v_ref[:, cs].astype(BF16),
                                                    ig_row, b_row, b_col, m0_col, mask)
            num_scr[:, cs] = num
            den_scr[:, h:h + 1] = den
            iw_scr[:, h:h + 1] = inter_w
            mcol_ref[:, h:h + 1] = m_col

    r0 = pl.multiple_of(bl * seq_len, 8)
    rows = pl.ds(r0, seq_len)
    last = pl.ds(r0 + seq_len - 1, 1)
    for h in range(HEADS):
        cs = slice(h * MLSTM_HEAD_DIM, (h + 1) * MLSTM_HEAD_DIM)
        qf = q_ref[rows, cs]
        kf = k_ref[rows, cs] * kscale
        vf = v_ref[rows, cs]
        ig_col = g_ref[rows, h:h + 1]
        b_col = g_ref[rows, HEADS + h:HEADS + h + 1]
        b_last = g_ref[last, HEADS + h:HEADS + h + 1]
        m0 = g_ref[last, 2 * HEADS + h:2 * HEADS + h + 1]
        m_col = mcol_ref[rows, h:h + 1]
        m_new = mcol_ref[last, h:h + 1]
        inter_w = iw_scr[rows, h:h + 1]
        c0 = c0_ref[0, h]
        n0 = n0_ref[0, h:h + 1, :]
        num = num_scr[rows, cs] + inter_w * lax.dot_general(
            qf.astype(BF16), c0.astype(BF16), (((1,), (1,)), ((), ())), preferred_element_type=F32)
        den = den_scr[rows, h:h + 1] + inter_w * jnp.sum(qf * n0, axis=1, keepdims=True)
        hval = num / jnp.maximum(jnp.abs(den), jnp.exp(-m_col))
        xb_ref[rows, cs] = (_sigmoid(o_ref[rows, cs]) * hval * _silu(z_ref[rows, cs])).astype(xb_ref.dtype)
        w_col = jnp.exp(b_last - b_col + ig_col - m_new)
        decay = jnp.exp(b_last + m0 - m_new)
        dc = lax.dot_general((w_col * vf).astype(BF16), kf.astype(BF16), (((0,), (0,)), ((), ())),
                             preferred_element_type=F32)
        c_ref[0, h] = decay * c0 + dc
        n_ref[0, h:h + 1, :] = decay * n0 + jnp.sum(w_col * kf, axis=0, keepdims=True)


def _mlstm_sample(proj, g, gt, c0, n0, nb, seq_len):
    tl = LANES
    per = tl // seq_len
    ng = nb // per
    t = nb * seq_len
    w = MLSTM_WIDTH
    d = MLSTM_HEAD_DIM

    def colspec(col0):
        return pl.BlockSpec((tl, w), lambda i, b: (i, col0 // w))

    state4 = pl.BlockSpec((1, HEADS, d, d), lambda i, b: (i * per + b, 0, 0, 0))
    state3 = pl.BlockSpec((1, HEADS, d), lambda i, b: (i * per + b, 0, 0))
    return pl.pallas_call(
        functools.partial(_mlstm_sample_kernel, seq_len=seq_len),
        grid=(ng, per),
        in_specs=[colspec(COL_Q), colspec(COL_K), colspec(COL_V), colspec(COL_O), colspec(COL_Z),
                  pl.BlockSpec((tl, LANES), lambda i, b: (i, 0)),
                  pl.BlockSpec((8, tl), lambda i, b: (0, i)),
                  state4, state3],
        out_specs=[pl.BlockSpec((tl, w), lambda i, b: (i, 0)), state4, state3,
                   pl.BlockSpec((tl, LANES), lambda i, b: (i, 0))],
        out_shape=[jax.ShapeDtypeStruct((t, w), F32),
                   jax.ShapeDtypeStruct((nb, HEADS, d, d), F32),
                   jax.ShapeDtypeStruct((nb, HEADS, d), F32),
                   jax.ShapeDtypeStruct((t, LANES), F32)],
        scratch_shapes=[pltpu.VMEM((tl, w), F32), pltpu.VMEM((tl, LANES), F32), pltpu.VMEM((tl, LANES), F32)],
        compiler_params=_cparams(("arbitrary", "arbitrary")),
        name="mlstm_sample",
    )(proj, proj, proj, proj, proj, g, gt, c0, n0)


def _final_kernel(xa_ref, xb_ref, xc_ref, ga_ref, gb_ref, gc_ref, x_ref, gpost_ref,
                  wa_ref, wb_ref, wc_ref, wo_ref, y_ref):
    ya = jnp.dot(xa_ref[...].astype(BF16), wa_ref[...], preferred_element_type=F32)
    merged = _sigmoid(ga_ref[...]) * ya
    yb = jnp.dot(xb_ref[...].astype(BF16), wb_ref[...], preferred_element_type=F32)
    merged = merged + _sigmoid(gb_ref[...]) * yb
    yc = jnp.dot(xc_ref[...].astype(BF16), wc_ref[...], preferred_element_type=F32)
    merged = merged + _sigmoid(gc_ref[...]) * yc
    out = jnp.dot(merged.astype(BF16), wo_ref[...], preferred_element_type=F32)
    ms = jnp.mean(out * out, axis=-1, keepdims=True)
    y_ref[...] = x_ref[...] + out * lax.rsqrt(ms + EPS) * gpost_ref[...]


def _final(xa, xb, xc, proj, x2d, gpost_row, wa, wb, wc, wo, tm):
    t, d = x2d.shape

    def rowspec(width, colblk=0):
        return pl.BlockSpec((tm, width), lambda i: (i, colblk))

    def resident(shape):
        return pl.BlockSpec(shape, lambda i: (0, 0), pipeline_mode=pl.Buffered(1))

    return pl.pallas_call(
        _final_kernel,
        grid=(t // tm,),
        in_specs=[rowspec(POOL_WIDTH), rowspec(MLSTM_WIDTH), rowspec(MEM_WIDTH),
                  rowspec(d, COL_GA // d), rowspec(d, COL_GB // d), rowspec(d, COL_GC // d),
                  rowspec(d), resident((1, d)),
                  resident(wa.shape), resident(wb.shape), resident(wc.shape), resident(wo.shape)],
        out_specs=rowspec(d),
        out_shape=jax.ShapeDtypeStruct((t, d), F32),
        compiler_params=_cparams(("parallel",)),
        name="merge_out",
    )(xa, xb, xc, proj, proj, proj, x2d, gpost_row, wa, wb, wc, wo)


def _mixer_group(x, pool_buf, c0, n0, m0, mem_k, mem_v, wts, start):
    nb, seq_len, d = x.shape
    t = nb * seq_len
    x2d = x.reshape(t, d)
    h = _rmsnorm_bf16(x2d, wts["g_pre"], tm=512)
    proj_a = _inproj(h, wts["w_in"], N_PROJ_A, tm=min(t, 1024), tn=1024)
    proj_b = _inproj(h, wts["w_in_b"], N_PROJ_B, tm=min(t, 1024), tn=1024)

    fresh = c0 is None
    if fresh:
        m0_rows = jnp.zeros((t, LANES), F32)
    else:
        m0_rows = jnp.pad(jnp.repeat(m0, seq_len, axis=0), ((0, 0), (2 * HEADS, LANES - 3 * HEADS)))
    g, gt = _gate_prep(h, wts["w_gate"], wts["gate_bias"], m0_rows, min(seq_len, MLSTM_CHUNK))

    if seq_len >= 512:
        xa, new_buf = _pool_branch(proj_a, pool_buf, wts["w_pool_grp"], wts["pool_scale"], nb, seq_len,
                                   nseq=1, tm=512, start=start)
        xc = _mem_attn(proj_b, mem_k, mem_v, nb, seq_len, nseq=1, tm=512)
    else:
        xa, new_buf = _pool_branch(proj_a, pool_buf, wts["w_pool_grp"], wts["pool_scale"], nb, seq_len,
                                   nseq=32, tm=seq_len, start=start)
        xc = _mem_attn_short(proj_b, mem_k, mem_v, nb, seq_len, nseq=8)

    if fresh:
        xb, c_new, n_new, mcol = _mlstm_prompt(proj_a, g, gt, nb, seq_len)
    else:
        xb, c_new, n_new, mcol = _mlstm_sample(proj_a, g, gt, c0, n0, nb, seq_len)
    m_new = mcol.reshape(nb, seq_len, LANES)[:, seq_len - 1, :HEADS]

    y = _final(xa, xb, xc, proj_b, x2d, wts["g_post"], wts["w_br_pool"], wts["w_br_mlstm"],
               wts["w_br_mem"], wts["w_out"], tm=256)
    return y.reshape(nb, seq_len, d), new_buf, c_new, n_new, m_new


def kernel(x_prompt, x_sample, state_pool, state_mlstm_C, state_mlstm_n, state_mlstm_m, cache_mem_k,
           cache_mem_v, mem_prompt, g_pre, g_post, w_in, b_mlstm_i, b_mlstm_f, w_pool_grp, pool_scale,
           g_mem, w_mem_kv, w_br_pool, w_br_mlstm, w_br_mem, w_out):
    nbp = x_prompt.shape[0]
    l = 0
    w_in_l = w_in[l]
    gate_end = GATE_COL0 + N_GATE_COLS
    wts = {
        "g_pre": g_pre[l][None, :],
        "g_post": g_post[l][None, :],
        "w_in": w_in,
        "w_in_b": w_in_l[:, gate_end:].astype(BF16),
        "w_gate": jnp.pad(w_in_l[:, GATE_COL0:gate_end], ((0, 0), (0, LANES - N_GATE_COLS))).astype(BF16),
        "gate_bias": jnp.pad(jnp.concatenate([b_mlstm_i[l], b_mlstm_f[l]]), (0, LANES - N_GATE_COLS))[None, :],
        "w_pool_grp": w_pool_grp[l].astype(BF16),
        "pool_scale": pool_scale[l][None, :],
        "w_br_pool": w_br_pool[l].astype(BF16),
        "w_br_mlstm": w_br_mlstm[l].astype(BF16),
        "w_br_mem": w_br_mem[l].astype(BF16),
        "w_out": w_out[l].astype(BF16),
    }
    mk2d, mv2d = _mem_kv(mem_prompt.reshape(nbp * MEM_TOKENS, D_MODEL), g_mem[l][None, :],
                         w_mem_kv[l].astype(BF16), tm=256)
    mk3, mv3 = mk2d.reshape(nbp, MEM_TOKENS, MEM_WIDTH), mv2d.reshape(nbp, MEM_TOKENS, MEM_WIDTH)
    yp, pool_p, c_p, n_p, m_p = _mixer_group(
        x_prompt, jnp.zeros((nbp, POOL_BUF, POOL_WIDTH), F32), None, None, None, mk3, mv3, wts, 0)
    nbs = x_sample.shape[0]
    ys, pool_s, c_s, n_s, m_s = _mixer_group(
        x_sample, state_pool[l], state_mlstm_C[l], state_mlstm_n[l], state_mlstm_m[l],
        cache_mem_k[l], cache_mem_v[l], wts, PAST_LEN)
    kv_shape = (1, nbp, MEM_TOKENS, HEADS, MEM_HEAD_DIM)
    return (yp, ys, pool_p[None], c_p[None], n_p[None], m_p[None],
            mk2d.reshape(kv_shape), mv2d.reshape(kv_shape),
            pool_s[None], c_s[None], n_s[None], m_s[None])
```

```python
import functools

import jax
import jax.numpy as jnp
from jax import lax
from jax.experimental import pallas as pl
from jax.experimental.pallas import tpu as pltpu

F32 = jnp.float32
BF16 = jnp.bfloat16

D_MODEL = 2048
PAST_LEN = 16384
POOL_WINDOWS = (2, 4, 8, 16)
POOL_GROUP_WIDTH = 256
POOL_WIDTH = 1024
POOL_BUF = 15
HEADS = 4
MLSTM_HEAD_DIM = 512
MLSTM_WIDTH = 2048
PROMPT_CHUNK = 256
MEM_TOKENS = 256
MEM_HEAD_DIM = 256
MEM_WIDTH = 1024
EPS = 1e-6
N_GATE_COLS = 2 * HEADS
GATE_COL0 = 2 * POOL_WIDTH + 5 * MLSTM_WIDTH
LANES = 128
COL_PV, COL_PZ = 0, 1024
COL_Q, COL_K, COL_V, COL_O, COL_Z = 2048, 4096, 6144, 8192, 10240
COL_CQ, COL_CZ = 0, 1024
COL_GA, COL_GB, COL_GC = 2048, 4096, 6144
N_PROJ_A = GATE_COL0
N_PROJ_B = 2 * MEM_WIDTH + 3 * D_MODEL
VMEM_LIMIT = 56 * 1024 * 1024


def _cparams(sem):
    return pltpu.CompilerParams(dimension_semantics=sem, vmem_limit_bytes=VMEM_LIMIT)


def _sigmoid(x):
    return 1.0 / (1.0 + jnp.exp(-x))


def _silu(x):
    return x * _sigmoid(x)


def _norm_kernel(x_ref, g_ref, o_ref):
    x = x_ref[...]
    ms = jnp.mean(x * x, axis=-1, keepdims=True)
    o_ref[...] = (x * lax.rsqrt(ms + EPS) * g_ref[...]).astype(o_ref.dtype)


def _rmsnorm_bf16(x2d, g_row, tm):
    t, d = x2d.shape
    return pl.pallas_call(
        _norm_kernel,
        grid=(t // tm,),
        in_specs=[pl.BlockSpec((tm, d), lambda i: (i, 0)),
                  pl.BlockSpec((1, d), lambda i: (0, 0))],
        out_specs=pl.BlockSpec((tm, d), lambda i: (i, 0)),
        out_shape=jax.ShapeDtypeStruct((t, d), BF16),
        compiler_params=_cparams(("parallel",)),
        name="rmsnorm",
    )(x2d, g_row)


def _inproj_kernel(h_ref, *refs, shift):
    if shift:
        w_ref, wnext_ref, o_ref, wbf_ref = refs
    else:
        w_ref, o_ref, wbf_ref = refs

    @pl.when(pl.program_id(1) == 0)
    def _():
        if shift:
            w = jnp.concatenate([w_ref[:, shift:], wnext_ref[:, :shift]], axis=1)
        else:
            w = w_ref[...]
        wbf_ref[...] = w.astype(BF16)

    o_ref[...] = jnp.dot(h_ref[...], wbf_ref[...], preferred_element_type=F32)


def _inproj(h, w3, col0, n_cols, tm, tn):
    t, d = h.shape
    shift = col0 % LANES
    base = (col0 - shift) // tn
    assert base * tn + shift == col0 and n_cols % tn == 0 and t % tm == 0
    in_specs = [pl.BlockSpec((tm, d), lambda j, i: (i, 0)),
                pl.BlockSpec((None, d, tn), lambda j, i: (0, 0, base + j))]
    operands = [h, w3]
    if shift:
        assert col0 + n_cols <= w3.shape[2]
        in_specs.append(pl.BlockSpec((None, d, LANES), lambda j, i: (0, 0, (base + j + 1) * (tn // LANES))))
        operands.append(w3)
    return pl.pallas_call(
        functools.partial(_inproj_kernel, shift=shift),
        grid=(n_cols // tn, t // tm),
        in_specs=in_specs,
        out_specs=pl.BlockSpec((tm, tn), lambda j, i: (i, j)),
        out_shape=jax.ShapeDtypeStruct((t, n_cols), F32),
        scratch_shapes=[pltpu.VMEM((d, tn), BF16)],
        compiler_params=_cparams(("arbitrary", "arbitrary")),
        name="inproj",
    )(*operands)


GATE_TILE = 512


def _gate_kernel(h_ref, wg_ref, bias_ref, m0_ref, g_ref, gt_ref, *, seg, ct):
    row = lax.broadcasted_iota(jnp.int32, (ct, ct), 0)
    col = lax.broadcasted_iota(jnp.int32, (ct, ct), 1)
    same = col <= row
    if seg < ct:
        shift = seg.bit_length() - 1
        same = same & (lax.shift_right_logical(col, shift) == lax.shift_right_logical(row, shift))
    same = same.astype(F32)
    lane = lax.broadcasted_iota(jnp.int32, (ct, LANES), 1)
    wg = wg_ref[...].astype(BF16)
    for r in range(GATE_TILE // ct):
        rs = slice(r * ct, (r + 1) * ct)
        x = jnp.dot(h_ref[rs, :], wg, preferred_element_type=F32) + bias_ref[...]
        lf = jnp.minimum(x, 0.0) - jnp.log1p(jnp.exp(-jnp.abs(x)))
        cums = jnp.dot(same, lf, precision=lax.Precision.HIGHEST, preferred_element_type=F32)
        g = jnp.where(lane < HEADS, x, jnp.where(lane < 2 * HEADS, cums, m0_ref[rs, :]))
        g_ref[rs, :] = g
        gt_ref[:, rs] = g.T[:8, :]


def _gate_prep(h, w3, bias_row, m0_rows, seg):
    t, d = h.shape
    tg = GATE_TILE
    ct = max(seg, LANES)
    assert GATE_COL0 % LANES == 0 and tg % ct == 0 and ct % seg == 0 and seg & (seg - 1) == 0
    return pl.pallas_call(
        functools.partial(_gate_kernel, seg=seg, ct=ct),
        grid=(t // tg,),
        in_specs=[pl.BlockSpec((tg, d), lambda i: (i, 0)),
                  pl.BlockSpec((None, d, LANES), lambda i: (0, 0, GATE_COL0 // LANES)),
                  pl.BlockSpec((1, LANES), lambda i: (0, 0)),
                  pl.BlockSpec((tg, LANES), lambda i: (i, 0))],
        out_specs=[pl.BlockSpec((tg, LANES), lambda i: (i, 0)),
                   pl.BlockSpec((8, tg), lambda i: (0, i))],
        out_shape=[jax.ShapeDtypeStruct((t, LANES), F32),
                   jax.ShapeDtypeStruct((8, t), F32)],
        compiler_params=_cparams(("parallel",)),
        name="gate_prep",
    )(h, w3, bias_row, m0_rows)


def _pool_kernel(pv_ref, pz_ref, buf_ref, wg_ref, scale_ref, xa_ref, nbuf_ref, s_ref, p_ref,
                 *, nseq, tm, nt, start):
    j = pl.program_id(1)

    def one_seq(b, r0):
        if nt > 1:
            @pl.when(j == 0)
            def _():
                s_ref[1:16, :] = buf_ref[b]
        else:
            s_ref[1:16, :] = buf_ref[b]
        s_ref[16:16 + tm, :] = pv_ref[pl.ds(r0, tm), :]
        pos = start + j * tm + lax.broadcasted_iota(jnp.int32, (tm, 1), 0)
        for g, w in enumerate(POOL_WINDOWS):
            cs = slice(g * POOL_GROUP_WIDTH, (g + 1) * POOL_GROUP_WIDTH)
            acc = s_ref[16:16 + tm, cs]
            for i in range(1, w):
                acc = acc + s_ref[16 - i:16 - i + tm, cs]
            cnt = jnp.minimum(w, pos + 1).astype(F32)
            p_ref[pl.ds(r0, tm), cs] = acc / cnt - s_ref[16:16 + tm, cs]
        tail = s_ref[tm + 1:tm + 16, :]
        nbuf_ref[b] = tail
        if nt > 1:
            s_ref[1:16, :] = tail

    if nseq == 1:
        one_seq(0, 0)
    else:
        def body(b, carry):
            one_seq(b, pl.multiple_of(b * tm, 8))
            return carry
        lax.fori_loop(0, nseq, body, 0)

    for g in range(len(POOL_WINDOWS)):
        cs = slice(g * POOL_GROUP_WIDTH, (g + 1) * POOL_GROUP_WIDTH)
        pa = jnp.dot(p_ref[:, cs].astype(BF16), wg_ref[g], preferred_element_type=F32) * scale_ref[:, cs]
        xa_ref[:, cs] = (pa * _silu(pz_ref[:, cs])).astype(xa_ref.dtype)


def _pool_branch(proj, buf, w_grp, scale_row, nb, seq_len, nseq, tm, start):
    assert nseq == 1 or tm == seq_len
    nt = seq_len // tm
    rows = nseq * tm
    t = nb * seq_len
    pvb, pzb = COL_PV // POOL_WIDTH, COL_PZ // POOL_WIDTH
    return pl.pallas_call(
        functools.partial(_pool_kernel, nseq=nseq, tm=tm, nt=nt, start=start),
        grid=(nb // nseq, nt),
        in_specs=[pl.BlockSpec((rows, POOL_WIDTH), lambda i, j: (i * nt + j, pvb)),
                  pl.BlockSpec((rows, POOL_WIDTH), lambda i, j: (i * nt + j, pzb)),
                  pl.BlockSpec((nseq, POOL_BUF, POOL_WIDTH), lambda i, j: (i, 0, 0)),
                  pl.BlockSpec((4, POOL_GROUP_WIDTH, POOL_GROUP_WIDTH), lambda i, j: (0, 0, 0)),
                  pl.BlockSpec((1, POOL_WIDTH), lambda i, j: (0, 0))],
        out_specs=[pl.BlockSpec((rows, POOL_WIDTH), lambda i, j: (i * nt + j, 0)),
                   pl.BlockSpec((nseq, POOL_BUF, POOL_WIDTH), lambda i, j: (i, 0, 0))],
        out_shape=[jax.ShapeDtypeStruct((t, POOL_WIDTH), BF16),
                   jax.ShapeDtypeStruct((nb, POOL_BUF, POOL_WIDTH), F32)],
        scratch_shapes=[pltpu.VMEM((16 + tm, POOL_WIDTH), F32),
                        pltpu.VMEM((rows, POOL_WIDTH), F32)],
        compiler_params=_cparams(("arbitrary", "arbitrary")),
        name="pool_branch",
    )(proj, proj, buf, w_grp, scale_row)


def _memkv_kernel(m_ref, g_ref, w_ref, k_ref, v_ref):
    x = m_ref[...]
    ms = jnp.mean(x * x, axis=-1, keepdims=True)
    h = (x * lax.rsqrt(ms + EPS) * g_ref[...]).astype(BF16)
    kv = jnp.dot(h, w_ref[...], preferred_element_type=F32)
    k_ref[...] = kv[:, :MEM_WIDTH]
    v_ref[...] = kv[:, MEM_WIDTH:]


def _mem_kv(mem2d, g_row, w_bf, tm):
    t, d = mem2d.shape
    return pl.pallas_call(
        _memkv_kernel,
        grid=(t // tm,),
        in_specs=[pl.BlockSpec((tm, d), lambda i: (i, 0)),
                  pl.BlockSpec((1, d), lambda i: (0, 0)),
                  pl.BlockSpec((d, 2 * MEM_WIDTH), lambda i: (0, 0))],
        out_specs=[pl.BlockSpec((tm, MEM_WIDTH), lambda i: (i, 0)),
                   pl.BlockSpec((tm, MEM_WIDTH), lambda i: (i, 0))],
        out_shape=[jax.ShapeDtypeStruct((t, MEM_WIDTH), F32),
                   jax.ShapeDtypeStruct((t, MEM_WIDTH), F32)],
        compiler_params=_cparams(("parallel",)),
        name="mem_kv",
    )(mem2d, g_row, w_bf)


def _attn_kernel(cq_ref, cz_ref, k_ref, v_ref, xc_ref, *, nseq, tm):
    scale = MEM_HEAD_DIM ** -0.5
    for b in range(nseq):
        rs = slice(b * tm, (b + 1) * tm)
        for h in range(HEADS):
            cs = slice(h * MEM_HEAD_DIM, (h + 1) * MEM_HEAD_DIM)
            q = cq_ref[rs, cs].astype(BF16)
            k = k_ref[b, :, cs].astype(BF16)
            v = v_ref[b, :, cs].astype(BF16)
            s = lax.dot_general(q, k, (((1,), (1,)), ((), ())), preferred_element_type=F32) * scale
            p = jnp.exp(s - jnp.max(s, axis=-1, keepdims=True))
            a = p / jnp.sum(p, axis=-1, keepdims=True)
            o = jnp.dot(a.astype(BF16), v, preferred_element_type=F32)
            xc_ref[rs, cs] = (o * _silu(cz_ref[rs, cs])).astype(xc_ref.dtype)


def _mem_attn(proj, mem_k, mem_v, nb, seq_len, nseq, tm):
    assert nseq == 1 or tm == seq_len
    nt = seq_len // tm
    rows = nseq * tm
    t = nb * seq_len
    cqb, czb = COL_CQ // MEM_WIDTH, COL_CZ // MEM_WIDTH
    return pl.pallas_call(
        functools.partial(_attn_kernel, nseq=nseq, tm=tm),
        grid=(nb // nseq, nt),
        in_specs=[pl.BlockSpec((rows, MEM_WIDTH), lambda i, j: (i * nt + j, cqb)),
                  pl.BlockSpec((rows, MEM_WIDTH), lambda i, j: (i * nt + j, czb)),
                  pl.BlockSpec((nseq, MEM_TOKENS, MEM_WIDTH), lambda i, j: (i, 0, 0)),
                  pl.BlockSpec((nseq, MEM_TOKENS, MEM_WIDTH), lambda i, j: (i, 0, 0))],
        out_specs=pl.BlockSpec((rows, MEM_WIDTH), lambda i, j: (i * nt + j, 0)),
        out_shape=jax.ShapeDtypeStruct((t, MEM_WIDTH), BF16),
        compiler_params=_cparams(("parallel", "arbitrary")),
        name="mem_attn",
    )(proj, proj, mem_k, mem_v)


def _attn_short_kernel(cq_ref, cz_ref, k_ref, v_ref, xc_ref, *, nseq, tm):
    scale = MEM_HEAD_DIM ** -0.5
    nr, nc = tm * HEADS, MEM_TOKENS * HEADS
    row_head = lax.broadcasted_iota(jnp.int32, (nr, nc), 0) // tm
    col_head = lax.broadcasted_iota(jnp.int32, (nr, nc), 1) % HEADS
    same_head = row_head == col_head
    for b in range(nseq):
        rs = slice(b * tm, (b + 1) * tm)
        kf = k_ref[b].reshape(nc, MEM_HEAD_DIM).astype(BF16)
        vf = v_ref[b].reshape(nc, MEM_HEAD_DIM).astype(BF16)
        q = jnp.concatenate([cq_ref[rs, h * MEM_HEAD_DIM:(h + 1) * MEM_HEAD_DIM] for h in range(HEADS)], axis=0)
        s = lax.dot_general(q.astype(BF16), kf, (((1,), (1,)), ((), ())), preferred_element_type=F32) * scale
        s = jnp.where(same_head, s, -jnp.inf)
        p = jnp.exp(s - jnp.max(s, axis=-1, keepdims=True))
        a = p / jnp.sum(p, axis=-1, keepdims=True)
        o = jnp.dot(a.astype(BF16), vf, preferred_element_type=F32)
        for h in range(HEADS):
            cs = slice(h * MEM_HEAD_DIM, (h + 1) * MEM_HEAD_DIM)
            xc_ref[rs, cs] = o[h * tm:(h + 1) * tm, :] * _silu(cz_ref[rs, cs])


def _mem_attn_short(proj, mem_k, mem_v, nb, seq_len, nseq):
    rows = nseq * seq_len
    t = nb * seq_len
    cqb, czb = COL_CQ // MEM_WIDTH, COL_CZ // MEM_WIDTH
    kv_spec = pl.BlockSpec((nseq, MEM_TOKENS, HEADS, MEM_HEAD_DIM), lambda i: (i, 0, 0, 0))
    return pl.pallas_call(
        functools.partial(_attn_short_kernel, nseq=nseq, tm=seq_len),
        grid=(nb // nseq,),
        in_specs=[pl.BlockSpec((rows, MEM_WIDTH), lambda i: (i, cqb)),
                  pl.BlockSpec((rows, MEM_WIDTH), lambda i: (i, czb)),
                  kv_spec, kv_spec],
        out_specs=pl.BlockSpec((rows, MEM_WIDTH), lambda i: (i, 0)),
        out_shape=jax.ShapeDtypeStruct((t, MEM_WIDTH), F32),
        compiler_params=_cparams(("parallel",)),
        name="mem_attn_short",
    )(proj, proj, mem_k, mem_v)


def _mlstm_intra(q_bf, ks_bf, v_bf, ig_row, b_row, b_col, m0_col, mask):
    dlog = jnp.where(mask, b_col - b_row + ig_row, -jnp.inf)
    inter = b_col + m0_col
    m_col = jnp.maximum(inter, jnp.max(dlog, axis=1, keepdims=True))
    dw = jnp.exp(dlog - m_col)
    inter_w = jnp.exp(inter - m_col)
    s = lax.dot_general(q_bf, ks_bf, (((1,), (1,)), ((), ())), preferred_element_type=F32) * dw
    num = jnp.dot(s.astype(BF16), v_bf, preferred_element_type=F32)
    den = jnp.sum(s, axis=1, keepdims=True)
    return num, den, m_col, inter_w


def _mlstm_prompt_kernel(q_ref, k_ref, v_ref, o_ref, z_ref, g_ref, gt_ref,
                         xb_ref, c_ref, n_ref, mcol_ref, m_scr):
    @pl.when(pl.program_id(1) == 0)
    def _():
        c_ref[...] = jnp.zeros_like(c_ref)
        n_ref[...] = jnp.zeros_like(n_ref)
        m_scr[...] = jnp.zeros_like(m_scr)

    cl = PROMPT_CHUNK
    row = lax.broadcasted_iota(jnp.int32, (cl, cl), 0)
    col = lax.broadcasted_iota(jnp.int32, (cl, cl), 1)
    mask = col <= row
    kscale = MLSTM_HEAD_DIM ** -0.5
    for h in range(HEADS):
        cs = slice(h * MLSTM_HEAD_DIM, (h + 1) * MLSTM_HEAD_DIM)
        ig_col = g_ref[:, h:h + 1]
        b_col = g_ref[:, HEADS + h:HEADS + h + 1]
        ig_row = gt_ref[h:h + 1, :]
        b_row = gt_ref[HEADS + h:HEADS + h + 1, :]
        m0 = m_scr[h:h + 1, 0:1]
        qf = q_ref[:, cs]
        kf = k_ref[:, cs] * kscale
        vf = v_ref[:, cs]
        q_bf, ks_bf = qf.astype(BF16), kf.astype(BF16)
        num, den, m_col, inter_w = _mlstm_intra(q_bf, ks_bf, vf.astype(BF16), ig_row, b_row, b_col, m0, mask)
        c0 = c_ref[0, h]
        n0 = n_ref[0, h:h + 1, :]
        num = num + inter_w * lax.dot_general(q_bf, c0.astype(BF16), (((1,), (1,)), ((), ())),
                                              preferred_element_type=F32)
        den = den + inter_w * jnp.sum(qf * n0, axis=1, keepdims=True)
        hval = num / jnp.maximum(jnp.abs(den), jnp.exp(-m_col))
        xb_ref[:, cs] = (_sigmoid(o_ref[:, cs]) * hval * _silu(z_ref[:, cs])).astype(xb_ref.dtype)
        m_new = m_col[cl - 1:cl, :]
        b_last = b_col[cl - 1:cl, :]
        w_col = jnp.exp(b_last - b_col + ig_col - m_new)
        decay = jnp.exp(b_last + m0 - m_new)
        dc = lax.dot_general((w_col * vf).astype(BF16), ks_bf, (((0,), (0,)), ((), ())),
                             preferred_element_type=F32)
        c_ref[0, h] = decay * c0 + dc
        n_ref[0, h:h + 1, :] = decay * n0 + jnp.sum(w_col * kf, axis=0, keepdims=True)
        m_scr[h:h + 1, :] = jnp.broadcast_to(m_new, (1, LANES))
        mcol_ref[:, h:h + 1] = m_col


def _mlstm_prompt(proj, g, gt, nb, seq_len):
    cl = PROMPT_CHUNK
    nc = seq_len // cl
    t = nb * seq_len
    w = MLSTM_WIDTH

    def colspec(col0):
        return pl.BlockSpec((cl, w), lambda b, c: (b * nc + c, col0 // w))

    return pl.pallas_call(
        _mlstm_prompt_kernel,
        grid=(nb, nc),
        in_specs=[colspec(COL_Q), colspec(COL_K), colspec(COL_V), colspec(COL_O), colspec(COL_Z),
                  pl.BlockSpec((cl, LANES), lambda b, c: (b * nc + c, 0)),
                  pl.BlockSpec((8, cl), lambda b, c: (0, b * nc + c))],
        out_specs=[pl.BlockSpec((cl, w), lambda b, c: (b * nc + c, 0)),
                   pl.BlockSpec((1, HEADS, MLSTM_HEAD_DIM, MLSTM_HEAD_DIM), lambda b, c: (b, 0, 0, 0)),
                   pl.BlockSpec((1, HEADS, MLSTM_HEAD_DIM), lambda b, c: (b, 0, 0)),
                   pl.BlockSpec((cl, LANES), lambda b, c: (b * nc + c, 0))],
        out_shape=[jax.ShapeDtypeStruct((t, w), BF16),
                   jax.ShapeDtypeStruct((nb, HEADS, MLSTM_HEAD_DIM, MLSTM_HEAD_DIM), F32),
                   jax.ShapeDtypeStruct((nb, HEADS, MLSTM_HEAD_DIM), F32),
                   jax.ShapeDtypeStruct((t, LANES), F32)],
        scratch_shapes=[pltpu.VMEM((8, LANES), F32)],
        compiler_params=_cparams(("arbitrary", "arbitrary")),
        name="mlstm_prompt",
    )(proj, proj, proj, proj, proj, g, gt)


def _mlstm_sample_kernel(q_ref, k_ref, v_ref, o_ref, z_ref, g_ref, gt_ref, c0_ref, n0_ref,
                         xb_ref, c_ref, n_ref, mcol_ref, num_scr, den_scr, iw_scr, *, seq_len):
    bl = pl.program_id(1)
    tl = LANES
    kscale = MLSTM_HEAD_DIM ** -0.5
    shift = seq_len.bit_length() - 1

    @pl.when(bl == 0)
    def _():
        row = lax.broadcasted_iota(jnp.int32, (tl, tl), 0)
        col = lax.broadcasted_iota(jnp.int32, (tl, tl), 1)
        mask = (col <= row) & (lax.shift_right_logical(col, shift) == lax.shift_right_logical(row, shift))
        for h in range(HEADS):
            cs = slice(h * MLSTM_HEAD_DIM, (h + 1) * MLSTM_HEAD_DIM)
            b_col = g_ref[:, HEADS + h:HEADS + h + 1]
            m0_col = g_ref[:, 2 * HEADS + h:2 * HEADS + h + 1]
            ig_row = gt_ref[h:h + 1, :]
            b_row = gt_ref[HEADS + h:HEADS + h + 1, :]
            q_bf = q_ref[:, cs].astype(BF16)
            ks_bf = (k_ref[:, cs] * kscale).astype(BF16)
            num, den, m_col, inter_w = _mlstm_intra(q_bf, ks_bf, v_ref[:, cs].astype(BF16),
                                                    ig_row, b_row, b_col, m0_col, mask)
            num_scr[:, cs] = num
            den_scr[:, h:h + 1] = den
            iw_scr[:, h:h + 1] = inter_w
            mcol_ref[:, h:h + 1] = m_col

    r0 = pl.multiple_of(bl * seq_len, 8)
    rows = pl.ds(r0, seq_len)
    last = pl.ds(r0 + seq_len - 1, 1)
    for h in range(HEADS):
        cs = slice(h * MLSTM_HEAD_DIM, (h + 1) * MLSTM_HEAD_DIM)
        qf = q_ref[rows, cs]
        kf = k_ref[rows, cs] * kscale
        vf = v_ref[rows, cs]
        ig_col = g_ref[rows, h:h + 1]
        b_col = g_ref[rows, HEADS + h:HEADS + h + 1]
        b_last = g_ref[last, HEADS + h:HEADS + h + 1]
        m0 = g_ref[last, 2 * HEADS + h:2 * HEADS + h + 1]
        m_col = mcol_ref[rows, h:h + 1]
        m_new = mcol_ref[last, h:h + 1]
        inter_w = iw_scr[rows, h:h + 1]
        c0 = c0_ref[0, h]
        n0 = n0_ref[0, h:h + 1, :]
        num = num_scr[rows, cs] + inter_w * lax.dot_general(
            qf.astype(BF16), c0.astype(BF16), (((1,), (1,)), ((), ())), preferred_element_type=F32)
        den = den_scr[rows, h:h + 1] + inter_w * jnp.sum(qf * n0, axis=1, keepdims=True)
        hval = num / jnp.maximum(jnp.abs(den), jnp.exp(-m_col))
        xb_ref[rows, cs] = (_sigmoid(o_ref[rows, cs]) * hval * _silu(z_ref[rows, cs])).astype(xb_ref.dtype)
        w_col = jnp.exp(b_last - b_col + ig_col - m_new)
        decay = jnp.exp(b_last + m0 - m_new)
        dc = lax.dot_general((w_col * vf).astype(BF16), kf.astype(BF16), (((0,), (0,)), ((), ())),
                             preferred_element_type=F32)
        c_ref[0, h] = decay * c0 + dc
        n_ref[0, h:h + 1, :] = decay * n0 + jnp.sum(w_col * kf, axis=0, keepdims=True)


def _mlstm_sample(proj, g, gt, c0, n0, nb, seq_len):
    tl = LANES
    per = tl // seq_len
    ng = nb // per
    t = nb * seq_len
    w = MLSTM_WIDTH
    d = MLSTM_HEAD_DIM

    def colspec(col0):
        return pl.BlockSpec((tl, w), lambda i, b: (i, col0 // w))

    state4 = pl.BlockSpec((1, HEADS, d, d), lambda i, b: (i * per + b, 0, 0, 0))
    state3 = pl.BlockSpec((1, HEADS, d), lambda i, b: (i * per + b, 0, 0))
    return pl.pallas_call(
        functools.partial(_mlstm_sample_kernel, seq_len=seq_len),
        grid=(ng, per),
        in_specs=[colspec(COL_Q), colspec(COL_K), colspec(COL_V), colspec(COL_O), colspec(COL_Z),
                  pl.BlockSpec((tl, LANES), lambda i, b: (i, 0)),
                  pl.BlockSpec((8, tl), lambda i, b: (0, i)),
                  state4, state3],
        out_specs=[pl.BlockSpec((tl, w), lambda i, b: (i, 0)), state4, state3,
                   pl.BlockSpec((tl, LANES), lambda i, b: (i, 0))],
        out_shape=[jax.ShapeDtypeStruct((t, w), F32),
                   jax.ShapeDtypeStruct((nb, HEADS, d, d), F32),
                   jax.ShapeDtypeStruct((nb, HEADS, d), F32),
                   jax.ShapeDtypeStruct((t, LANES), F32)],
        scratch_shapes=[pltpu.VMEM((tl, w), F32), pltpu.VMEM((tl, LANES), F32), pltpu.VMEM((tl, LANES), F32)],
        compiler_params=_cparams(("arbitrary", "arbitrary")),
        name="mlstm_sample",
    )(proj, proj, proj, proj, proj, g, gt, c0, n0)


def _final_kernel(xa_ref, xb_ref, xc_ref, ga_ref, gb_ref, gc_ref, x_ref, gpost_ref,
                  wa_ref, wb_ref, wc_ref, wo_ref, y_ref):
    ya = jnp.dot(xa_ref[...].astype(BF16), wa_ref[...], preferred_element_type=F32)
    merged = _sigmoid(ga_ref[...]) * ya
    yb = jnp.dot(xb_ref[...].astype(BF16), wb_ref[...], preferred_element_type=F32)
    merged = merged + _sigmoid(gb_ref[...]) * yb
    yc = jnp.dot(xc_ref[...].astype(BF16), wc_ref[...], preferred_element_type=F32)
    merged = merged + _sigmoid(gc_ref[...]) * yc
    out = jnp.dot(merged.astype(BF16), wo_ref[...], preferred_element_type=F32)
    ms = jnp.mean(out * out, axis=-1, keepdims=True)
    y_ref[...] = x_ref[...] + out * lax.rsqrt(ms + EPS) * gpost_ref[...]


def _final(xa, xb, xc, proj, x2d, gpost_row, wa, wb, wc, wo, tm):
    t, d = x2d.shape

    def rowspec(width, colblk=0):
        return pl.BlockSpec((tm, width), lambda i: (i, colblk))

    def resident(shape):
        return pl.BlockSpec(shape, lambda i: (0, 0), pipeline_mode=pl.Buffered(1))

    return pl.pallas_call(
        _final_kernel,
        grid=(t // tm,),
        in_specs=[rowspec(POOL_WIDTH), rowspec(MLSTM_WIDTH), rowspec(MEM_WIDTH),
                  rowspec(d, COL_GA // d), rowspec(d, COL_GB // d), rowspec(d, COL_GC // d),
                  rowspec(d), resident((1, d)),
                  resident(wa.shape), resident(wb.shape), resident(wc.shape), resident(wo.shape)],
        out_specs=rowspec(d),
        out_shape=jax.ShapeDtypeStruct((t, d), F32),
        compiler_params=_cparams(("parallel",)),
        name="merge_out",
    )(xa, xb, xc, proj, proj, proj, x2d, gpost_row, wa, wb, wc, wo)


def _mixer_group(x, pool_buf, c0, n0, m0, mem_k, mem_v, wts, start):
    nb, seq_len, d = x.shape
    t = nb * seq_len
    x2d = x.reshape(t, d)
    h = _rmsnorm_bf16(x2d, wts["g_pre"], tm=512)
    proj_a = _inproj(h, wts["w_in"], 0, N_PROJ_A, tm=min(t, 1024), tn=1024)
    proj_b = _inproj(h, wts["w_in"], GATE_COL0 + N_GATE_COLS, N_PROJ_B, tm=min(t, 1024), tn=1024)

    fresh = c0 is None
    if fresh:
        m0_rows = jnp.zeros((t, LANES), F32)
    else:
        m0_rows = jnp.pad(jnp.repeat(m0, seq_len, axis=0), ((0, 0), (2 * HEADS, LANES - 3 * HEADS)))
    g, gt = _gate_prep(h, wts["w_in"], wts["gate_bias"], m0_rows, min(seq_len, PROMPT_CHUNK))

    if seq_len >= 512:
        xa, new_buf = _pool_branch(proj_a, pool_buf, wts["w_pool_grp"], wts["pool_scale"], nb, seq_len,
                                   nseq=1, tm=512, start=start)
        xc = _mem_attn(proj_b, mem_k, mem_v, nb, seq_len, nseq=1, tm=512)
    else:
        xa, new_buf = _pool_branch(proj_a, pool_buf, wts["w_pool_grp"], wts["pool_scale"], nb, seq_len,
                                   nseq=32, tm=seq_len, start=start)
        xc = _mem_attn_short(proj_b, mem_k, mem_v, nb, seq_len, nseq=8)

    if fresh:
        xb, c_new, n_new, mcol = _mlstm_prompt(proj_a, g, gt, nb, seq_len)
    else:
        xb, c_new, n_new, mcol = _mlstm_sample(proj_a, g, gt, c0, n0, nb, seq_len)
    m_new = mcol.reshape(nb, seq_len, LANES)[:, seq_len - 1, :HEADS]

    y = _final(xa, xb, xc, proj_b, x2d, wts["g_post"], wts["w_br_pool"], wts["w_br_mlstm"],
               wts["w_br_mem"], wts["w_out"], tm=256)
    return y.reshape(nb, seq_len, d), new_buf, c_new, n_new, m_new


def kernel(x_prompt, x_sample, state_pool, state_mlstm_C, state_mlstm_n, state_mlstm_m, cache_mem_k,
           cache_mem_v, mem_prompt, g_pre, g_post, w_in, b_mlstm_i, b_mlstm_f, w_pool_grp, pool_scale,
           g_mem, w_mem_kv, w_br_pool, w_br_mlstm, w_br_mem, w_out):
    nbp = x_prompt.shape[0]
    assert w_in.shape[0] == 1, "single-layer problem: the kernels index layer 0 of the stacked weights"
    l = 0
    wts = {
        "g_pre": g_pre[l][None, :],
        "g_post": g_post[l][None, :],
        "w_in": w_in,
        "gate_bias": jnp.pad(jnp.concatenate([b_mlstm_i[l], b_mlstm_f[l]]), (0, LANES - N_GATE_COLS))[None, :],
        "w_pool_grp": w_pool_grp[l].astype(BF16),
        "pool_scale": pool_scale[l][None, :],
        "w_br_pool": w_br_pool[l].astype(BF16),
        "w_br_mlstm": w_br_mlstm[l].astype(BF16),
        "w_br_mem": w_br_mem[l].astype(BF16),
        "w_out": w_out[l].astype(BF16),
    }
    mk2d, mv2d = _mem_kv(mem_prompt.reshape(nbp * MEM_TOKENS, D_MODEL), g_mem[l][None, :],
                         w_mem_kv[l].astype(BF16), tm=256)
    mk3, mv3 = mk2d.reshape(nbp, MEM_TOKENS, MEM_WIDTH), mv2d.reshape(nbp, MEM_TOKENS, MEM_WIDTH)
    yp, pool_p, c_p, n_p, m_p = _mixer_group(
        x_prompt, jnp.zeros((nbp, POOL_BUF, POOL_WIDTH), F32), None, None, None, mk3, mv3, wts, 0)
    ys, pool_s, c_s, n_s, m_s = _mixer_group(
        x_sample, state_pool[l], state_mlstm_C[l], state_mlstm_n[l], state_mlstm_m[l],
        cache_mem_k[l], cache_mem_v[l], wts, PAST_LEN)
    kv_shape = (1, nbp, MEM_TOKENS, HEADS, MEM_HEAD_DIM)
    return (yp, ys, pool_p[None], c_p[None], n_p[None], m_p[None],
            mk2d.reshape(kv_shape), mv2d.reshape(kv_shape),
            pool_s[None], c_s[None], n_s[None], m_s[None])
```

```python
import functools

import jax
import jax.numpy as jnp
from jax import lax
from jax.experimental import pallas as pl
from jax.experimental.pallas import tpu as pltpu

F32 = jnp.float32
BF16 = jnp.bfloat16

D_MODEL = 2048
PAST_LEN = 16384
POOL_WINDOWS = (2, 4, 8, 16)
POOL_GROUP_WIDTH = 256
POOL_WIDTH = 1024
POOL_BUF = 15
HEADS = 4
MLSTM_HEAD_DIM = 512
MLSTM_WIDTH = 2048
PROMPT_CHUNK = 256
MEM_TOKENS = 256
MEM_HEAD_DIM = 256
MEM_WIDTH = 1024
EPS = 1e-6
N_GATE_COLS = 2 * HEADS
GATE_COL0 = 2 * POOL_WIDTH + 5 * MLSTM_WIDTH
LANES = 128
SUBLANES = 8
COL_PV, COL_PZ = 0, 1024
COL_Q, COL_K, COL_V, COL_O, COL_Z = 2048, 4096, 6144, 8192, 10240
COL_CQ, COL_CZ = 0, 1024
COL_GA, COL_GB, COL_GC = 2048, 4096, 6144
N_PROJ_A = GATE_COL0
N_PROJ_B = 2 * MEM_WIDTH + 3 * D_MODEL
VMEM_LIMIT = 56 * 1024 * 1024


def _cparams(sem):
    return pltpu.CompilerParams(dimension_semantics=sem, vmem_limit_bytes=VMEM_LIMIT)


def _sigmoid(x):
    return 1.0 / (1.0 + jnp.exp(-x))


def _silu(x):
    return x * _sigmoid(x)


def _norm_kernel(x_ref, g_ref, o_ref):
    x = x_ref[...]
    ms = jnp.mean(x * x, axis=-1, keepdims=True)
    o_ref[...] = (x * lax.rsqrt(ms + EPS) * g_ref[...]).astype(o_ref.dtype)


def _rmsnorm_bf16(x2d, g_row, tm):
    t, d = x2d.shape
    return pl.pallas_call(
        _norm_kernel,
        grid=(t // tm,),
        in_specs=[pl.BlockSpec((tm, d), lambda i: (i, 0)),
                  pl.BlockSpec((1, d), lambda i: (0, 0))],
        out_specs=pl.BlockSpec((tm, d), lambda i: (i, 0)),
        out_shape=jax.ShapeDtypeStruct((t, d), BF16),
        compiler_params=_cparams(("parallel",)),
        name="rmsnorm",
    )(x2d, g_row)


def _inproj_kernel(h_ref, w_ref, o_ref, wbf_ref):
    @pl.when(pl.program_id(1) == 0)
    def _():
        wbf_ref[...] = w_ref[...].astype(BF16)

    o_ref[...] = lax.dot_general(h_ref[...], wbf_ref[...], (((1,), (1,)), ((), ())), preferred_element_type=F32)


def _inproj(h, w_t, row0, n_cols, tm, tn):
    t, d = h.shape
    assert row0 % SUBLANES == 0 and n_cols % tn == 0 and t % tm == 0 and row0 + n_cols <= w_t.shape[0]
    return pl.pallas_call(
        _inproj_kernel,
        grid=(n_cols // tn, t // tm),
        in_specs=[pl.BlockSpec((tm, d), lambda j, i: (i, 0)),
                  pl.BlockSpec((pl.Element(tn), pl.Element(d)),
                               lambda j, i: (pl.multiple_of(row0 + j * tn, SUBLANES), 0))],
        out_specs=pl.BlockSpec((tm, tn), lambda j, i: (i, j)),
        out_shape=jax.ShapeDtypeStruct((t, n_cols), F32),
        scratch_shapes=[pltpu.VMEM((tn, d), BF16)],
        compiler_params=_cparams(("arbitrary", "arbitrary")),
        name="inproj",
    )(h, w_t)


GATE_TILE = 512


def _gate_kernel(h_ref, wg_ref, bias_ref, m0_ref, g_ref, gt_ref, *, seg, ct):
    row = lax.broadcasted_iota(jnp.int32, (ct, ct), 0)
    col = lax.broadcasted_iota(jnp.int32, (ct, ct), 1)
    same = col <= row
    if seg < ct:
        shift = seg.bit_length() - 1
        same = same & (lax.shift_right_logical(col, shift) == lax.shift_right_logical(row, shift))
    same = same.astype(F32)
    lane = lax.broadcasted_iota(jnp.int32, (ct, LANES), 1)
    wg = wg_ref[...].astype(BF16)
    for r in range(GATE_TILE // ct):
        rs = slice(r * ct, (r + 1) * ct)
        x = lax.dot_general(h_ref[rs, :], wg, (((1,), (1,)), ((), ())), preferred_element_type=F32) + bias_ref[...]
        lf = jnp.minimum(x, 0.0) - jnp.log1p(jnp.exp(-jnp.abs(x)))
        cums = jnp.dot(same, lf, precision=lax.Precision.HIGHEST, preferred_element_type=F32)
        g = jnp.where(lane < HEADS, x, jnp.where(lane < 2 * HEADS, cums, m0_ref[rs, :]))
        g_ref[rs, :] = g
        gt_ref[:, rs] = g.T[:8, :]


def _gate_prep(h, w3, bias_row, m0_rows, seg):
    t, d = h.shape
    tg = GATE_TILE
    ct = max(seg, LANES)
    assert GATE_COL0 % LANES == 0 and tg % ct == 0 and ct % seg == 0 and seg & (seg - 1) == 0
    return pl.pallas_call(
        functools.partial(_gate_kernel, seg=seg, ct=ct),
        grid=(t // tg,),
        in_specs=[pl.BlockSpec((tg, d), lambda i: (i, 0)),
                  pl.BlockSpec((LANES, d), lambda i: (GATE_COL0 // LANES, 0)),
                  pl.BlockSpec((1, LANES), lambda i: (0, 0)),
                  pl.BlockSpec((tg, LANES), lambda i: (i, 0))],
        out_specs=[pl.BlockSpec((tg, LANES), lambda i: (i, 0)),
                   pl.BlockSpec((8, tg), lambda i: (0, i))],
        out_shape=[jax.ShapeDtypeStruct((t, LANES), F32),
                   jax.ShapeDtypeStruct((8, t), F32)],
        compiler_params=_cparams(("parallel",)),
        name="gate_prep",
    )(h, w3, bias_row, m0_rows)


def _pool_kernel(pv_ref, pz_ref, buf_ref, wg_ref, scale_ref, xa_ref, nbuf_ref, s_ref, p_ref,
                 *, nseq, tm, nt, start):
    j = pl.program_id(1)

    def one_seq(b, r0):
        if nt > 1:
            @pl.when(j == 0)
            def _():
                s_ref[1:16, :] = buf_ref[b]
        else:
            s_ref[1:16, :] = buf_ref[b]
        s_ref[16:16 + tm, :] = pv_ref[pl.ds(r0, tm), :]
        pos = start + j * tm + lax.broadcasted_iota(jnp.int32, (tm, 1), 0)
        for g, w in enumerate(POOL_WINDOWS):
            cs = slice(g * POOL_GROUP_WIDTH, (g + 1) * POOL_GROUP_WIDTH)
            acc = s_ref[16:16 + tm, cs]
            for i in range(1, w):
                acc = acc + s_ref[16 - i:16 - i + tm, cs]
            cnt = jnp.minimum(w, pos + 1).astype(F32)
            p_ref[pl.ds(r0, tm), cs] = acc / cnt - s_ref[16:16 + tm, cs]
        tail = s_ref[tm + 1:tm + 16, :]
        nbuf_ref[b] = tail
        if nt > 1:
            s_ref[1:16, :] = tail

    if nseq == 1:
        one_seq(0, 0)
    else:
        def body(b, carry):
            one_seq(b, pl.multiple_of(b * tm, 8))
            return carry
        lax.fori_loop(0, nseq, body, 0)

    for g in range(len(POOL_WINDOWS)):
        cs = slice(g * POOL_GROUP_WIDTH, (g + 1) * POOL_GROUP_WIDTH)
        pa = jnp.dot(p_ref[:, cs].astype(BF16), wg_ref[g], preferred_element_type=F32) * scale_ref[:, cs]
        xa_ref[:, cs] = (pa * _silu(pz_ref[:, cs])).astype(xa_ref.dtype)


def _pool_branch(proj, buf, w_grp, scale_row, nb, seq_len, nseq, tm, start):
    assert nseq == 1 or tm == seq_len
    nt = seq_len // tm
    rows = nseq * tm
    t = nb * seq_len
    pvb, pzb = COL_PV // POOL_WIDTH, COL_PZ // POOL_WIDTH
    return pl.pallas_call(
        functools.partial(_pool_kernel, nseq=nseq, tm=tm, nt=nt, start=start),
        grid=(nb // nseq, nt),
        in_specs=[pl.BlockSpec((rows, POOL_WIDTH), lambda i, j: (i * nt + j, pvb)),
                  pl.BlockSpec((rows, POOL_WIDTH), lambda i, j: (i * nt + j, pzb)),
                  pl.BlockSpec((nseq, POOL_BUF, POOL_WIDTH), lambda i, j: (i, 0, 0)),
                  pl.BlockSpec((4, POOL_GROUP_WIDTH, POOL_GROUP_WIDTH), lambda i, j: (0, 0, 0)),
                  pl.BlockSpec((1, POOL_WIDTH), lambda i, j: (0, 0))],
        out_specs=[pl.BlockSpec((rows, POOL_WIDTH), lambda i, j: (i * nt + j, 0)),
                   pl.BlockSpec((nseq, POOL_BUF, POOL_WIDTH), lambda i, j: (i, 0, 0))],
        out_shape=[jax.ShapeDtypeStruct((t, POOL_WIDTH), BF16),
                   jax.ShapeDtypeStruct((nb, POOL_BUF, POOL_WIDTH), F32)],
        scratch_shapes=[pltpu.VMEM((16 + tm, POOL_WIDTH), F32),
                        pltpu.VMEM((rows, POOL_WIDTH), F32)],
        compiler_params=_cparams(("arbitrary", "arbitrary")),
        name="pool_branch",
    )(proj, proj, buf, w_grp, scale_row)


def _memkv_kernel(m_ref, g_ref, w_ref, k_ref, v_ref):
    x = m_ref[...]
    ms = jnp.mean(x * x, axis=-1, keepdims=True)
    h = (x * lax.rsqrt(ms + EPS) * g_ref[...]).astype(BF16)
    kv = jnp.dot(h, w_ref[...], preferred_element_type=F32)
    k_ref[...] = kv[:, :MEM_WIDTH]
    v_ref[...] = kv[:, MEM_WIDTH:]


def _mem_kv(mem2d, g_row, w_bf, tm):
    t, d = mem2d.shape
    return pl.pallas_call(
        _memkv_kernel,
        grid=(t // tm,),
        in_specs=[pl.BlockSpec((tm, d), lambda i: (i, 0)),
                  pl.BlockSpec((1, d), lambda i: (0, 0)),
                  pl.BlockSpec((d, 2 * MEM_WIDTH), lambda i: (0, 0))],
        out_specs=[pl.BlockSpec((tm, MEM_WIDTH), lambda i: (i, 0)),
                   pl.BlockSpec((tm, MEM_WIDTH), lambda i: (i, 0))],
        out_shape=[jax.ShapeDtypeStruct((t, MEM_WIDTH), F32),
                   jax.ShapeDtypeStruct((t, MEM_WIDTH), F32)],
        compiler_params=_cparams(("parallel",)),
        name="mem_kv",
    )(mem2d, g_row, w_bf)


def _attn_kernel(cq_ref, cz_ref, k_ref, v_ref, xc_ref, *, nseq, tm):
    scale = MEM_HEAD_DIM ** -0.5
    for b in range(nseq):
        rs = slice(b * tm, (b + 1) * tm)
        for h in range(HEADS):
            cs = slice(h * MEM_HEAD_DIM, (h + 1) * MEM_HEAD_DIM)
            q = cq_ref[rs, cs].astype(BF16)
            k = k_ref[b, :, cs].astype(BF16)
            v = v_ref[b, :, cs].astype(BF16)
            s = lax.dot_general(q, k, (((1,), (1,)), ((), ())), preferred_element_type=F32) * scale
            p = jnp.exp(s - jnp.max(s, axis=-1, keepdims=True))
            a = p / jnp.sum(p, axis=-1, keepdims=True)
            o = jnp.dot(a.astype(BF16), v, preferred_element_type=F32)
            xc_ref[rs, cs] = (o * _silu(cz_ref[rs, cs])).astype(xc_ref.dtype)


def _mem_attn(proj, mem_k, mem_v, nb, seq_len, nseq, tm):
    assert nseq == 1 or tm == seq_len
    nt = seq_len // tm
    rows = nseq * tm
    t = nb * seq_len
    cqb, czb = COL_CQ // MEM_WIDTH, COL_CZ // MEM_WIDTH
    return pl.pallas_call(
        functools.partial(_attn_kernel, nseq=nseq, tm=tm),
        grid=(nb // nseq, nt),
        in_specs=[pl.BlockSpec((rows, MEM_WIDTH), lambda i, j: (i * nt + j, cqb)),
                  pl.BlockSpec((rows, MEM_WIDTH), lambda i, j: (i * nt + j, czb)),
                  pl.BlockSpec((nseq, MEM_TOKENS, MEM_WIDTH), lambda i, j: (i, 0, 0)),
                  pl.BlockSpec((nseq, MEM_TOKENS, MEM_WIDTH), lambda i, j: (i, 0, 0))],
        out_specs=pl.BlockSpec((rows, MEM_WIDTH), lambda i, j: (i * nt + j, 0)),
        out_shape=jax.ShapeDtypeStruct((t, MEM_WIDTH), BF16),
        compiler_params=_cparams(("parallel", "arbitrary")),
        name="mem_attn",
    )(proj, proj, mem_k, mem_v)


def _attn_short_kernel(cq_ref, cz_ref, k_ref, v_ref, xc_ref, *, nseq, tm):
    scale = MEM_HEAD_DIM ** -0.5
    nr, nc = tm * HEADS, MEM_TOKENS * HEADS
    row_head = lax.broadcasted_iota(jnp.int32, (nr, nc), 0) // tm
    col_head = lax.broadcasted_iota(jnp.int32, (nr, nc), 1) % HEADS
    same_head = row_head == col_head
    for b in range(nseq):
        rs = slice(b * tm, (b + 1) * tm)
        kf = k_ref[b].reshape(nc, MEM_HEAD_DIM).astype(BF16)
        vf = v_ref[b].reshape(nc, MEM_HEAD_DIM).astype(BF16)
        q = jnp.concatenate([cq_ref[rs, h * MEM_HEAD_DIM:(h + 1) * MEM_HEAD_DIM] for h in range(HEADS)], axis=0)
        s = lax.dot_general(q.astype(BF16), kf, (((1,), (1,)), ((), ())), preferred_element_type=F32) * scale
        s = jnp.where(same_head, s, -jnp.inf)
        p = jnp.exp(s - jnp.max(s, axis=-1, keepdims=True))
        a = p / jnp.sum(p, axis=-1, keepdims=True)
        o = jnp.dot(a.astype(BF16), vf, preferred_element_type=F32)
        for h in range(HEADS):
            cs = slice(h * MEM_HEAD_DIM, (h + 1) * MEM_HEAD_DIM)
            xc_ref[rs, cs] = o[h * tm:(h + 1) * tm, :] * _silu(cz_ref[rs, cs])


def _mem_attn_short(proj, mem_k, mem_v, nb, seq_len, nseq):
    rows = nseq * seq_len
    t = nb * seq_len
    cqb, czb = COL_CQ // MEM_WIDTH, COL_CZ // MEM_WIDTH
    kv_spec = pl.BlockSpec((nseq, MEM_TOKENS, HEADS, MEM_HEAD_DIM), lambda i: (i, 0, 0, 0))
    return pl.pallas_call(
        functools.partial(_attn_short_kernel, nseq=nseq, tm=seq_len),
        grid=(nb // nseq,),
        in_specs=[pl.BlockSpec((rows, MEM_WIDTH), lambda i: (i, cqb)),
                  pl.BlockSpec((rows, MEM_WIDTH), lambda i: (i, czb)),
                  kv_spec, kv_spec],
        out_specs=pl.BlockSpec((rows, MEM_WIDTH), lambda i: (i, 0)),
        out_shape=jax.ShapeDtypeStruct((t, MEM_WIDTH), F32),
        compiler_params=_cparams(("parallel",)),
        name="mem_attn_short",
    )(proj, proj, mem_k, mem_v)


def _mlstm_intra(q_bf, ks_bf, v_bf, ig_row, b_row, b_col, m0_col, mask):
    dlog = jnp.where(mask, b_col - b_row + ig_row, -jnp.inf)
    inter = b_col + m0_col
    m_col = jnp.maximum(inter, jnp.max(dlog, axis=1, keepdims=True))
    dw = jnp.exp(dlog - m_col)
    inter_w = jnp.exp(inter - m_col)
    s = lax.dot_general(q_bf, ks_bf, (((1,), (1,)), ((), ())), preferred_element_type=F32) * dw
    num = jnp.dot(s.astype(BF16), v_bf, preferred_element_type=F32)
    den = jnp.sum(s, axis=1, keepdims=True)
    return num, den, m_col, inter_w


def _mlstm_prompt_kernel(q_ref, k_ref, v_ref, o_ref, z_ref, g_ref, gt_ref,
                         xb_ref, c_ref, n_ref, mcol_ref, m_scr):
    @pl.when(pl.program_id(1) == 0)
    def _():
        c_ref[...] = jnp.zeros_like(c_ref)
        n_ref[...] = jnp.zeros_like(n_ref)
        m_scr[...] = jnp.zeros_like(m_scr)

    cl = PROMPT_CHUNK
    row = lax.broadcasted_iota(jnp.int32, (cl, cl), 0)
    col = lax.broadcasted_iota(jnp.int32, (cl, cl), 1)
    mask = col <= row
    kscale = MLSTM_HEAD_DIM ** -0.5
    mcol_ref[...] = jnp.zeros_like(mcol_ref)
    for h in range(HEADS):
        cs = slice(h * MLSTM_HEAD_DIM, (h + 1) * MLSTM_HEAD_DIM)
        ig_col = g_ref[:, h:h + 1]
        b_col = g_ref[:, HEADS + h:HEADS + h + 1]
        ig_row = gt_ref[h:h + 1, :]
        b_row = gt_ref[HEADS + h:HEADS + h + 1, :]
        m0 = m_scr[h:h + 1, 0:1]
        qf = q_ref[:, cs]
        kf = k_ref[:, cs] * kscale
        vf = v_ref[:, cs]
        q_bf, ks_bf = qf.astype(BF16), kf.astype(BF16)
        num, den, m_col, inter_w = _mlstm_intra(q_bf, ks_bf, vf.astype(BF16), ig_row, b_row, b_col, m0, mask)
        c0 = c_ref[0, h]
        n0 = n_ref[0, h:h + 1, :]
        num = num + inter_w * lax.dot_general(q_bf, c0.astype(BF16), (((1,), (1,)), ((), ())),
                                              preferred_element_type=F32)
        den = den + inter_w * jnp.sum(qf * n0, axis=1, keepdims=True)
        hval = num / jnp.maximum(jnp.abs(den), jnp.exp(-m_col))
        xb_ref[:, cs] = (_sigmoid(o_ref[:, cs]) * hval * _silu(z_ref[:, cs])).astype(xb_ref.dtype)
        m_new = m_col[cl - 1:cl, :]
        b_last = b_col[cl - 1:cl, :]
        w_col = jnp.exp(b_last - b_col + ig_col - m_new)
        decay = jnp.exp(b_last + m0 - m_new)
        dc = lax.dot_general((w_col * vf).astype(BF16), ks_bf, (((0,), (0,)), ((), ())),
                             preferred_element_type=F32)
        c_ref[0, h] = decay * c0 + dc
        n_ref[0, h:h + 1, :] = decay * n0 + jnp.sum(w_col * kf, axis=0, keepdims=True)
        m_scr[h:h + 1, :] = jnp.broadcast_to(m_new, (1, LANES))
        mcol_ref[:, h:h + 1] = m_col


def _mlstm_prompt(proj, g, gt, nb, seq_len):
    cl = PROMPT_CHUNK
    nc = seq_len // cl
    t = nb * seq_len
    w = MLSTM_WIDTH

    def colspec(col0):
        return pl.BlockSpec((cl, w), lambda b, c: (b * nc + c, col0 // w))

    return pl.pallas_call(
        _mlstm_prompt_kernel,
        grid=(nb, nc),
        in_specs=[colspec(COL_Q), colspec(COL_K), colspec(COL_V), colspec(COL_O), colspec(COL_Z),
                  pl.BlockSpec((cl, LANES), lambda b, c: (b * nc + c, 0)),
                  pl.BlockSpec((8, cl), lambda b, c: (0, b * nc + c))],
        out_specs=[pl.BlockSpec((cl, w), lambda b, c: (b * nc + c, 0)),
                   pl.BlockSpec((1, HEADS, MLSTM_HEAD_DIM, MLSTM_HEAD_DIM), lambda b, c: (b, 0, 0, 0)),
                   pl.BlockSpec((1, HEADS, MLSTM_HEAD_DIM), lambda b, c: (b, 0, 0)),
                   pl.BlockSpec((cl, LANES), lambda b, c: (b * nc + c, 0))],
        out_shape=[jax.ShapeDtypeStruct((t, w), BF16),
                   jax.ShapeDtypeStruct((nb, HEADS, MLSTM_HEAD_DIM, MLSTM_HEAD_DIM), F32),
                   jax.ShapeDtypeStruct((nb, HEADS, MLSTM_HEAD_DIM), F32),
                   jax.ShapeDtypeStruct((t, LANES), F32)],
        scratch_shapes=[pltpu.VMEM((8, LANES), F32)],
        compiler_params=_cparams(("arbitrary", "arbitrary")),
        name="mlstm_prompt",
    )(proj, proj, proj, proj, proj, g, gt)


def _mlstm_sample_kernel(q_ref, k_ref, v_ref, o_ref, z_ref, g_ref, gt_ref, c0_ref, n0_ref,
                         xb_ref, c_ref, n_ref, mcol_ref, num_scr, den_scr, iw_scr, *, seq_len):
    bl = pl.program_id(1)
    tl = LANES
    kscale = MLSTM_HEAD_DIM ** -0.5
    shift = seq_len.bit_length() - 1

    @pl.when(bl == 0)
    def _():
        row = lax.broadcasted_iota(jnp.int32, (tl, tl), 0)
        col = lax.broadcasted_iota(jnp.int32, (tl, tl), 1)
        mask = (col <= row) & (lax.shift_right_logical(col, shift) == lax.shift_right_logical(row, shift))
        mcol_ref[...] = jnp.zeros_like(mcol_ref)
        for h in range(HEADS):
            cs = slice(h * MLSTM_HEAD_DIM, (h + 1) * MLSTM_HEAD_DIM)
            b_col = g_ref[:, HEADS + h:HEADS + h + 1]
            m0_col = g_ref[:, 2 * HEADS + h:2 * HEADS + h + 1]
            ig_row = gt_ref[h:h + 1, :]
            b_row = gt_ref[HEADS + h:HEADS + h + 1, :]
            q_bf = q_ref[:, cs].astype(BF16)
            ks_bf = (k_ref[:, cs] * kscale).astype(BF16)
            num, den, m_col, inter_w = _mlstm_intra(q_bf, ks_bf, v_ref[:, cs].astype(BF16),
                                                    ig_row, b_row, b_col, m0_col, mask)
            num_scr[:, cs] = num
            den_scr[:, h:h + 1] = den
            iw_scr[:, h:h + 1] = inter_w
            mcol_ref[:, h:h + 1] = m_col

    r0 = pl.multiple_of(bl * seq_len, 8)
    rows = pl.ds(r0, seq_len)
    last = pl.ds(r0 + seq_len - 1, 1)
    for h in range(HEADS):
        cs = slice(h * MLSTM_HEAD_DIM, (h + 1) * MLSTM_HEAD_DIM)
        qf = q_ref[rows, cs]
        kf = k_ref[rows, cs] * kscale
        vf = v_ref[rows, cs]
        ig_col = g_ref[rows, h:h + 1]
        b_col = g_ref[rows, HEADS + h:HEADS + h + 1]
        b_last = g_ref[last, HEADS + h:HEADS + h + 1]
        m0 = g_ref[last, 2 * HEADS + h:2 * HEADS + h + 1]
        m_col = mcol_ref[rows, h:h + 1]
        m_new = mcol_ref[last, h:h + 1]
        inter_w = iw_scr[rows, h:h + 1]
        c0 = c0_ref[0, h]
        n0 = n0_ref[0, h:h + 1, :]
        num = num_scr[rows, cs] + inter_w * lax.dot_general(
            qf.astype(BF16), c0.astype(BF16), (((1,), (1,)), ((), ())), preferred_element_type=F32)
        den = den_scr[rows, h:h + 1] + inter_w * jnp.sum(qf * n0, axis=1, keepdims=True)
        hval = num / jnp.maximum(jnp.abs(den), jnp.exp(-m_col))
        xb_ref[rows, cs] = (_sigmoid(o_ref[rows, cs]) * hval * _silu(z_ref[rows, cs])).astype(xb_ref.dtype)
        w_col = jnp.exp(b_last - b_col + ig_col - m_new)
        decay = jnp.exp(b_last + m0 - m_new)
        dc = lax.dot_general((w_col * vf).astype(BF16), kf.astype(BF16), (((0,), (0,)), ((), ())),
                             preferred_element_type=F32)
        c_ref[0, h] = decay * c0 + dc
        n_ref[0, h:h + 1, :] = decay * n0 + jnp.sum(w_col * kf, axis=0, keepdims=True)


def _mlstm_sample(proj, g, gt, c0, n0, nb, seq_len):
    tl = LANES
    per = tl // seq_len
    ng = nb // per
    t = nb * seq_len
    w = MLSTM_WIDTH
    d = MLSTM_HEAD_DIM

    def colspec(col0):
        return pl.BlockSpec((tl, w), lambda i, b: (i, col0 // w))

    state4 = pl.BlockSpec((1, HEADS, d, d), lambda i, b: (i * per + b, 0, 0, 0))
    state3 = pl.BlockSpec((1, HEADS, d), lambda i, b: (i * per + b, 0, 0))
    return pl.pallas_call(
        functools.partial(_mlstm_sample_kernel, seq_len=seq_len),
        grid=(ng, per),
        in_specs=[colspec(COL_Q), colspec(COL_K), colspec(COL_V), colspec(COL_O), colspec(COL_Z),
                  pl.BlockSpec((tl, LANES), lambda i, b: (i, 0)),
                  pl.BlockSpec((8, tl), lambda i, b: (0, i)),
                  state4, state3],
        out_specs=[pl.BlockSpec((tl, w), lambda i, b: (i, 0)), state4, state3,
                   pl.BlockSpec((tl, LANES), lambda i, b: (i, 0))],
        out_shape=[jax.ShapeDtypeStruct((t, w), F32),
                   jax.ShapeDtypeStruct((nb, HEADS, d, d), F32),
                   jax.ShapeDtypeStruct((nb, HEADS, d), F32),
                   jax.ShapeDtypeStruct((t, LANES), F32)],
        scratch_shapes=[pltpu.VMEM((tl, w), F32), pltpu.VMEM((tl, LANES), F32), pltpu.VMEM((tl, LANES), F32)],
        compiler_params=_cparams(("arbitrary", "arbitrary")),
        name="mlstm_sample",
    )(proj, proj, proj, proj, proj, g, gt, c0, n0)


def _final_kernel(xa_ref, xb_ref, xc_ref, ga_ref, gb_ref, gc_ref, x_ref, gpost_ref,
                  wa_ref, wb_ref, wc_ref, wo_ref, y_ref):
    ya = jnp.dot(xa_ref[...].astype(BF16), wa_ref[...], preferred_element_type=F32)
    merged = _sigmoid(ga_ref[...]) * ya
    yb = jnp.dot(xb_ref[...].astype(BF16), wb_ref[...], preferred_element_type=F32)
    merged = merged + _sigmoid(gb_ref[...]) * yb
    yc = jnp.dot(xc_ref[...].astype(BF16), wc_ref[...], preferred_element_type=F32)
    merged = merged + _sigmoid(gc_ref[...]) * yc
    out = jnp.dot(merged.astype(BF16), wo_ref[...], preferred_element_type=F32)
    ms = jnp.mean(out * out, axis=-1, keepdims=True)
    y_ref[...] = x_ref[...] + out * lax.rsqrt(ms + EPS) * gpost_ref[...]


def _final(xa, xb, xc, proj, x2d, gpost_row, wa, wb, wc, wo, tm):
    t, d = x2d.shape

    def rowspec(width, colblk=0):
        return pl.BlockSpec((tm, width), lambda i: (i, colblk))

    def resident(shape):
        return pl.BlockSpec(shape, lambda i: (0, 0), pipeline_mode=pl.Buffered(1))

    return pl.pallas_call(
        _final_kernel,
        grid=(t // tm,),
        in_specs=[rowspec(POOL_WIDTH), rowspec(MLSTM_WIDTH), rowspec(MEM_WIDTH),
                  rowspec(d, COL_GA // d), rowspec(d, COL_GB // d), rowspec(d, COL_GC // d),
                  rowspec(d), resident((1, d)),
                  resident(wa.shape), resident(wb.shape), resident(wc.shape), resident(wo.shape)],
        out_specs=rowspec(d),
        out_shape=jax.ShapeDtypeStruct((t, d), F32),
        compiler_params=_cparams(("parallel",)),
        name="merge_out",
    )(xa, xb, xc, proj, proj, proj, x2d, gpost_row, wa, wb, wc, wo)


def _mixer_group(x, pool_buf, c0, n0, m0, mem_k, mem_v, wts, start):
    nb, seq_len, d = x.shape
    t = nb * seq_len
    x2d = x.reshape(t, d)
    h = _rmsnorm_bf16(x2d, wts["g_pre"], tm=512)
    proj_a = _inproj(h, wts["w_in"], 0, N_PROJ_A, tm=min(t, 1024), tn=1024)
    proj_b = _inproj(h, wts["w_in"], GATE_COL0 + N_GATE_COLS, N_PROJ_B, tm=min(t, 1024), tn=1024)

    fresh = c0 is None
    if fresh:
        m0_rows = jnp.zeros((t, LANES), F32)
    else:
        m0_rows = jnp.pad(jnp.repeat(m0, seq_len, axis=0), ((0, 0), (2 * HEADS, LANES - 3 * HEADS)))
    g, gt = _gate_prep(h, wts["w_in"], wts["gate_bias"], m0_rows, min(seq_len, PROMPT_CHUNK))

    if seq_len >= 512:
        xa, new_buf = _pool_branch(proj_a, pool_buf, wts["w_pool_grp"], wts["pool_scale"], nb, seq_len,
                                   nseq=1, tm=512, start=start)
        xc = _mem_attn(proj_b, mem_k, mem_v, nb, seq_len, nseq=1, tm=512)
    else:
        xa, new_buf = _pool_branch(proj_a, pool_buf, wts["w_pool_grp"], wts["pool_scale"], nb, seq_len,
                                   nseq=32, tm=seq_len, start=start)
        xc = _mem_attn_short(proj_b, mem_k, mem_v, nb, seq_len, nseq=8)

    if fresh:
        xb, c_new, n_new, mcol = _mlstm_prompt(proj_a, g, gt, nb, seq_len)
    else:
        xb, c_new, n_new, mcol = _mlstm_sample(proj_a, g, gt, c0, n0, nb, seq_len)
    m_new = mcol.reshape(nb, seq_len, LANES)[:, seq_len - 1, :HEADS]

    y = _final(xa, xb, xc, proj_b, x2d, wts["g_post"], wts["w_br_pool"], wts["w_br_mlstm"],
               wts["w_br_mem"], wts["w_out"], tm=256)
    return y.reshape(nb, seq_len, d), new_buf, c_new, n_new, m_new


def kernel(x_prompt, x_sample, state_pool, state_mlstm_C, state_mlstm_n, state_mlstm_m, cache_mem_k,
           cache_mem_v, mem_prompt, g_pre, g_post, w_in, b_mlstm_i, b_mlstm_f, w_pool_grp, pool_scale,
           g_mem, w_mem_kv, w_br_pool, w_br_mlstm, w_br_mem, w_out):
    nbp = x_prompt.shape[0]
    assert w_in.shape[0] == 1, "single-layer problem: the kernels index layer 0 of the stacked weights"
    l = 0
    wts = {
        "g_pre": g_pre[l][None, :],
        "g_post": g_post[l][None, :],
        "w_in": jnp.transpose(w_in[l]),
        "gate_bias": jnp.pad(jnp.concatenate([b_mlstm_i[l], b_mlstm_f[l]]), (0, LANES - N_GATE_COLS))[None, :],
        "w_pool_grp": w_pool_grp[l].astype(BF16),
        "pool_scale": pool_scale[l][None, :],
        "w_br_pool": w_br_pool[l].astype(BF16),
        "w_br_mlstm": w_br_mlstm[l].astype(BF16),
        "w_br_mem": w_br_mem[l].astype(BF16),
        "w_out": w_out[l].astype(BF16),
    }
    mk2d, mv2d = _mem_kv(mem_prompt.reshape(nbp * MEM_TOKENS, D_MODEL), g_mem[l][None, :],
                         w_mem_kv[l].astype(BF16), tm=256)
    mk3, mv3 = mk2d.reshape(nbp, MEM_TOKENS, MEM_WIDTH), mv2d.reshape(nbp, MEM_TOKENS, MEM_WIDTH)
    yp, pool_p, c_p, n_p, m_p = _mixer_group(
        x_prompt, jnp.zeros((nbp, POOL_BUF, POOL_WIDTH), F32), None, None, None, mk3, mv3, wts, 0)
    ys, pool_s, c_s, n_s, m_s = _mixer_group(
        x_sample, state_pool[l], state_mlstm_C[l], state_mlstm_n[l], state_mlstm_m[l],
        cache_mem_k[l], cache_mem_v[l], wts, PAST_LEN)
    kv_shape = (1, nbp, MEM_TOKENS, HEADS, MEM_HEAD_DIM)
    return (yp, ys, pool_p[None], c_p[None], n_p[None], m_p[None],
            mk2d.reshape(kv_shape), mv2d.reshape(kv_shape),
            pool_s[None], c_s[None], n_s[None], m_s[None])
```

```python
import functools

import jax
import jax.numpy as jnp
from jax import lax
from jax.experimental import pallas as pl
from jax.experimental.pallas import tpu as pltpu

F32 = jnp.float32
BF16 = jnp.bfloat16

D_MODEL = 2048
PAST_LEN = 16384
POOL_WINDOWS = (2, 4, 8, 16)
POOL_GROUP_WIDTH = 256
POOL_WIDTH = 1024
POOL_BUF = 15
HEADS = 4
MLSTM_HEAD_DIM = 512
MLSTM_WIDTH = 2048
PROMPT_CHUNK = 256
MEM_TOKENS = 256
MEM_HEAD_DIM = 256
MEM_WIDTH = 1024
EPS = 1e-6
N_GATE_COLS = 2 * HEADS
GATE_COL0 = 2 * POOL_WIDTH + 5 * MLSTM_WIDTH
LANES = 128
SUBLANES = 8
COL_PV, COL_PZ = 0, 1024
COL_Q, COL_K, COL_V, COL_O, COL_Z = 2048, 4096, 6144, 8192, 10240
COL_CQ, COL_CZ = 0, 1024
COL_GA, COL_GB, COL_GC = 2048, 4096, 6144
N_PROJ_A = GATE_COL0
N_PROJ_B = 2 * MEM_WIDTH + 3 * D_MODEL
VMEM_LIMIT = 56 * 1024 * 1024


def _cparams(sem):
    return pltpu.CompilerParams(dimension_semantics=sem, vmem_limit_bytes=VMEM_LIMIT)


def _sigmoid(x):
    return 0.5 * jnp.tanh(0.5 * x) + 0.5


def _silu(x):
    return x * _sigmoid(x)


def _inproj_kernel(h_ref, w_ref, o_ref, wbf_ref):
    @pl.when(pl.program_id(1) == 0)
    def _():
        wbf_ref[...] = w_ref[...].astype(BF16)

    o_ref[...] = lax.dot_general(h_ref[...], wbf_ref[...], (((1,), (1,)), ((), ())), preferred_element_type=F32)


def _inproj(h, w_t, row0, n_cols, tm, tn):
    t, d = h.shape
    assert row0 % SUBLANES == 0 and n_cols % tn == 0 and t % tm == 0 and row0 + n_cols <= w_t.shape[0]
    return pl.pallas_call(
        _inproj_kernel,
        grid=(n_cols // tn, t // tm),
        in_specs=[pl.BlockSpec((tm, d), lambda j, i: (i, 0)),
                  pl.BlockSpec((pl.Element(tn), pl.Element(d)),
                               lambda j, i: (pl.multiple_of(row0 + j * tn, SUBLANES), 0))],
        out_specs=pl.BlockSpec((tm, tn), lambda j, i: (i, j)),
        out_shape=jax.ShapeDtypeStruct((t, n_cols), F32),
        scratch_shapes=[pltpu.VMEM((tn, d), BF16)],
        compiler_params=_cparams(("arbitrary", "arbitrary")),
        name="inproj",
    )(h, w_t)


GATE_TILE = 512


def _gate_kernel(x_ref, gpre_ref, wg_ref, bias_ref, m0_ref, h_ref, g_ref, gt_ref, *, seg, ct):
    x = x_ref[...]
    ms = jnp.mean(x * x, axis=-1, keepdims=True)
    h_ref[...] = (x * lax.rsqrt(ms + EPS) * gpre_ref[...]).astype(h_ref.dtype)
    row = lax.broadcasted_iota(jnp.int32, (ct, ct), 0)
    col = lax.broadcasted_iota(jnp.int32, (ct, ct), 1)
    same = col <= row
    if seg < ct:
        shift = seg.bit_length() - 1
        same = same & (lax.shift_right_logical(col, shift) == lax.shift_right_logical(row, shift))
    same = same.astype(F32)
    lane = lax.broadcasted_iota(jnp.int32, (ct, LANES), 1)
    wg = wg_ref[...].astype(BF16)
    for r in range(GATE_TILE // ct):
        rs = slice(r * ct, (r + 1) * ct)
        x = lax.dot_general(h_ref[rs, :], wg, (((1,), (1,)), ((), ())), preferred_element_type=F32) + bias_ref[...]
        lf = jnp.minimum(x, 0.0) - jnp.log1p(jnp.exp(-jnp.abs(x)))
        cums = jnp.dot(same, lf, precision=lax.Precision.HIGHEST, preferred_element_type=F32)
        g = jnp.where(lane < HEADS, x, jnp.where(lane < 2 * HEADS, cums, m0_ref[rs, :]))
        g_ref[rs, :] = g
        gt_ref[:, rs] = g.T[:8, :]


def _norm_gate_prep(x2d, gpre_row, w_t, bias_row, m0_rows, seg):
    t, d = x2d.shape
    tg = GATE_TILE
    ct = max(seg, LANES)
    assert GATE_COL0 % LANES == 0 and tg % ct == 0 and ct % seg == 0 and seg & (seg - 1) == 0
    return pl.pallas_call(
        functools.partial(_gate_kernel, seg=seg, ct=ct),
        grid=(t // tg,),
        in_specs=[pl.BlockSpec((tg, d), lambda i: (i, 0)),
                  pl.BlockSpec((1, d), lambda i: (0, 0)),
                  pl.BlockSpec((LANES, d), lambda i: (GATE_COL0 // LANES, 0)),
                  pl.BlockSpec((1, LANES), lambda i: (0, 0)),
                  pl.BlockSpec((tg, LANES), lambda i: (i, 0))],
        out_specs=[pl.BlockSpec((tg, d), lambda i: (i, 0)),
                   pl.BlockSpec((tg, LANES), lambda i: (i, 0)),
                   pl.BlockSpec((8, tg), lambda i: (0, i))],
        out_shape=[jax.ShapeDtypeStruct((t, d), BF16),
                   jax.ShapeDtypeStruct((t, LANES), F32),
                   jax.ShapeDtypeStruct((8, t), F32)],
        compiler_params=_cparams(("parallel",)),
        name="norm_gate_prep",
    )(x2d, gpre_row, w_t, bias_row, m0_rows)


def _pool_kernel(pv_ref, pz_ref, buf_ref, wg_ref, scale_ref, xa_ref, nbuf_ref, s_ref, p_ref,
                 *, nseq, tm, nt, start):
    j = pl.program_id(1)

    def one_seq(b, r0):
        if nt > 1:
            @pl.when(j == 0)
            def _():
                s_ref[1:16, :] = buf_ref[b]
        else:
            s_ref[1:16, :] = buf_ref[b]
        s_ref[16:16 + tm, :] = pv_ref[pl.ds(r0, tm), :]
        pos = start + j * tm + lax.broadcasted_iota(jnp.int32, (tm, 1), 0)
        for g, w in enumerate(POOL_WINDOWS):
            cs = slice(g * POOL_GROUP_WIDTH, (g + 1) * POOL_GROUP_WIDTH)
            acc = s_ref[16:16 + tm, cs]
            for i in range(1, w):
                acc = acc + s_ref[16 - i:16 - i + tm, cs]
            inv_cnt = 1.0 / jnp.minimum(w, pos + 1).astype(F32)
            p_ref[pl.ds(r0, tm), cs] = acc * inv_cnt - s_ref[16:16 + tm, cs]
        tail = s_ref[tm + 1:tm + 16, :]
        nbuf_ref[b] = tail
        if nt > 1:
            s_ref[1:16, :] = tail

    if nseq == 1:
        one_seq(0, 0)
    else:
        def body(b, carry):
            one_seq(b, pl.multiple_of(b * tm, 8))
            return carry
        lax.fori_loop(0, nseq, body, 0)

    for g in range(len(POOL_WINDOWS)):
        cs = slice(g * POOL_GROUP_WIDTH, (g + 1) * POOL_GROUP_WIDTH)
        pa = jnp.dot(p_ref[:, cs].astype(BF16), wg_ref[g], preferred_element_type=F32) * scale_ref[:, cs]
        xa_ref[:, cs] = (pa * _silu(pz_ref[:, cs])).astype(xa_ref.dtype)


def _pool_branch(proj, buf, w_grp, scale_row, nb, seq_len, nseq, tm, start):
    assert nseq == 1 or tm == seq_len
    nt = seq_len // tm
    rows = nseq * tm
    t = nb * seq_len
    pvb, pzb = COL_PV // POOL_WIDTH, COL_PZ // POOL_WIDTH
    return pl.pallas_call(
        functools.partial(_pool_kernel, nseq=nseq, tm=tm, nt=nt, start=start),
        grid=(nb // nseq, nt),
        in_specs=[pl.BlockSpec((rows, POOL_WIDTH), lambda i, j: (i * nt + j, pvb)),
                  pl.BlockSpec((rows, POOL_WIDTH), lambda i, j: (i * nt + j, pzb)),
                  pl.BlockSpec((nseq, POOL_BUF, POOL_WIDTH), lambda i, j: (i, 0, 0)),
                  pl.BlockSpec((4, POOL_GROUP_WIDTH, POOL_GROUP_WIDTH), lambda i, j: (0, 0, 0)),
                  pl.BlockSpec((1, POOL_WIDTH), lambda i, j: (0, 0))],
        out_specs=[pl.BlockSpec((rows, POOL_WIDTH), lambda i, j: (i * nt + j, 0)),
                   pl.BlockSpec((nseq, POOL_BUF, POOL_WIDTH), lambda i, j: (i, 0, 0))],
        out_shape=[jax.ShapeDtypeStruct((t, POOL_WIDTH), BF16),
                   jax.ShapeDtypeStruct((nb, POOL_BUF, POOL_WIDTH), F32)],
        scratch_shapes=[pltpu.VMEM((16 + tm, POOL_WIDTH), F32),
                        pltpu.VMEM((rows, POOL_WIDTH), F32)],
        compiler_params=_cparams(("arbitrary", "arbitrary")),
        name="pool_branch",
    )(proj, proj, buf, w_grp, scale_row)


def _memkv_kernel(m_ref, g_ref, w_ref, k_ref, v_ref):
    x = m_ref[...]
    ms = jnp.mean(x * x, axis=-1, keepdims=True)
    h = (x * lax.rsqrt(ms + EPS) * g_ref[...]).astype(BF16)
    kv = jnp.dot(h, w_ref[...], preferred_element_type=F32)
    k_ref[...] = kv[:, :MEM_WIDTH]
    v_ref[...] = kv[:, MEM_WIDTH:]


def _mem_kv(mem2d, g_row, w_bf, tm):
    t, d = mem2d.shape
    return pl.pallas_call(
        _memkv_kernel,
        grid=(t // tm,),
        in_specs=[pl.BlockSpec((tm, d), lambda i: (i, 0)),
                  pl.BlockSpec((1, d), lambda i: (0, 0)),
                  pl.BlockSpec((d, 2 * MEM_WIDTH), lambda i: (0, 0))],
        out_specs=[pl.BlockSpec((tm, MEM_WIDTH), lambda i: (i, 0)),
                   pl.BlockSpec((tm, MEM_WIDTH), lambda i: (i, 0))],
        out_shape=[jax.ShapeDtypeStruct((t, MEM_WIDTH), F32),
                   jax.ShapeDtypeStruct((t, MEM_WIDTH), F32)],
        compiler_params=_cparams(("parallel",)),
        name="mem_kv",
    )(mem2d, g_row, w_bf)


def _attn_kernel(cq_ref, cz_ref, k_ref, v_ref, xc_ref, *, nseq, tm):
    scale = MEM_HEAD_DIM ** -0.5
    for b in range(nseq):
        rs = slice(b * tm, (b + 1) * tm)
        for h in range(HEADS):
            cs = slice(h * MEM_HEAD_DIM, (h + 1) * MEM_HEAD_DIM)
            q = cq_ref[rs, cs].astype(BF16)
            k = k_ref[b, :, cs].astype(BF16)
            v = v_ref[b, :, cs].astype(BF16)
            s = lax.dot_general(q, k, (((1,), (1,)), ((), ())), preferred_element_type=F32) * scale
            p = jnp.exp(s - jnp.max(s, axis=-1, keepdims=True))
            a = p * (1.0 / jnp.sum(p, axis=-1, keepdims=True))
            o = jnp.dot(a.astype(BF16), v, preferred_element_type=F32)
            xc_ref[rs, cs] = (o * _silu(cz_ref[rs, cs])).astype(xc_ref.dtype)


def _mem_attn(proj, mem_k, mem_v, nb, seq_len, nseq, tm):
    assert nseq == 1 or tm == seq_len
    nt = seq_len // tm
    rows = nseq * tm
    t = nb * seq_len
    cqb, czb = COL_CQ // MEM_WIDTH, COL_CZ // MEM_WIDTH
    return pl.pallas_call(
        functools.partial(_attn_kernel, nseq=nseq, tm=tm),
        grid=(nb // nseq, nt),
        in_specs=[pl.BlockSpec((rows, MEM_WIDTH), lambda i, j: (i * nt + j, cqb)),
                  pl.BlockSpec((rows, MEM_WIDTH), lambda i, j: (i * nt + j, czb)),
                  pl.BlockSpec((nseq, MEM_TOKENS, MEM_WIDTH), lambda i, j: (i, 0, 0)),
                  pl.BlockSpec((nseq, MEM_TOKENS, MEM_WIDTH), lambda i, j: (i, 0, 0))],
        out_specs=pl.BlockSpec((rows, MEM_WIDTH), lambda i, j: (i * nt + j, 0)),
        out_shape=jax.ShapeDtypeStruct((t, MEM_WIDTH), BF16),
        compiler_params=_cparams(("parallel", "arbitrary")),
        name="mem_attn",
    )(proj, proj, mem_k, mem_v)


def _attn_short_kernel(cq_ref, cz_ref, k_ref, v_ref, xc_ref, *, nseq, tm):
    scale = MEM_HEAD_DIM ** -0.5
    nr, nc = tm * HEADS, MEM_TOKENS * HEADS
    row_head = lax.broadcasted_iota(jnp.int32, (nr, nc), 0) // tm
    col_head = lax.broadcasted_iota(jnp.int32, (nr, nc), 1) % HEADS
    same_head = row_head == col_head
    for b in range(nseq):
        rs = slice(b * tm, (b + 1) * tm)
        kf = k_ref[b].reshape(nc, MEM_HEAD_DIM).astype(BF16)
        vf = v_ref[b].reshape(nc, MEM_HEAD_DIM).astype(BF16)
        q = jnp.concatenate([cq_ref[rs, h * MEM_HEAD_DIM:(h + 1) * MEM_HEAD_DIM] for h in range(HEADS)], axis=0)
        s = lax.dot_general(q.astype(BF16), kf, (((1,), (1,)), ((), ())), preferred_element_type=F32) * scale
        s = jnp.where(same_head, s, -jnp.inf)
        p = jnp.exp(s - jnp.max(s, axis=-1, keepdims=True))
        a = p * (1.0 / jnp.sum(p, axis=-1, keepdims=True))
        o = jnp.dot(a.astype(BF16), vf, preferred_element_type=F32)
        for h in range(HEADS):
            cs = slice(h * MEM_HEAD_DIM, (h + 1) * MEM_HEAD_DIM)
            xc_ref[rs, cs] = o[h * tm:(h + 1) * tm, :] * _silu(cz_ref[rs, cs])


def _mem_attn_short(proj, mem_k, mem_v, nb, seq_len, nseq):
    rows = nseq * seq_len
    t = nb * seq_len
    cqb, czb = COL_CQ // MEM_WIDTH, COL_CZ // MEM_WIDTH
    kv_spec = pl.BlockSpec((nseq, MEM_TOKENS, HEADS, MEM_HEAD_DIM), lambda i: (i, 0, 0, 0))
    return pl.pallas_call(
        functools.partial(_attn_short_kernel, nseq=nseq, tm=seq_len),
        grid=(nb // nseq,),
        in_specs=[pl.BlockSpec((rows, MEM_WIDTH), lambda i: (i, cqb)),
                  pl.BlockSpec((rows, MEM_WIDTH), lambda i: (i, czb)),
                  kv_spec, kv_spec],
        out_specs=pl.BlockSpec((rows, MEM_WIDTH), lambda i: (i, 0)),
        out_shape=jax.ShapeDtypeStruct((t, MEM_WIDTH), F32),
        compiler_params=_cparams(("parallel",)),
        name="mem_attn_short",
    )(proj, proj, mem_k, mem_v)


def _mlstm_intra(q_bf, ks_bf, v_bf, ig_row, b_row, b_col, m0_col, mask):
    dlog = jnp.where(mask, b_col - b_row + ig_row, -jnp.inf)
    inter = b_col + m0_col
    m_col = jnp.maximum(inter, jnp.max(dlog, axis=1, keepdims=True))
    dw = jnp.exp(dlog - m_col)
    inter_w = jnp.exp(inter - m_col)
    s = lax.dot_general(q_bf, ks_bf, (((1,), (1,)), ((), ())), preferred_element_type=F32) * dw
    num = jnp.dot(s.astype(BF16), v_bf, preferred_element_type=F32)
    den = jnp.sum(s, axis=1, keepdims=True)
    return num, den, m_col, inter_w


def _mlstm_prompt_kernel(q_ref, k_ref, v_ref, o_ref, z_ref, g_ref, gt_ref,
                         xb_ref, c_ref, n_ref, mcol_ref, m_scr):
    @pl.when(pl.program_id(1) == 0)
    def _():
        c_ref[...] = jnp.zeros_like(c_ref)
        n_ref[...] = jnp.zeros_like(n_ref)
        m_scr[...] = jnp.zeros_like(m_scr)

    cl = PROMPT_CHUNK
    row = lax.broadcasted_iota(jnp.int32, (cl, cl), 0)
    col = lax.broadcasted_iota(jnp.int32, (cl, cl), 1)
    mask = col <= row
    kscale = MLSTM_HEAD_DIM ** -0.5
    mcol_ref[...] = jnp.zeros_like(mcol_ref)
    for h in range(HEADS):
        cs = slice(h * MLSTM_HEAD_DIM, (h + 1) * MLSTM_HEAD_DIM)
        ig_col = g_ref[:, h:h + 1]
        b_col = g_ref[:, HEADS + h:HEADS + h + 1]
        ig_row = gt_ref[h:h + 1, :]
        b_row = gt_ref[HEADS + h:HEADS + h + 1, :]
        m0 = m_scr[h:h + 1, 0:1]
        qf = q_ref[:, cs]
        kf = k_ref[:, cs] * kscale
        vf = v_ref[:, cs]
        q_bf, ks_bf = qf.astype(BF16), kf.astype(BF16)
        num, den, m_col, inter_w = _mlstm_intra(q_bf, ks_bf, vf.astype(BF16), ig_row, b_row, b_col, m0, mask)
        c0 = c_ref[0, h]
        n0 = n_ref[0, h:h + 1, :]
        num = num + inter_w * lax.dot_general(q_bf, c0.astype(BF16), (((1,), (1,)), ((), ())),
                                              preferred_element_type=F32)
        den = den + inter_w * jnp.sum(qf * n0, axis=1, keepdims=True)
        hval = num * (1.0 / jnp.maximum(jnp.abs(den), jnp.exp(-m_col)))
        xb_ref[:, cs] = (_sigmoid(o_ref[:, cs]) * hval * _silu(z_ref[:, cs])).astype(xb_ref.dtype)
        m_new = m_col[cl - 1:cl, :]
        b_last = b_col[cl - 1:cl, :]
        w_col = jnp.exp(b_last - b_col + ig_col - m_new)
        decay = jnp.exp(b_last + m0 - m_new)
        dc = lax.dot_general((w_col * vf).astype(BF16), ks_bf, (((0,), (0,)), ((), ())),
                             preferred_element_type=F32)
        c_ref[0, h] = decay * c0 + dc
        n_ref[0, h:h + 1, :] = decay * n0 + jnp.sum(w_col * kf, axis=0, keepdims=True)
        m_scr[h:h + 1, :] = jnp.broadcast_to(m_new, (1, LANES))
        mcol_ref[:, h:h + 1] = m_col


def _mlstm_prompt(proj, g, gt, nb, seq_len):
    cl = PROMPT_CHUNK
    nc = seq_len // cl
    t = nb * seq_len
    w = MLSTM_WIDTH

    def colspec(col0):
        return pl.BlockSpec((cl, w), lambda b, c: (b * nc + c, col0 // w))

    return pl.pallas_call(
        _mlstm_prompt_kernel,
        grid=(nb, nc),
        in_specs=[colspec(COL_Q), colspec(COL_K), colspec(COL_V), colspec(COL_O), colspec(COL_Z),
                  pl.BlockSpec((cl, LANES), lambda b, c: (b * nc + c, 0)),
                  pl.BlockSpec((8, cl), lambda b, c: (0, b * nc + c))],
        out_specs=[pl.BlockSpec((cl, w), lambda b, c: (b * nc + c, 0)),
                   pl.BlockSpec((1, HEADS, MLSTM_HEAD_DIM, MLSTM_HEAD_DIM), lambda b, c: (b, 0, 0, 0)),
                   pl.BlockSpec((1, HEADS, MLSTM_HEAD_DIM), lambda b, c: (b, 0, 0)),
                   pl.BlockSpec((cl, LANES), lambda b, c: (b * nc + c, 0))],
        out_shape=[jax.ShapeDtypeStruct((t, w), BF16),
                   jax.ShapeDtypeStruct((nb, HEADS, MLSTM_HEAD_DIM, MLSTM_HEAD_DIM), F32),
                   jax.ShapeDtypeStruct((nb, HEADS, MLSTM_HEAD_DIM), F32),
                   jax.ShapeDtypeStruct((t, LANES), F32)],
        scratch_shapes=[pltpu.VMEM((8, LANES), F32)],
        compiler_params=_cparams(("arbitrary", "arbitrary")),
        name="mlstm_prompt",
    )(proj, proj, proj, proj, proj, g, gt)


def _mlstm_sample_kernel(q_ref, k_ref, v_ref, o_ref, z_ref, g_ref, gt_ref, c0_ref, n0_ref,
                         xb_ref, c_ref, n_ref, mcol_ref, num_scr, den_scr, iw_scr, *, seq_len, spb):
    bl = pl.program_id(1)
    tl = LANES
    kscale = MLSTM_HEAD_DIM ** -0.5
    shift = seq_len.bit_length() - 1

    @pl.when(bl == 0)
    def _():
        row = lax.broadcasted_iota(jnp.int32, (tl, tl), 0)
        col = lax.broadcasted_iota(jnp.int32, (tl, tl), 1)
        mask = (col <= row) & (lax.shift_right_logical(col, shift) == lax.shift_right_logical(row, shift))
        mcol_ref[...] = jnp.zeros_like(mcol_ref)
        for h in range(HEADS):
            cs = slice(h * MLSTM_HEAD_DIM, (h + 1) * MLSTM_HEAD_DIM)
            b_col = g_ref[:, HEADS + h:HEADS + h + 1]
            m0_col = g_ref[:, 2 * HEADS + h:2 * HEADS + h + 1]
            ig_row = gt_ref[h:h + 1, :]
            b_row = gt_ref[HEADS + h:HEADS + h + 1, :]
            q_bf = q_ref[:, cs].astype(BF16)
            ks_bf = (k_ref[:, cs] * kscale).astype(BF16)
            num, den, m_col, inter_w = _mlstm_intra(q_bf, ks_bf, v_ref[:, cs].astype(BF16),
                                                    ig_row, b_row, b_col, m0_col, mask)
            num_scr[:, cs] = num
            den_scr[:, h:h + 1] = den
            iw_scr[:, h:h + 1] = inter_w
            mcol_ref[:, h:h + 1] = m_col

    for s, h in [(s, h) for s in range(spb) for h in range(HEADS)]:
        r0 = pl.multiple_of((bl * spb + s) * seq_len, 8)
        rows = pl.ds(r0, seq_len)
        last = pl.ds(r0 + seq_len - 1, 1)
        cs = slice(h * MLSTM_HEAD_DIM, (h + 1) * MLSTM_HEAD_DIM)
        qf = q_ref[rows, cs]
        kf = k_ref[rows, cs] * kscale
        vf = v_ref[rows, cs]
        ig_col = g_ref[rows, h:h + 1]
        b_col = g_ref[rows, HEADS + h:HEADS + h + 1]
        b_last = g_ref[last, HEADS + h:HEADS + h + 1]
        m0 = g_ref[last, 2 * HEADS + h:2 * HEADS + h + 1]
        m_col = mcol_ref[rows, h:h + 1]
        m_new = mcol_ref[last, h:h + 1]
        inter_w = iw_scr[rows, h:h + 1]
        c0 = c0_ref[s, h]
        n0 = n0_ref[s, h:h + 1, :]
        num = num_scr[rows, cs] + inter_w * lax.dot_general(
            qf.astype(BF16), c0.astype(BF16), (((1,), (1,)), ((), ())), preferred_element_type=F32)
        den = den_scr[rows, h:h + 1] + inter_w * jnp.sum(qf * n0, axis=1, keepdims=True)
        hval = num * (1.0 / jnp.maximum(jnp.abs(den), jnp.exp(-m_col)))
        xb_ref[rows, cs] = (_sigmoid(o_ref[rows, cs]) * hval * _silu(z_ref[rows, cs])).astype(xb_ref.dtype)
        w_col = jnp.exp(b_last - b_col + ig_col - m_new)
        decay = jnp.exp(b_last + m0 - m_new)
        dc = lax.dot_general((w_col * vf).astype(BF16), kf.astype(BF16), (((0,), (0,)), ((), ())),
                             preferred_element_type=F32)
        c_ref[s, h] = decay * c0 + dc
        n_ref[s, h:h + 1, :] = decay * n0 + jnp.sum(w_col * kf, axis=0, keepdims=True)


def _mlstm_sample(proj, g, gt, c0, n0, nb, seq_len, spb):
    tl = LANES
    per = tl // seq_len // spb
    ng = nb * seq_len // tl
    assert per * spb * seq_len == tl
    t = nb * seq_len
    w = MLSTM_WIDTH
    d = MLSTM_HEAD_DIM

    def colspec(col0):
        return pl.BlockSpec((tl, w), lambda i, b: (i, col0 // w))

    state4 = pl.BlockSpec((spb, HEADS, d, d), lambda i, b: (i * per + b, 0, 0, 0))
    state3 = pl.BlockSpec((spb, HEADS, d), lambda i, b: (i * per + b, 0, 0))
    return pl.pallas_call(
        functools.partial(_mlstm_sample_kernel, seq_len=seq_len, spb=spb),
        grid=(ng, per),
        in_specs=[colspec(COL_Q), colspec(COL_K), colspec(COL_V), colspec(COL_O), colspec(COL_Z),
                  pl.BlockSpec((tl, LANES), lambda i, b: (i, 0)),
                  pl.BlockSpec((8, tl), lambda i, b: (0, i)),
                  state4, state3],
        out_specs=[pl.BlockSpec((tl, w), lambda i, b: (i, 0)), state4, state3,
                   pl.BlockSpec((tl, LANES), lambda i, b: (i, 0))],
        out_shape=[jax.ShapeDtypeStruct((t, w), F32),
                   jax.ShapeDtypeStruct((nb, HEADS, d, d), F32),
                   jax.ShapeDtypeStruct((nb, HEADS, d), F32),
                   jax.ShapeDtypeStruct((t, LANES), F32)],
        scratch_shapes=[pltpu.VMEM((tl, w), F32), pltpu.VMEM((tl, LANES), F32), pltpu.VMEM((tl, LANES), F32)],
        compiler_params=_cparams(("arbitrary", "arbitrary")),
        name="mlstm_sample",
    )(proj, proj, proj, proj, proj, g, gt, c0, n0)


def _final_kernel(xa_ref, xb_ref, xc_ref, ga_ref, gb_ref, gc_ref, x_ref, gpost_ref,
                  wa_ref, wb_ref, wc_ref, wo_ref, y_ref):
    ya = jnp.dot(xa_ref[...].astype(BF16), wa_ref[...], preferred_element_type=F32)
    merged = _sigmoid(ga_ref[...]) * ya
    yb = jnp.dot(xb_ref[...].astype(BF16), wb_ref[...], preferred_element_type=F32)
    merged = merged + _sigmoid(gb_ref[...]) * yb
    yc = jnp.dot(xc_ref[...].astype(BF16), wc_ref[...], preferred_element_type=F32)
    merged = merged + _sigmoid(gc_ref[...]) * yc
    out = jnp.dot(merged.astype(BF16), wo_ref[...], preferred_element_type=F32)
    ms = jnp.mean(out * out, axis=-1, keepdims=True)
    y_ref[...] = x_ref[...] + out * lax.rsqrt(ms + EPS) * gpost_ref[...]


def _final(xa, xb, xc, proj, x2d, gpost_row, wa, wb, wc, wo, tm):
    t, d = x2d.shape

    def rowspec(width, colblk=0):
        return pl.BlockSpec((tm, width), lambda i: (i, colblk))

    def resident(shape):
        return pl.BlockSpec(shape, lambda i: (0, 0), pipeline_mode=pl.Buffered(1))

    return pl.pallas_call(
        _final_kernel,
        grid=(t // tm,),
        in_specs=[rowspec(POOL_WIDTH), rowspec(MLSTM_WIDTH), rowspec(MEM_WIDTH),
                  rowspec(d, COL_GA // d), rowspec(d, COL_GB // d), rowspec(d, COL_GC // d),
                  rowspec(d), resident((1, d)),
                  resident(wa.shape), resident(wb.shape), resident(wc.shape), resident(wo.shape)],
        out_specs=rowspec(d),
        out_shape=jax.ShapeDtypeStruct((t, d), F32),
        compiler_params=_cparams(("parallel",)),
        name="merge_out",
    )(xa, xb, xc, proj, proj, proj, x2d, gpost_row, wa, wb, wc, wo)


def _mixer_group(x, pool_buf, c0, n0, m0, mem_k, mem_v, wts, start):
    nb, seq_len, d = x.shape
    t = nb * seq_len
    x2d = x.reshape(t, d)
    fresh = c0 is None
    if fresh:
        m0_rows = jnp.zeros((t, LANES), F32)
    else:
        m0_rows = jnp.pad(jnp.repeat(m0, seq_len, axis=0), ((0, 0), (2 * HEADS, LANES - 3 * HEADS)))
    h, g, gt = _norm_gate_prep(x2d, wts["g_pre"], wts["w_in"], wts["gate_bias"], m0_rows,
                               min(seq_len, PROMPT_CHUNK))
    proj_a = _inproj(h, wts["w_in"], 0, N_PROJ_A, tm=min(t, 1024), tn=1024)
    proj_b = _inproj(h, wts["w_in"], GATE_COL0 + N_GATE_COLS, N_PROJ_B, tm=min(t, 1024), tn=1024)

    if seq_len >= 512:
        xa, new_buf = _pool_branch(proj_a, pool_buf, wts["w_pool_grp"], wts["pool_scale"], nb, seq_len,
                                   nseq=1, tm=512, start=start)
        xc = _mem_attn(proj_b, mem_k, mem_v, nb, seq_len, nseq=1, tm=512)
    else:
        xa, new_buf = _pool_branch(proj_a, pool_buf, wts["w_pool_grp"], wts["pool_scale"], nb, seq_len,
                                   nseq=32, tm=seq_len, start=start)
        xc = _mem_attn_short(proj_b, mem_k, mem_v, nb, seq_len, nseq=8)

    if fresh:
        xb, c_new, n_new, mcol = _mlstm_prompt(proj_a, g, gt, nb, seq_len)
    else:
        xb, c_new, n_new, mcol = _mlstm_sample(proj_a, g, gt, c0, n0, nb, seq_len, spb=2)
    m_new = mcol.reshape(nb, seq_len, LANES)[:, seq_len - 1, :HEADS]

    y = _final(xa, xb, xc, proj_b, x2d, wts["g_post"], wts["w_br_pool"], wts["w_br_mlstm"],
               wts["w_br_mem"], wts["w_out"], tm=256)
    return y.reshape(nb, seq_len, d), new_buf, c_new, n_new, m_new


def kernel(x_prompt, x_sample, state_pool, state_mlstm_C, state_mlstm_n, state_mlstm_m, cache_mem_k,
           cache_mem_v, mem_prompt, g_pre, g_post, w_in, b_mlstm_i, b_mlstm_f, w_pool_grp, pool_scale,
           g_mem, w_mem_kv, w_br_pool, w_br_mlstm, w_br_mem, w_out):
    nbp = x_prompt.shape[0]
    assert w_in.shape[0] == 1, "single-layer problem: the kernels index layer 0 of the stacked weights"
    l = 0
    wts = {
        "g_pre": g_pre[l][None, :],
        "g_post": g_post[l][None, :],
        "w_in": jnp.transpose(w_in[l]),
        "gate_bias": jnp.pad(jnp.concatenate([b_mlstm_i[l], b_mlstm_f[l]]), (0, LANES - N_GATE_COLS))[None, :],
        "w_pool_grp": w_pool_grp[l].astype(BF16),
        "pool_scale": pool_scale[l][None, :],
        "w_br_pool": w_br_pool[l].astype(BF16),
        "w_br_mlstm": w_br_mlstm[l].astype(BF16),
        "w_br_mem": w_br_mem[l].astype(BF16),
        "w_out": w_out[l].astype(BF16),
    }
    mk2d, mv2d = _mem_kv(mem_prompt.reshape(nbp * MEM_TOKENS, D_MODEL), g_mem[l][None, :],
                         w_mem_kv[l].astype(BF16), tm=256)
    mk3, mv3 = mk2d.reshape(nbp, MEM_TOKENS, MEM_WIDTH), mv2d.reshape(nbp, MEM_TOKENS, MEM_WIDTH)
    yp, pool_p, c_p, n_p, m_p = _mixer_group(
        x_prompt, jnp.zeros((nbp, POOL_BUF, POOL_WIDTH), F32), None, None, None, mk3, mv3, wts, 0)
    ys, pool_s, c_s, n_s, m_s = _mixer_group(
        x_sample, state_pool[l], state_mlstm_C[l], state_mlstm_n[l], state_mlstm_m[l],
        cache_mem_k[l], cache_mem_v[l], wts, PAST_LEN)
    kv_shape = (1, nbp, MEM_TOKENS, HEADS, MEM_HEAD_DIM)
    return (yp, ys, pool_p[None], c_p[None], n_p[None], m_p[None],
            mk2d.reshape(kv_shape), mv2d.reshape(kv_shape),
            pool_s[None], c_s[None], n_s[None], m_s[None])
```

```python
import functools

import jax
import jax.numpy as jnp
from jax import lax
from jax.experimental import pallas as pl
from jax.experimental.pallas import tpu as pltpu

F32 = jnp.float32
BF16 = jnp.bfloat16

D_MODEL = 2048
PAST_LEN = 16384
POOL_WINDOWS = (2, 4, 8, 16)
POOL_GROUP_WIDTH = 256
POOL_WIDTH = 1024
POOL_BUF = 15
HEADS = 4
MLSTM_HEAD_DIM = 512
MLSTM_WIDTH = 2048
PROMPT_CHUNK = 256
MEM_TOKENS = 256
MEM_HEAD_DIM = 256
MEM_WIDTH = 1024
EPS = 1e-6
N_GATE_COLS = 2 * HEADS
GATE_COL0 = 2 * POOL_WIDTH + 5 * MLSTM_WIDTH
LANES = 128
SUBLANES = 8
COL_PV, COL_PZ = 0, 1024
COL_Q, COL_K, COL_V, COL_O, COL_Z = 2048, 4096, 6144, 8192, 10240
COL_CQ, COL_CZ = 0, 1024
COL_GA, COL_GB, COL_GC = 2048, 4096, 6144
N_PROJ_A = GATE_COL0
N_PROJ_B = 2 * MEM_WIDTH + 3 * D_MODEL
VMEM_LIMIT = 56 * 1024 * 1024


def _cparams(sem):
    return pltpu.CompilerParams(dimension_semantics=sem, vmem_limit_bytes=VMEM_LIMIT)


def _sigmoid(x):
    return 0.5 * jnp.tanh(0.5 * x) + 0.5


def _silu(x):
    return x * _sigmoid(x)


def _inproj_kernel(h_ref, w_ref, o_ref, wbf_ref):
    @pl.when(pl.program_id(1) == 0)
    def _():
        wbf_ref[...] = w_ref[...].astype(BF16)

    o_ref[...] = lax.dot_general(h_ref[...], wbf_ref[...], (((1,), (1,)), ((), ())), preferred_element_type=F32)


def _inproj(h, w_t, row0, n_cols, tm, tn):
    t, d = h.shape
    assert row0 % SUBLANES == 0 and n_cols % tn == 0 and t % tm == 0 and row0 + n_cols <= w_t.shape[0]
    return pl.pallas_call(
        _inproj_kernel,
        grid=(n_cols // tn, t // tm),
        in_specs=[pl.BlockSpec((tm, d), lambda j, i: (i, 0)),
                  pl.BlockSpec((pl.Element(tn), pl.Element(d)),
                               lambda j, i: (pl.multiple_of(row0 + j * tn, SUBLANES), 0))],
        out_specs=pl.BlockSpec((tm, tn), lambda j, i: (i, j)),
        out_shape=jax.ShapeDtypeStruct((t, n_cols), F32),
        scratch_shapes=[pltpu.VMEM((tn, d), BF16)],
        compiler_params=_cparams(("arbitrary", "arbitrary")),
        name="inproj",
    )(h, w_t)


GATE_TILE = 512


def _gate_kernel(x_ref, gpre_ref, wg_ref, bias_ref, m0_ref, h_ref, g_ref, gt_ref, *, seg, ct):
    x = x_ref[...]
    ms = jnp.mean(x * x, axis=-1, keepdims=True)
    h_ref[...] = (x * lax.rsqrt(ms + EPS) * gpre_ref[...]).astype(h_ref.dtype)
    row = lax.broadcasted_iota(jnp.int32, (ct, ct), 0)
    col = lax.broadcasted_iota(jnp.int32, (ct, ct), 1)
    same = col <= row
    if seg < ct:
        shift = seg.bit_length() - 1
        same = same & (lax.shift_right_logical(col, shift) == lax.shift_right_logical(row, shift))
    same = same.astype(F32)
    lane = lax.broadcasted_iota(jnp.int32, (ct, LANES), 1)
    wg = wg_ref[...].astype(BF16)
    for r in range(GATE_TILE // ct):
        rs = slice(r * ct, (r + 1) * ct)
        x = lax.dot_general(h_ref[rs, :], wg, (((1,), (1,)), ((), ())), preferred_element_type=F32) + bias_ref[...]
        lf = jnp.minimum(x, 0.0) - jnp.log1p(jnp.exp(-jnp.abs(x)))
        cums = jnp.dot(same, lf, precision=lax.Precision.HIGHEST, preferred_element_type=F32)
        g = jnp.where(lane < HEADS, x, jnp.where(lane < 2 * HEADS, cums, m0_ref[rs, :]))
        g_ref[rs, :] = g
        gt_ref[:, rs] = g.T[:8, :]


def _norm_gate_prep(x2d, gpre_row, w_t, bias_row, m0_rows, seg):
    t, d = x2d.shape
    tg = GATE_TILE
    ct = max(seg, LANES)
    assert GATE_COL0 % LANES == 0 and tg % ct == 0 and ct % seg == 0 and seg & (seg - 1) == 0
    return pl.pallas_call(
        functools.partial(_gate_kernel, seg=seg, ct=ct),
        grid=(t // tg,),
        in_specs=[pl.BlockSpec((tg, d), lambda i: (i, 0)),
                  pl.BlockSpec((1, d), lambda i: (0, 0)),
                  pl.BlockSpec((LANES, d), lambda i: (GATE_COL0 // LANES, 0)),
                  pl.BlockSpec((1, LANES), lambda i: (0, 0)),
                  pl.BlockSpec((tg, LANES), lambda i: (i, 0))],
        out_specs=[pl.BlockSpec((tg, d), lambda i: (i, 0)),
                   pl.BlockSpec((tg, LANES), lambda i: (i, 0)),
                   pl.BlockSpec((8, tg), lambda i: (0, i))],
        out_shape=[jax.ShapeDtypeStruct((t, d), BF16),
                   jax.ShapeDtypeStruct((t, LANES), F32),
                   jax.ShapeDtypeStruct((8, t), F32)],
        compiler_params=_cparams(("parallel",)),
        name="norm_gate_prep",
    )(x2d, gpre_row, w_t, bias_row, m0_rows)


def _pool_kernel(pv_ref, pz_ref, buf_ref, wg_ref, scale_ref, xa_ref, nbuf_ref, s_ref, p_ref,
                 *, nseq, tm, nt, start):
    j = pl.program_id(1)

    def one_seq(b, r0):
        if nt > 1:
            @pl.when(j == 0)
            def _():
                s_ref[1:16, :] = buf_ref[b]
        else:
            s_ref[1:16, :] = buf_ref[b]
        s_ref[16:16 + tm, :] = pv_ref[pl.ds(r0, tm), :]
        pos = start + j * tm + lax.broadcasted_iota(jnp.int32, (tm, 1), 0)
        for g, w in enumerate(POOL_WINDOWS):
            cs = slice(g * POOL_GROUP_WIDTH, (g + 1) * POOL_GROUP_WIDTH)
            acc = s_ref[16:16 + tm, cs]
            for i in range(1, w):
                acc = acc + s_ref[16 - i:16 - i + tm, cs]
            inv_cnt = 1.0 / jnp.minimum(w, pos + 1).astype(F32)
            p_ref[pl.ds(r0, tm), cs] = acc * inv_cnt - s_ref[16:16 + tm, cs]
        tail = s_ref[tm + 1:tm + 16, :]
        nbuf_ref[b] = tail
        if nt > 1:
            s_ref[1:16, :] = tail

    if nseq == 1:
        one_seq(0, 0)
    else:
        def body(b, carry):
            one_seq(b, pl.multiple_of(b * tm, 8))
            return carry
        lax.fori_loop(0, nseq, body, 0)

    for g in range(len(POOL_WINDOWS)):
        cs = slice(g * POOL_GROUP_WIDTH, (g + 1) * POOL_GROUP_WIDTH)
        pa = jnp.dot(p_ref[:, cs].astype(BF16), wg_ref[g], preferred_element_type=F32) * scale_ref[:, cs]
        xa_ref[:, cs] = (pa * _silu(pz_ref[:, cs])).astype(xa_ref.dtype)


def _pool_branch(proj, buf, w_grp, scale_row, nb, seq_len, nseq, tm, start):
    assert nseq == 1 or tm == seq_len
    nt = seq_len // tm
    rows = nseq * tm
    t = nb * seq_len
    pvb, pzb = COL_PV // POOL_WIDTH, COL_PZ // POOL_WIDTH
    return pl.pallas_call(
        functools.partial(_pool_kernel, nseq=nseq, tm=tm, nt=nt, start=start),
        grid=(nb // nseq, nt),
        in_specs=[pl.BlockSpec((rows, POOL_WIDTH), lambda i, j: (i * nt + j, pvb)),
                  pl.BlockSpec((rows, POOL_WIDTH), lambda i, j: (i * nt + j, pzb)),
                  pl.BlockSpec((nseq, POOL_BUF, POOL_WIDTH), lambda i, j: (i, 0, 0)),
                  pl.BlockSpec((4, POOL_GROUP_WIDTH, POOL_GROUP_WIDTH), lambda i, j: (0, 0, 0)),
                  pl.BlockSpec((1, POOL_WIDTH), lambda i, j: (0, 0))],
        out_specs=[pl.BlockSpec((rows, POOL_WIDTH), lambda i, j: (i * nt + j, 0)),
                   pl.BlockSpec((nseq, POOL_BUF, POOL_WIDTH), lambda i, j: (i, 0, 0))],
        out_shape=[jax.ShapeDtypeStruct((t, POOL_WIDTH), BF16),
                   jax.ShapeDtypeStruct((nb, POOL_BUF, POOL_WIDTH), F32)],
        scratch_shapes=[pltpu.VMEM((16 + tm, POOL_WIDTH), F32),
                        pltpu.VMEM((rows, POOL_WIDTH), F32)],
        compiler_params=_cparams(("arbitrary", "arbitrary")),
        name="pool_branch",
    )(proj, proj, buf, w_grp, scale_row)


def _memkv_kernel(m_ref, g_ref, w_ref, k_ref, v_ref):
    x = m_ref[...]
    ms = jnp.mean(x * x, axis=-1, keepdims=True)
    h = (x * lax.rsqrt(ms + EPS) * g_ref[...]).astype(BF16)
    kv = jnp.dot(h, w_ref[...], preferred_element_type=F32)
    k_ref[...] = kv[:, :MEM_WIDTH]
    v_ref[...] = kv[:, MEM_WIDTH:]


def _mem_kv(mem2d, g_row, w_bf, tm):
    t, d = mem2d.shape
    return pl.pallas_call(
        _memkv_kernel,
        grid=(t // tm,),
        in_specs=[pl.BlockSpec((tm, d), lambda i: (i, 0)),
                  pl.BlockSpec((1, d), lambda i: (0, 0)),
                  pl.BlockSpec((d, 2 * MEM_WIDTH), lambda i: (0, 0))],
        out_specs=[pl.BlockSpec((tm, MEM_WIDTH), lambda i: (i, 0)),
                   pl.BlockSpec((tm, MEM_WIDTH), lambda i: (i, 0))],
        out_shape=[jax.ShapeDtypeStruct((t, MEM_WIDTH), F32),
                   jax.ShapeDtypeStruct((t, MEM_WIDTH), F32)],
        compiler_params=_cparams(("parallel",)),
        name="mem_kv",
    )(mem2d, g_row, w_bf)


def _attn_kernel(cq_ref, cz_ref, k_ref, v_ref, xc_ref, *, nseq, tm):
    scale = MEM_HEAD_DIM ** -0.5
    for b in range(nseq):
        rs = slice(b * tm, (b + 1) * tm)
        for h in range(HEADS):
            cs = slice(h * MEM_HEAD_DIM, (h + 1) * MEM_HEAD_DIM)
            q = cq_ref[rs, cs].astype(BF16)
            k = k_ref[b, :, cs].astype(BF16)
            v = v_ref[b, :, cs].astype(BF16)
            s = lax.dot_general(q, k, (((1,), (1,)), ((), ())), preferred_element_type=F32) * scale
            p = jnp.exp(s - jnp.max(s, axis=-1, keepdims=True))
            a = p * (1.0 / jnp.sum(p, axis=-1, keepdims=True))
            o = jnp.dot(a.astype(BF16), v, preferred_element_type=F32)
            xc_ref[rs, cs] = (o * _silu(cz_ref[rs, cs])).astype(xc_ref.dtype)


def _mem_attn(proj, mem_k, mem_v, nb, seq_len, nseq, tm):
    assert nseq == 1 or tm == seq_len
    nt = seq_len // tm
    rows = nseq * tm
    t = nb * seq_len
    cqb, czb = COL_CQ // MEM_WIDTH, COL_CZ // MEM_WIDTH
    return pl.pallas_call(
        functools.partial(_attn_kernel, nseq=nseq, tm=tm),
        grid=(nb // nseq, nt),
        in_specs=[pl.BlockSpec((rows, MEM_WIDTH), lambda i, j: (i * nt + j, cqb)),
                  pl.BlockSpec((rows, MEM_WIDTH), lambda i, j: (i * nt + j, czb)),
                  pl.BlockSpec((nseq, MEM_TOKENS, MEM_WIDTH), lambda i, j: (i, 0, 0)),
                  pl.BlockSpec((nseq, MEM_TOKENS, MEM_WIDTH), lambda i, j: (i, 0, 0))],
        out_specs=pl.BlockSpec((rows, MEM_WIDTH), lambda i, j: (i * nt + j, 0)),
        out_shape=jax.ShapeDtypeStruct((t, MEM_WIDTH), BF16),
        compiler_params=_cparams(("parallel", "arbitrary")),
        name="mem_attn",
    )(proj, proj, mem_k, mem_v)


def _attn_short_kernel(cq_ref, cz_ref, k_ref, v_ref, xc_ref, *, nseq, tm):
    scale = MEM_HEAD_DIM ** -0.5
    nr, nc = tm * HEADS, MEM_TOKENS * HEADS
    row_head = lax.broadcasted_iota(jnp.int32, (nr, nc), 0) // tm
    col_head = lax.broadcasted_iota(jnp.int32, (nr, nc), 1) % HEADS
    same_head = row_head == col_head
    for b in range(nseq):
        rs = slice(b * tm, (b + 1) * tm)
        kf = k_ref[b].reshape(nc, MEM_HEAD_DIM).astype(BF16)
        vf = v_ref[b].reshape(nc, MEM_HEAD_DIM).astype(BF16)
        q = jnp.concatenate([cq_ref[rs, h * MEM_HEAD_DIM:(h + 1) * MEM_HEAD_DIM] for h in range(HEADS)], axis=0)
        s = lax.dot_general(q.astype(BF16), kf, (((1,), (1,)), ((), ())), preferred_element_type=F32) * scale
        s = jnp.where(same_head, s, -jnp.inf)
        p = jnp.exp(s - jnp.max(s, axis=-1, keepdims=True))
        a = p * (1.0 / jnp.sum(p, axis=-1, keepdims=True))
        o = jnp.dot(a.astype(BF16), vf, preferred_element_type=F32)
        for h in range(HEADS):
            cs = slice(h * MEM_HEAD_DIM, (h + 1) * MEM_HEAD_DIM)
            xc_ref[rs, cs] = o[h * tm:(h + 1) * tm, :] * _silu(cz_ref[rs, cs])


def _mem_attn_short(proj, mem_k, mem_v, nb, seq_len, nseq):
    rows = nseq * seq_len
    t = nb * seq_len
    cqb, czb = COL_CQ // MEM_WIDTH, COL_CZ // MEM_WIDTH
    kv_spec = pl.BlockSpec((nseq, MEM_TOKENS, HEADS, MEM_HEAD_DIM), lambda i: (i, 0, 0, 0))
    return pl.pallas_call(
        functools.partial(_attn_short_kernel, nseq=nseq, tm=seq_len),
        grid=(nb // nseq,),
        in_specs=[pl.BlockSpec((rows, MEM_WIDTH), lambda i: (i, cqb)),
                  pl.BlockSpec((rows, MEM_WIDTH), lambda i: (i, czb)),
                  kv_spec, kv_spec],
        out_specs=pl.BlockSpec((rows, MEM_WIDTH), lambda i: (i, 0)),
        out_shape=jax.ShapeDtypeStruct((t, MEM_WIDTH), F32),
        compiler_params=_cparams(("parallel",)),
        name="mem_attn_short",
    )(proj, proj, mem_k, mem_v)


def _mlstm_intra(q_bf, ks_bf, v_bf, ig_row, b_row, b_col, m0_col, mask):
    dlog = jnp.where(mask, b_col - b_row + ig_row, -jnp.inf)
    inter = b_col + m0_col
    m_col = jnp.maximum(inter, jnp.max(dlog, axis=1, keepdims=True))
    dw = jnp.exp(dlog - m_col)
    inter_w = jnp.exp(inter - m_col)
    s = lax.dot_general(q_bf, ks_bf, (((1,), (1,)), ((), ())), preferred_element_type=F32) * dw
    num = jnp.dot(s.astype(BF16), v_bf, preferred_element_type=F32)
    den = jnp.sum(s, axis=1, keepdims=True)
    return num, den, m_col, inter_w


def _mlstm_prompt_kernel(q_ref, k_ref, v_ref, o_ref, z_ref, g_ref, gt_ref,
                         xb_ref, c_ref, n_ref, mcol_ref, m_scr):
    @pl.when(pl.program_id(1) == 0)
    def _():
        c_ref[...] = jnp.zeros_like(c_ref)
        n_ref[...] = jnp.zeros_like(n_ref)
        m_scr[...] = jnp.zeros_like(m_scr)

    cl = PROMPT_CHUNK
    row = lax.broadcasted_iota(jnp.int32, (cl, cl), 0)
    col = lax.broadcasted_iota(jnp.int32, (cl, cl), 1)
    mask = col <= row
    kscale = MLSTM_HEAD_DIM ** -0.5
    mcol_ref[...] = jnp.zeros_like(mcol_ref)
    for h in range(HEADS):
        cs = slice(h * MLSTM_HEAD_DIM, (h + 1) * MLSTM_HEAD_DIM)
        ig_col = g_ref[:, h:h + 1]
        b_col = g_ref[:, HEADS + h:HEADS + h + 1]
        ig_row = gt_ref[h:h + 1, :]
        b_row = gt_ref[HEADS + h:HEADS + h + 1, :]
        m0 = m_scr[h:h + 1, 0:1]
        qf = q_ref[:, cs]
        kf = k_ref[:, cs] * kscale
        vf = v_ref[:, cs]
        q_bf, ks_bf = qf.astype(BF16), kf.astype(BF16)
        num, den, m_col, inter_w = _mlstm_intra(q_bf, ks_bf, vf.astype(BF16), ig_row, b_row, b_col, m0, mask)
        c0 = c_ref[0, h]
        n0 = n_ref[0, h:h + 1, :]
        num = num + inter_w * lax.dot_general(q_bf, c0.astype(BF16), (((1,), (1,)), ((), ())),
                                              preferred_element_type=F32)
        den = den + inter_w * jnp.sum(qf * n0, axis=1, keepdims=True)
        hval = num * (1.0 / jnp.maximum(jnp.abs(den), jnp.exp(-m_col)))
        xb_ref[:, cs] = (_sigmoid(o_ref[:, cs]) * hval * _silu(z_ref[:, cs])).astype(xb_ref.dtype)
        m_new = m_col[cl - 1:cl, :]
        b_last = b_col[cl - 1:cl, :]
        w_col = jnp.exp(b_last - b_col + ig_col - m_new)
        decay = jnp.exp(b_last + m0 - m_new)
        dc = lax.dot_general((w_col * vf).astype(BF16), ks_bf, (((0,), (0,)), ((), ())),
                             preferred_element_type=F32)
        c_ref[0, h] = decay * c0 + dc
        n_ref[0, h:h + 1, :] = decay * n0 + jnp.sum(w_col * kf, axis=0, keepdims=True)
        m_scr[h:h + 1, :] = jnp.broadcast_to(m_new, (1, LANES))
        mcol_ref[:, h:h + 1] = m_col


def _mlstm_prompt(proj, g, gt, nb, seq_len):
    cl = PROMPT_CHUNK
    nc = seq_len // cl
    t = nb * seq_len
    w = MLSTM_WIDTH

    def colspec(col0):
        return pl.BlockSpec((cl, w), lambda b, c: (b * nc + c, col0 // w))

    return pl.pallas_call(
        _mlstm_prompt_kernel,
        grid=(nb, nc),
        in_specs=[colspec(COL_Q), colspec(COL_K), colspec(COL_V), colspec(COL_O), colspec(COL_Z),
                  pl.BlockSpec((cl, LANES), lambda b, c: (b * nc + c, 0)),
                  pl.BlockSpec((8, cl), lambda b, c: (0, b * nc + c))],
        out_specs=[pl.BlockSpec((cl, w), lambda b, c: (b * nc + c, 0)),
                   pl.BlockSpec((1, HEADS, MLSTM_HEAD_DIM, MLSTM_HEAD_DIM), lambda b, c: (b, 0, 0, 0)),
                   pl.BlockSpec((1, HEADS, MLSTM_HEAD_DIM), lambda b, c: (b, 0, 0)),
                   pl.BlockSpec((cl, LANES), lambda b, c: (b * nc + c, 0))],
        out_shape=[jax.ShapeDtypeStruct((t, w), BF16),
                   jax.ShapeDtypeStruct((nb, HEADS, MLSTM_HEAD_DIM, MLSTM_HEAD_DIM), F32),
                   jax.ShapeDtypeStruct((nb, HEADS, MLSTM_HEAD_DIM), F32),
                   jax.ShapeDtypeStruct((t, LANES), F32)],
        scratch_shapes=[pltpu.VMEM((8, LANES), F32)],
        compiler_params=_cparams(("arbitrary", "arbitrary")),
        name="mlstm_prompt",
    )(proj, proj, proj, proj, proj, g, gt)


STAT_DEN, STAT_IW, STAT_M = 0, HEADS, 2 * HEADS


def _mlstm_short_intra_kernel(q_ref, k_ref, v_ref, g_ref, gt_ref, num_ref, stat_ref, *, seq_len):
    tl = LANES
    kscale = MLSTM_HEAD_DIM ** -0.5
    shift = seq_len.bit_length() - 1
    row = lax.broadcasted_iota(jnp.int32, (tl, tl), 0)
    col = lax.broadcasted_iota(jnp.int32, (tl, tl), 1)
    mask = (col <= row) & (lax.shift_right_logical(col, shift) == lax.shift_right_logical(row, shift))
    stat_ref[...] = jnp.zeros_like(stat_ref)
    for h in range(HEADS):
        cs = slice(h * MLSTM_HEAD_DIM, (h + 1) * MLSTM_HEAD_DIM)
        b_col = g_ref[:, HEADS + h:HEADS + h + 1]
        m0_col = g_ref[:, 2 * HEADS + h:2 * HEADS + h + 1]
        ig_row = gt_ref[h:h + 1, :]
        b_row = gt_ref[HEADS + h:HEADS + h + 1, :]
        q_bf = q_ref[:, cs].astype(BF16)
        ks_bf = (k_ref[:, cs] * kscale).astype(BF16)
        num, den, m_col, inter_w = _mlstm_intra(q_bf, ks_bf, v_ref[:, cs].astype(BF16),
                                                ig_row, b_row, b_col, m0_col, mask)
        num_ref[:, cs] = num
        stat_ref[:, STAT_DEN + h:STAT_DEN + h + 1] = den
        stat_ref[:, STAT_IW + h:STAT_IW + h + 1] = inter_w
        stat_ref[:, STAT_M + h:STAT_M + h + 1] = m_col


def _mlstm_short_intra(proj, g, gt, seq_len):
    t = proj.shape[0]
    tl = LANES
    w = MLSTM_WIDTH
    assert tl % seq_len == 0 and t % tl == 0

    def colspec(col0):
        return pl.BlockSpec((tl, w), lambda i: (i, col0 // w))

    return pl.pallas_call(
        functools.partial(_mlstm_short_intra_kernel, seq_len=seq_len),
        grid=(t // tl,),
        in_specs=[colspec(COL_Q), colspec(COL_K), colspec(COL_V),
                  pl.BlockSpec((tl, LANES), lambda i: (i, 0)),
                  pl.BlockSpec((8, tl), lambda i: (0, i))],
        out_specs=[pl.BlockSpec((tl, w), lambda i: (i, 0)),
                   pl.BlockSpec((tl, LANES), lambda i: (i, 0))],
        out_shape=[jax.ShapeDtypeStruct((t, w), F32),
                   jax.ShapeDtypeStruct((t, LANES), F32)],
        compiler_params=_cparams(("parallel",)),
        name="mlstm_short_intra",
    )(proj, proj, proj, g, gt)


def _mlstm_state_step(q_ref, k_ref, v_ref, o_ref, z_ref, num_ref, g_ref, stat_ref, c0_ref, n0_ref,
                      xb_ref, c_ref, n_ref):
    kscale = MLSTM_HEAD_DIM ** -0.5
    seq_len = q_ref.shape[0]
    last = slice(seq_len - 1, seq_len)
    for h in range(HEADS):
        cs = slice(h * MLSTM_HEAD_DIM, (h + 1) * MLSTM_HEAD_DIM)
        qf = q_ref[:, cs]
        kf = k_ref[:, cs] * kscale
        vf = v_ref[:, cs]
        ig_col = g_ref[:, h:h + 1]
        b_col = g_ref[:, HEADS + h:HEADS + h + 1]
        b_last = g_ref[last, HEADS + h:HEADS + h + 1]
        m0 = g_ref[last, 2 * HEADS + h:2 * HEADS + h + 1]
        den_intra = stat_ref[:, STAT_DEN + h:STAT_DEN + h + 1]
        inter_w = stat_ref[:, STAT_IW + h:STAT_IW + h + 1]
        m_col = stat_ref[:, STAT_M + h:STAT_M + h + 1]
        m_new = stat_ref[last, STAT_M + h:STAT_M + h + 1]
        c0 = c0_ref[0, h]
        n0 = n0_ref[0, h:h + 1, :]
        num = num_ref[:, cs] + inter_w * lax.dot_general(
            qf.astype(BF16), c0.astype(BF16), (((1,), (1,)), ((), ())), preferred_element_type=F32)
        den = den_intra + inter_w * jnp.sum(qf * n0, axis=1, keepdims=True)
        hval = num * (1.0 / jnp.maximum(jnp.abs(den), jnp.exp(-m_col)))
        xb_ref[:, cs] = (_sigmoid(o_ref[:, cs]) * hval * _silu(z_ref[:, cs])).astype(xb_ref.dtype)
        w_col = jnp.exp(b_last - b_col + ig_col - m_new)
        decay = jnp.exp(b_last + m0 - m_new)
        dc = lax.dot_general((w_col * vf).astype(BF16), kf.astype(BF16), (((0,), (0,)), ((), ())),
                             preferred_element_type=F32)
        c_ref[0, h] = decay * c0 + dc
        n_ref[0, h:h + 1, :] = decay * n0 + jnp.sum(w_col * kf, axis=0, keepdims=True)


def _inproj_state_kernel(h_ref, w_ref, *refs):
    state_refs, (o_ref,), out_state_refs, (wbf_ref,) = refs[:10], refs[10:11], refs[11:14], refs[14:]

    @pl.when(pl.program_id(1) == 0)
    def _():
        wbf_ref[...] = w_ref[...].astype(BF16)

    o_ref[...] = lax.dot_general(h_ref[...], wbf_ref[...], (((1,), (1,)), ((), ())), preferred_element_type=F32)
    _mlstm_state_step(*state_refs, *out_state_refs)


def _inproj_with_state(h, w_t, row0, n_cols, tm, tn, proj_s, num_s, g_s, stat_s, c0, n0, seq_len):
    t, d = h.shape
    n_seq = c0.shape[0]
    n_inner = t // tm
    assert row0 % SUBLANES == 0 and n_cols % tn == 0 and t % tm == 0 and (n_cols // tn) * n_inner == n_seq
    w = MLSTM_WIDTH
    hd = MLSTM_HEAD_DIM

    def seq(j, i):
        return j * n_inner + i

    def colspec(col0):
        return pl.BlockSpec((seq_len, w), lambda j, i: (seq(j, i), col0 // w))

    rowspec = pl.BlockSpec((seq_len, w), lambda j, i: (seq(j, i), 0))
    lanespec = pl.BlockSpec((seq_len, LANES), lambda j, i: (seq(j, i), 0))
    state4 = pl.BlockSpec((1, HEADS, hd, hd), lambda j, i: (seq(j, i), 0, 0, 0))
    state3 = pl.BlockSpec((1, HEADS, hd), lambda j, i: (seq(j, i), 0, 0))
    return pl.pallas_call(
        _inproj_state_kernel,
        grid=(n_cols // tn, n_inner),
        in_specs=[pl.BlockSpec((tm, d), lambda j, i: (i, 0)),
                  pl.BlockSpec((pl.Element(tn), pl.Element(d)),
                               lambda j, i: (pl.multiple_of(row0 + j * tn, SUBLANES), 0)),
                  colspec(COL_Q), colspec(COL_K), colspec(COL_V), colspec(COL_O), colspec(COL_Z),
                  rowspec, lanespec, lanespec, state4, state3],
        out_specs=[pl.BlockSpec((tm, tn), lambda j, i: (i, j)), rowspec, state4, state3],
        out_shape=[jax.ShapeDtypeStruct((t, n_cols), F32),
                   jax.ShapeDtypeStruct((n_seq * seq_len, w), F32),
                   jax.ShapeDtypeStruct((n_seq, HEADS, hd, hd), F32),
                   jax.ShapeDtypeStruct((n_seq, HEADS, hd), F32)],
        scratch_shapes=[pltpu.VMEM((tn, d), BF16)],
        compiler_params=_cparams(("arbitrary", "arbitrary")),
        name="inproj_state",
    )(h, w_t, proj_s, proj_s, proj_s, proj_s, proj_s, num_s, g_s, stat_s, c0, n0)


def _final_kernel(xa_ref, xb_ref, xc_ref, ga_ref, gb_ref, gc_ref, x_ref, gpost_ref,
                  wa_ref, wb_ref, wc_ref, wo_ref, y_ref):
    ya = jnp.dot(xa_ref[...].astype(BF16), wa_ref[...], preferred_element_type=F32)
    merged = _sigmoid(ga_ref[...]) * ya
    yb = jnp.dot(xb_ref[...].astype(BF16), wb_ref[...], preferred_element_type=F32)
    merged = merged + _sigmoid(gb_ref[...]) * yb
    yc = jnp.dot(xc_ref[...].astype(BF16), wc_ref[...], preferred_element_type=F32)
    merged = merged + _sigmoid(gc_ref[...]) * yc
    out = jnp.dot(merged.astype(BF16), wo_ref[...], preferred_element_type=F32)
    ms = jnp.mean(out * out, axis=-1, keepdims=True)
    y_ref[...] = x_ref[...] + out * lax.rsqrt(ms + EPS) * gpost_ref[...]


def _final(xa, xb, xc, proj, x2d, gpost_row, wa, wb, wc, wo, tm):
    t, d = x2d.shape

    def rowspec(width, colblk=0):
        return pl.BlockSpec((tm, width), lambda i: (i, colblk))

    def resident(shape):
        return pl.BlockSpec(shape, lambda i: (0, 0), pipeline_mode=pl.Buffered(1))

    return pl.pallas_call(
        _final_kernel,
        grid=(t // tm,),
        in_specs=[rowspec(POOL_WIDTH), rowspec(MLSTM_WIDTH), rowspec(MEM_WIDTH),
                  rowspec(d, COL_GA // d), rowspec(d, COL_GB // d), rowspec(d, COL_GC // d),
                  rowspec(d), resident((1, d)),
                  resident(wa.shape), resident(wb.shape), resident(wc.shape), resident(wo.shape)],
        out_specs=rowspec(d),
        out_shape=jax.ShapeDtypeStruct((t, d), F32),
        compiler_params=_cparams(("parallel",)),
        name="merge_out",
    )(xa, xb, xc, proj, proj, proj, x2d, gpost_row, wa, wb, wc, wo)


B_ROW0 = GATE_COL0 + N_GATE_COLS


def _branches_and_merge(x2d, nb, seq_len, proj_a, proj_b, xb, pool_buf, mem_k, mem_v, wts, start):
    if seq_len >= 512:
        xa, new_buf = _pool_branch(proj_a, pool_buf, wts["w_pool_grp"], wts["pool_scale"], nb, seq_len,
                                   nseq=1, tm=512, start=start)
        xc = _mem_attn(proj_b, mem_k, mem_v, nb, seq_len, nseq=1, tm=512)
    else:
        xa, new_buf = _pool_branch(proj_a, pool_buf, wts["w_pool_grp"], wts["pool_scale"], nb, seq_len,
                                   nseq=32, tm=seq_len, start=start)
        xc = _mem_attn_short(proj_b, mem_k, mem_v, nb, seq_len, nseq=8)
    y = _final(xa, xb, xc, proj_b, x2d, wts["g_post"], wts["w_br_pool"], wts["w_br_mlstm"],
               wts["w_br_mem"], wts["w_out"], tm=256)
    return y.reshape(nb, seq_len, x2d.shape[1]), new_buf


def kernel(x_prompt, x_sample, state_pool, state_mlstm_C, state_mlstm_n, state_mlstm_m, cache_mem_k,
           cache_mem_v, mem_prompt, g_pre, g_post, w_in, b_mlstm_i, b_mlstm_f, w_pool_grp, pool_scale,
           g_mem, w_mem_kv, w_br_pool, w_br_mlstm, w_br_mem, w_out):
    nbp = x_prompt.shape[0]
    assert w_in.shape[0] == 1, "single-layer problem: the kernels index layer 0 of the stacked weights"
    l = 0
    wts = {
        "g_pre": g_pre[l][None, :],
        "g_post": g_post[l][None, :],
        "w_in": jnp.transpose(w_in[l]),
        "gate_bias": jnp.pad(jnp.concatenate([b_mlstm_i[l], b_mlstm_f[l]]), (0, LANES - N_GATE_COLS))[None, :],
        "w_pool_grp": w_pool_grp[l].astype(BF16),
        "pool_scale": pool_scale[l][None, :],
        "w_br_pool": w_br_pool[l].astype(BF16),
        "w_br_mlstm": w_br_mlstm[l].astype(BF16),
        "w_br_mem": w_br_mem[l].astype(BF16),
        "w_out": w_out[l].astype(BF16),
    }
    mk2d, mv2d = _mem_kv(mem_prompt.reshape(nbp * MEM_TOKENS, D_MODEL), g_mem[l][None, :],
                         w_mem_kv[l].astype(BF16), tm=256)
    mk3, mv3 = mk2d.reshape(nbp, MEM_TOKENS, MEM_WIDTH), mv2d.reshape(nbp, MEM_TOKENS, MEM_WIDTH)
    w_t = wts["w_in"]

    nbs, ls, d = x_sample.shape
    ts = nbs * ls
    xs2d = x_sample.reshape(ts, d)
    m0_rows = jnp.pad(jnp.repeat(state_mlstm_m[l], ls, axis=0), ((0, 0), (2 * HEADS, LANES - 3 * HEADS)))
    hs, gs, gts = _norm_gate_prep(xs2d, wts["g_pre"], w_t, wts["gate_bias"], m0_rows, ls)
    proj_a_s = _inproj(hs, w_t, 0, N_PROJ_A, tm=ts, tn=1024)
    proj_b_s = _inproj(hs, w_t, B_ROW0, N_PROJ_B, tm=ts, tn=1024)
    num_s, stat_s = _mlstm_short_intra(proj_a_s, gs, gts, ls)
    m_s = stat_s.reshape(nbs, ls, LANES)[:, ls - 1, STAT_M:STAT_M + HEADS]

    _, lp, _ = x_prompt.shape
    tp = nbp * lp
    xp2d = x_prompt.reshape(tp, d)
    hp, gp, gtp = _norm_gate_prep(xp2d, wts["g_pre"], w_t, wts["gate_bias"], jnp.zeros((tp, LANES), F32),
                                  PROMPT_CHUNK)
    proj_a_p, xb_s, c_s, n_s = _inproj_with_state(hp, w_t, 0, N_PROJ_A, 1024, 768, proj_a_s, num_s, gs, stat_s,
                                                  state_mlstm_C[l], state_mlstm_n[l], ls)
    proj_b_p = _inproj(hp, w_t, B_ROW0, N_PROJ_B, tm=1024, tn=1024)
    xb_p, c_p, n_p, mcol_p = _mlstm_prompt(proj_a_p, gp, gtp, nbp, lp)
    m_p = mcol_p.reshape(nbp, lp, LANES)[:, lp - 1, :HEADS]

    yp, pool_p = _branches_and_merge(xp2d, nbp, lp, proj_a_p, proj_b_p, xb_p,
                                     jnp.zeros((nbp, POOL_BUF, POOL_WIDTH), F32), mk3, mv3, wts, 0)
    ys, pool_s = _branches_and_merge(xs2d, nbs, ls, proj_a_s, proj_b_s, xb_s, state_pool[l],
                                     cache_mem_k[l], cache_mem_v[l], wts, PAST_LEN)
    kv_shape = (1, nbp, MEM_TOKENS, HEADS, MEM_HEAD_DIM)
    return (yp, ys, pool_p[None], c_p[None], n_p[None], m_p[None],
            mk2d.reshape(kv_shape), mv2d.reshape(kv_shape),
            pool_s[None], c_s[None], n_s[None], m_s[None])
```

```python
import functools

import jax
import jax.numpy as jnp
from jax import lax
from jax.experimental import pallas as pl
from jax.experimental.pallas import tpu as pltpu

F32 = jnp.float32
BF16 = jnp.bfloat16

D_MODEL = 2048
PAST_LEN = 16384
POOL_WINDOWS = (2, 4, 8, 16)
POOL_GROUP_WIDTH = 256
POOL_WIDTH = 1024
POOL_BUF = 15
HEADS = 4
MLSTM_HEAD_DIM = 512
MLSTM_WIDTH = 2048
PROMPT_CHUNK = 256
MEM_TOKENS = 256
MEM_HEAD_DIM = 256
MEM_WIDTH = 1024
EPS = 1e-6
N_GATE_COLS = 2 * HEADS
GATE_COL0 = 2 * POOL_WIDTH + 5 * MLSTM_WIDTH
LANES = 128
SUBLANES = 8
COL_PV, COL_PZ = 0, 1024
COL_Q, COL_K, COL_V, COL_O, COL_Z = 2048, 4096, 6144, 8192, 10240
COL_CQ, COL_CZ = 0, 1024
COL_GA, COL_GB, COL_GC = 2048, 4096, 6144
N_PROJ_A = GATE_COL0
N_PROJ_B = 2 * MEM_WIDTH + 3 * D_MODEL
VMEM_LIMIT = 56 * 1024 * 1024


def _cparams(sem):
    return pltpu.CompilerParams(dimension_semantics=sem, vmem_limit_bytes=VMEM_LIMIT)


def _sigmoid(x):
    return 0.5 * jnp.tanh(0.5 * x) + 0.5


def _silu(x):
    return x * _sigmoid(x)


def _inproj_kernel(h_ref, w_ref, o_ref, wbf_ref):
    @pl.when(pl.program_id(1) == 0)
    def _():
        wbf_ref[...] = w_ref[...].astype(BF16)

    o_ref[...] = lax.dot_general(h_ref[...], wbf_ref[...], (((1,), (1,)), ((), ())), preferred_element_type=F32)


def _inproj(h, w_t, row0, n_cols, tm, tn):
    t, d = h.shape
    assert row0 % SUBLANES == 0 and n_cols % tn == 0 and t % tm == 0 and row0 + n_cols <= w_t.shape[0]
    return pl.pallas_call(
        _inproj_kernel,
        grid=(n_cols // tn, t // tm),
        in_specs=[pl.BlockSpec((tm, d), lambda j, i: (i, 0)),
                  pl.BlockSpec((pl.Element(tn), pl.Element(d)),
                               lambda j, i: (pl.multiple_of(row0 + j * tn, SUBLANES), 0))],
        out_specs=[pl.BlockSpec((tm, tn), lambda j, i: (i, j)),
                   pl.BlockSpec((tn, d), lambda j, i: (j, 0))],
        out_shape=[jax.ShapeDtypeStruct((t, n_cols), F32),
                   jax.ShapeDtypeStruct((n_cols, d), BF16)],
        compiler_params=_cparams(("arbitrary", "arbitrary")),
        name="inproj",
    )(h, w_t)


GATE_TILE = 512


def _gate_kernel(x_ref, gpre_ref, wg_ref, bias_ref, m0_ref, h_ref, g_ref, gt_ref, *, seg, ct):
    x = x_ref[...]
    ms = jnp.mean(x * x, axis=-1, keepdims=True)
    h_ref[...] = (x * lax.rsqrt(ms + EPS) * gpre_ref[...]).astype(h_ref.dtype)
    row = lax.broadcasted_iota(jnp.int32, (ct, ct), 0)
    col = lax.broadcasted_iota(jnp.int32, (ct, ct), 1)
    same = col <= row
    if seg < ct:
        shift = seg.bit_length() - 1
        same = same & (lax.shift_right_logical(col, shift) == lax.shift_right_logical(row, shift))
    same = same.astype(F32)
    lane = lax.broadcasted_iota(jnp.int32, (ct, LANES), 1)
    wg = wg_ref[...].astype(BF16)
    for r in range(GATE_TILE // ct):
        rs = slice(r * ct, (r + 1) * ct)
        x = lax.dot_general(h_ref[rs, :], wg, (((1,), (1,)), ((), ())), preferred_element_type=F32) + bias_ref[...]
        lf = jnp.minimum(x, 0.0) - jnp.log1p(jnp.exp(-jnp.abs(x)))
        cums = jnp.dot(same, lf, precision=lax.Precision.HIGHEST, preferred_element_type=F32)
        g = jnp.where(lane < HEADS, x, jnp.where(lane < 2 * HEADS, cums, m0_ref[rs, :]))
        g_ref[rs, :] = g
        gt_ref[:, rs] = g.T[:8, :]


def _norm_gate_prep(x2d, gpre_row, w_t, bias_row, m0_rows, seg):
    t, d = x2d.shape
    tg = GATE_TILE
    ct = max(seg, LANES)
    assert GATE_COL0 % LANES == 0 and tg % ct == 0 and ct % seg == 0 and seg & (seg - 1) == 0
    return pl.pallas_call(
        functools.partial(_gate_kernel, seg=seg, ct=ct),
        grid=(t // tg,),
        in_specs=[pl.BlockSpec((tg, d), lambda i: (i, 0)),
                  pl.BlockSpec((1, d), lambda i: (0, 0)),
                  pl.BlockSpec((LANES, d), lambda i: (GATE_COL0 // LANES, 0)),
                  pl.BlockSpec((1, LANES), lambda i: (0, 0)),
                  pl.BlockSpec((tg, LANES), lambda i: (i, 0))],
        out_specs=[pl.BlockSpec((tg, d), lambda i: (i, 0)),
                   pl.BlockSpec((tg, LANES), lambda i: (i, 0)),
                   pl.BlockSpec((8, tg), lambda i: (0, i))],
        out_shape=[jax.ShapeDtypeStruct((t, d), BF16),
                   jax.ShapeDtypeStruct((t, LANES), F32),
                   jax.ShapeDtypeStruct((8, t), F32)],
        compiler_params=_cparams(("parallel",)),
        name="norm_gate_prep",
    )(x2d, gpre_row, w_t, bias_row, m0_rows)


def _pool_kernel(pv_ref, pz_ref, buf_ref, wg_ref, scale_ref, xa_ref, nbuf_ref, s_ref, p_ref,
                 *, nseq, tm, nt, start):
    j = pl.program_id(1)

    def one_seq(b, r0):
        if nt > 1:
            @pl.when(j == 0)
            def _():
                s_ref[1:16, :] = buf_ref[b]
        else:
            s_ref[1:16, :] = buf_ref[b]
        s_ref[16:16 + tm, :] = pv_ref[pl.ds(r0, tm), :]
        pos = start + j * tm + lax.broadcasted_iota(jnp.int32, (tm, 1), 0)
        for g, w in enumerate(POOL_WINDOWS):
            cs = slice(g * POOL_GROUP_WIDTH, (g + 1) * POOL_GROUP_WIDTH)
            acc = s_ref[16:16 + tm, cs]
            for i in range(1, w):
                acc = acc + s_ref[16 - i:16 - i + tm, cs]
            inv_cnt = 1.0 / jnp.minimum(w, pos + 1).astype(F32)
            p_ref[pl.ds(r0, tm), cs] = acc * inv_cnt - s_ref[16:16 + tm, cs]
        tail = s_ref[tm + 1:tm + 16, :]
        nbuf_ref[b] = tail
        if nt > 1:
            s_ref[1:16, :] = tail

    if nseq == 1:
        one_seq(0, 0)
    else:
        def body(b, carry):
            one_seq(b, pl.multiple_of(b * tm, 8))
            return carry
        lax.fori_loop(0, nseq, body, 0)

    for g in range(len(POOL_WINDOWS)):
        cs = slice(g * POOL_GROUP_WIDTH, (g + 1) * POOL_GROUP_WIDTH)
        pa = jnp.dot(p_ref[:, cs].astype(BF16), wg_ref[g], preferred_element_type=F32) * scale_ref[:, cs]
        xa_ref[:, cs] = (pa * _silu(pz_ref[:, cs])).astype(xa_ref.dtype)


def _pool_branch(proj, buf, w_grp, scale_row, nb, seq_len, nseq, tm, start):
    assert nseq == 1 or tm == seq_len
    nt = seq_len // tm
    rows = nseq * tm
    t = nb * seq_len
    pvb, pzb = COL_PV // POOL_WIDTH, COL_PZ // POOL_WIDTH
    return pl.pallas_call(
        functools.partial(_pool_kernel, nseq=nseq, tm=tm, nt=nt, start=start),
        grid=(nb // nseq, nt),
        in_specs=[pl.BlockSpec((rows, POOL_WIDTH), lambda i, j: (i * nt + j, pvb)),
                  pl.BlockSpec((rows, POOL_WIDTH), lambda i, j: (i * nt + j, pzb)),
                  pl.BlockSpec((nseq, POOL_BUF, POOL_WIDTH), lambda i, j: (i, 0, 0)),
                  pl.BlockSpec((4, POOL_GROUP_WIDTH, POOL_GROUP_WIDTH), lambda i, j: (0, 0, 0)),
                  pl.BlockSpec((1, POOL_WIDTH), lambda i, j: (0, 0))],
        out_specs=[pl.BlockSpec((rows, POOL_WIDTH), lambda i, j: (i * nt + j, 0)),
                   pl.BlockSpec((nseq, POOL_BUF, POOL_WIDTH), lambda i, j: (i, 0, 0))],
        out_shape=[jax.ShapeDtypeStruct((t, POOL_WIDTH), BF16),
                   jax.ShapeDtypeStruct((nb, POOL_BUF, POOL_WIDTH), F32)],
        scratch_shapes=[pltpu.VMEM((16 + tm, POOL_WIDTH), F32),
                        pltpu.VMEM((rows, POOL_WIDTH), F32)],
        compiler_params=_cparams(("arbitrary", "arbitrary")),
        name="pool_branch",
    )(proj, proj, buf, w_grp, scale_row)


def _memkv_kernel(m_ref, g_ref, w_ref, k_ref, v_ref):
    x = m_ref[...]
    ms = jnp.mean(x * x, axis=-1, keepdims=True)
    h = (x * lax.rsqrt(ms + EPS) * g_ref[...]).astype(BF16)
    kv = jnp.dot(h, w_ref[...], preferred_element_type=F32)
    k_ref[...] = kv[:, :MEM_WIDTH]
    v_ref[...] = kv[:, MEM_WIDTH:]


def _mem_kv(mem2d, g_row, w_bf, tm):
    t, d = mem2d.shape
    return pl.pallas_call(
        _memkv_kernel,
        grid=(t // tm,),
        in_specs=[pl.BlockSpec((tm, d), lambda i: (i, 0)),
                  pl.BlockSpec((1, d), lambda i: (0, 0)),
                  pl.BlockSpec((d, 2 * MEM_WIDTH), lambda i: (0, 0))],
        out_specs=[pl.BlockSpec((tm, MEM_WIDTH), lambda i: (i, 0)),
                   pl.BlockSpec((tm, MEM_WIDTH), lambda i: (i, 0))],
        out_shape=[jax.ShapeDtypeStruct((t, MEM_WIDTH), F32),
                   jax.ShapeDtypeStruct((t, MEM_WIDTH), F32)],
        compiler_params=_cparams(("parallel",)),
        name="mem_kv",
    )(mem2d, g_row, w_bf)


def _attn_kernel(cq_ref, cz_ref, k_ref, v_ref, xc_ref, *, nseq, tm):
    scale = MEM_HEAD_DIM ** -0.5
    for b in range(nseq):
        rs = slice(b * tm, (b + 1) * tm)
        for h in range(HEADS):
            cs = slice(h * MEM_HEAD_DIM, (h + 1) * MEM_HEAD_DIM)
            q = cq_ref[rs, cs].astype(BF16)
            k = k_ref[b, :, cs].astype(BF16)
            v = v_ref[b, :, cs].astype(BF16)
            s = lax.dot_general(q, k, (((1,), (1,)), ((), ())), preferred_element_type=F32) * scale
            p = jnp.exp(s - jnp.max(s, axis=-1, keepdims=True))
            a = p * (1.0 / jnp.sum(p, axis=-1, keepdims=True))
            o = jnp.dot(a.astype(BF16), v, preferred_element_type=F32)
            xc_ref[rs, cs] = (o * _silu(cz_ref[rs, cs])).astype(xc_ref.dtype)


def _mem_attn(proj, mem_k, mem_v, nb, seq_len, nseq, tm):
    assert nseq == 1 or tm == seq_len
    nt = seq_len // tm
    rows = nseq * tm
    t = nb * seq_len
    cqb, czb = COL_CQ // MEM_WIDTH, COL_CZ // MEM_WIDTH
    return pl.pallas_call(
        functools.partial(_attn_kernel, nseq=nseq, tm=tm),
        grid=(nb // nseq, nt),
        in_specs=[pl.BlockSpec((rows, MEM_WIDTH), lambda i, j: (i * nt + j, cqb)),
                  pl.BlockSpec((rows, MEM_WIDTH), lambda i, j: (i * nt + j, czb)),
                  pl.BlockSpec((nseq, MEM_TOKENS, MEM_WIDTH), lambda i, j: (i, 0, 0)),
                  pl.BlockSpec((nseq, MEM_TOKENS, MEM_WIDTH), lambda i, j: (i, 0, 0))],
        out_specs=pl.BlockSpec((rows, MEM_WIDTH), lambda i, j: (i * nt + j, 0)),
        out_shape=jax.ShapeDtypeStruct((t, MEM_WIDTH), BF16),
        compiler_params=_cparams(("parallel", "arbitrary")),
        name="mem_attn",
    )(proj, proj, mem_k, mem_v)


def _attn_short_kernel(cq_ref, cz_ref, k_ref, v_ref, xc_ref, *, nseq, tm):
    scale = MEM_HEAD_DIM ** -0.5
    nr, nc = tm * HEADS, MEM_TOKENS * HEADS
    row_head = lax.broadcasted_iota(jnp.int32, (nr, nc), 0) // tm
    col_head = lax.broadcasted_iota(jnp.int32, (nr, nc), 1) % HEADS
    same_head = row_head == col_head
    for b in range(nseq):
        rs = slice(b * tm, (b + 1) * tm)
        kf = k_ref[b].reshape(nc, MEM_HEAD_DIM).astype(BF16)
        vf = v_ref[b].reshape(nc, MEM_HEAD_DIM).astype(BF16)
        q = jnp.concatenate([cq_ref[rs, h * MEM_HEAD_DIM:(h + 1) * MEM_HEAD_DIM] for h in range(HEADS)], axis=0)
        s = lax.dot_general(q.astype(BF16), kf, (((1,), (1,)), ((), ())), preferred_element_type=F32) * scale
        s = jnp.where(same_head, s, -jnp.inf)
        p = jnp.exp(s - jnp.max(s, axis=-1, keepdims=True))
        a = p * (1.0 / jnp.sum(p, axis=-1, keepdims=True))
        o = jnp.dot(a.astype(BF16), vf, preferred_element_type=F32)
        for h in range(HEADS):
            cs = slice(h * MEM_HEAD_DIM, (h + 1) * MEM_HEAD_DIM)
            xc_ref[rs, cs] = o[h * tm:(h + 1) * tm, :] * _silu(cz_ref[rs, cs])


def _inproj_attn_kernel(h_ref, w_ref, cq_ref, cz_ref, k_ref, v_ref, o_ref, xc_ref, *, nseq, seq_len):
    o_ref[...] = lax.dot_general(h_ref[...], w_ref[...], (((1,), (1,)), ((), ())), preferred_element_type=F32)
    _attn_short_kernel(cq_ref, cz_ref, k_ref, v_ref, xc_ref, nseq=nseq, tm=seq_len)


def _inproj_with_attn(h, w_bf, tm, tn, proj_s, mem_k, mem_v, seq_len):
    t, d = h.shape
    n_cols = w_bf.shape[0]
    nb = mem_k.shape[0]
    n_inner = t // tm
    n_steps = (n_cols // tn) * n_inner
    nseq = nb // n_steps
    rows = nseq * seq_len
    assert n_cols % tn == 0 and t % tm == 0 and nseq * n_steps == nb and rows % SUBLANES == 0
    cqb, czb = COL_CQ // MEM_WIDTH, COL_CZ // MEM_WIDTH
    kv_spec = pl.BlockSpec((nseq, MEM_TOKENS, HEADS, MEM_HEAD_DIM), lambda j, i: (j * n_inner + i, 0, 0, 0))
    return pl.pallas_call(
        functools.partial(_inproj_attn_kernel, nseq=nseq, seq_len=seq_len),
        grid=(n_cols // tn, n_inner),
        in_specs=[pl.BlockSpec((tm, d), lambda j, i: (i, 0)),
                  pl.BlockSpec((tn, d), lambda j, i: (j, 0)),
                  pl.BlockSpec((rows, MEM_WIDTH), lambda j, i: (j * n_inner + i, cqb)),
                  pl.BlockSpec((rows, MEM_WIDTH), lambda j, i: (j * n_inner + i, czb)),
                  kv_spec, kv_spec],
        out_specs=[pl.BlockSpec((tm, tn), lambda j, i: (i, j)),
                   pl.BlockSpec((rows, MEM_WIDTH), lambda j, i: (j * n_inner + i, 0))],
        out_shape=[jax.ShapeDtypeStruct((t, n_cols), F32),
                   jax.ShapeDtypeStruct((nb * seq_len, MEM_WIDTH), F32)],
        compiler_params=_cparams(("arbitrary", "arbitrary")),
        name="inproj_attn",
    )(h, w_bf, proj_s, proj_s, mem_k, mem_v)


def _mlstm_intra(q_bf, ks_bf, v_bf, ig_row, b_row, b_col, m0_col, mask):
    dlog = jnp.where(mask, b_col - b_row + ig_row, -jnp.inf)
    inter = b_col + m0_col
    m_col = jnp.maximum(inter, jnp.max(dlog, axis=1, keepdims=True))
    dw = jnp.exp(dlog - m_col)
    inter_w = jnp.exp(inter - m_col)
    s = lax.dot_general(q_bf, ks_bf, (((1,), (1,)), ((), ())), preferred_element_type=F32) * dw
    num = jnp.dot(s.astype(BF16), v_bf, preferred_element_type=F32)
    den = jnp.sum(s, axis=1, keepdims=True)
    return num, den, m_col, inter_w


def _mlstm_prompt_kernel(q_ref, k_ref, v_ref, o_ref, z_ref, g_ref, gt_ref,
                         xb_ref, c_ref, n_ref, mcol_ref, m_scr):
    @pl.when(pl.program_id(1) == 0)
    def _():
        c_ref[...] = jnp.zeros_like(c_ref)
        n_ref[...] = jnp.zeros_like(n_ref)
        m_scr[...] = jnp.zeros_like(m_scr)

    cl = PROMPT_CHUNK
    row = lax.broadcasted_iota(jnp.int32, (cl, cl), 0)
    col = lax.broadcasted_iota(jnp.int32, (cl, cl), 1)
    mask = col <= row
    kscale = MLSTM_HEAD_DIM ** -0.5
    mcol_ref[...] = jnp.zeros_like(mcol_ref)
    for h in range(HEADS):
        cs = slice(h * MLSTM_HEAD_DIM, (h + 1) * MLSTM_HEAD_DIM)
        ig_col = g_ref[:, h:h + 1]
        b_col = g_ref[:, HEADS + h:HEADS + h + 1]
        ig_row = gt_ref[h:h + 1, :]
        b_row = gt_ref[HEADS + h:HEADS + h + 1, :]
        m0 = m_scr[h:h + 1, 0:1]
        qf = q_ref[:, cs]
        kf = k_ref[:, cs] * kscale
        vf = v_ref[:, cs]
        q_bf, ks_bf = qf.astype(BF16), kf.astype(BF16)
        num, den, m_col, inter_w = _mlstm_intra(q_bf, ks_bf, vf.astype(BF16), ig_row, b_row, b_col, m0, mask)
        c0 = c_ref[0, h]
        n0 = n_ref[0, h:h + 1, :]
        num = num + inter_w * lax.dot_general(q_bf, c0.astype(BF16), (((1,), (1,)), ((), ())),
                                              preferred_element_type=F32)
        den = den + inter_w * jnp.sum(qf * n0, axis=1, keepdims=True)
        hval = num * (1.0 / jnp.maximum(jnp.abs(den), jnp.exp(-m_col)))
        xb_ref[:, cs] = (_sigmoid(o_ref[:, cs]) * hval * _silu(z_ref[:, cs])).astype(xb_ref.dtype)
        m_new = m_col[cl - 1:cl, :]
        b_last = b_col[cl - 1:cl, :]
        w_col = jnp.exp(b_last - b_col + ig_col - m_new)
        decay = jnp.exp(b_last + m0 - m_new)
        dc = lax.dot_general((w_col * vf).astype(BF16), ks_bf, (((0,), (0,)), ((), ())),
                             preferred_element_type=F32)
        c_ref[0, h] = decay * c0 + dc
        n_ref[0, h:h + 1, :] = decay * n0 + jnp.sum(w_col * kf, axis=0, keepdims=True)
        m_scr[h:h + 1, :] = jnp.broadcast_to(m_new, (1, LANES))
        mcol_ref[:, h:h + 1] = m_col


def _mlstm_prompt(proj, g, gt, nb, seq_len):
    cl = PROMPT_CHUNK
    nc = seq_len // cl
    t = nb * seq_len
    w = MLSTM_WIDTH

    def colspec(col0):
        return pl.BlockSpec((cl, w), lambda b, c: (b * nc + c, col0 // w))

    return pl.pallas_call(
        _mlstm_prompt_kernel,
        grid=(nb, nc),
        in_specs=[colspec(COL_Q), colspec(COL_K), colspec(COL_V), colspec(COL_O), colspec(COL_Z),
                  pl.BlockSpec((cl, LANES), lambda b, c: (b * nc + c, 0)),
                  pl.BlockSpec((8, cl), lambda b, c: (0, b * nc + c))],
        out_specs=[pl.BlockSpec((cl, w), lambda b, c: (b * nc + c, 0)),
                   pl.BlockSpec((1, HEADS, MLSTM_HEAD_DIM, MLSTM_HEAD_DIM), lambda b, c: (b, 0, 0, 0)),
                   pl.BlockSpec((1, HEADS, MLSTM_HEAD_DIM), lambda b, c: (b, 0, 0)),
                   pl.BlockSpec((cl, LANES), lambda b, c: (b * nc + c, 0))],
        out_shape=[jax.ShapeDtypeStruct((t, w), BF16),
                   jax.ShapeDtypeStruct((nb, HEADS, MLSTM_HEAD_DIM, MLSTM_HEAD_DIM), F32),
                   jax.ShapeDtypeStruct((nb, HEADS, MLSTM_HEAD_DIM), F32),
                   jax.ShapeDtypeStruct((t, LANES), F32)],
        scratch_shapes=[pltpu.VMEM((8, LANES), F32)],
        compiler_params=_cparams(("arbitrary", "arbitrary")),
        name="mlstm_prompt",
    )(proj, proj, proj, proj, proj, g, gt)


STAT_DEN, STAT_IW, STAT_M = 0, HEADS, 2 * HEADS


def _mlstm_short_intra_kernel(q_ref, k_ref, v_ref, g_ref, gt_ref, num_ref, stat_ref, *, seq_len):
    tl = LANES
    kscale = MLSTM_HEAD_DIM ** -0.5
    shift = seq_len.bit_length() - 1
    row = lax.broadcasted_iota(jnp.int32, (tl, tl), 0)
    col = lax.broadcasted_iota(jnp.int32, (tl, tl), 1)
    mask = (col <= row) & (lax.shift_right_logical(col, shift) == lax.shift_right_logical(row, shift))
    stat_ref[...] = jnp.zeros_like(stat_ref)
    for h in range(HEADS):
        cs = slice(h * MLSTM_HEAD_DIM, (h + 1) * MLSTM_HEAD_DIM)
        b_col = g_ref[:, HEADS + h:HEADS + h + 1]
        m0_col = g_ref[:, 2 * HEADS + h:2 * HEADS + h + 1]
        ig_row = gt_ref[h:h + 1, :]
        b_row = gt_ref[HEADS + h:HEADS + h + 1, :]
        q_bf = q_ref[:, cs].astype(BF16)
        ks_bf = (k_ref[:, cs] * kscale).astype(BF16)
        num, den, m_col, inter_w = _mlstm_intra(q_bf, ks_bf, v_ref[:, cs].astype(BF16),
                                                ig_row, b_row, b_col, m0_col, mask)
        num_ref[:, cs] = num
        stat_ref[:, STAT_DEN + h:STAT_DEN + h + 1] = den
        stat_ref[:, STAT_IW + h:STAT_IW + h + 1] = inter_w
        stat_ref[:, STAT_M + h:STAT_M + h + 1] = m_col


def _mlstm_short_intra(proj, g, gt, seq_len):
    t = proj.shape[0]
    tl = LANES
    w = MLSTM_WIDTH
    assert tl % seq_len == 0 and t % tl == 0

    def colspec(col0):
        return pl.BlockSpec((tl, w), lambda i: (i, col0 // w))

    return pl.pallas_call(
        functools.partial(_mlstm_short_intra_kernel, seq_len=seq_len),
        grid=(t // tl,),
        in_specs=[colspec(COL_Q), colspec(COL_K), colspec(COL_V),
                  pl.BlockSpec((tl, LANES), lambda i: (i, 0)),
                  pl.BlockSpec((8, tl), lambda i: (0, i))],
        out_specs=[pl.BlockSpec((tl, w), lambda i: (i, 0)),
                   pl.BlockSpec((tl, LANES), lambda i: (i, 0))],
        out_shape=[jax.ShapeDtypeStruct((t, w), F32),
                   jax.ShapeDtypeStruct((t, LANES), F32)],
        compiler_params=_cparams(("parallel",)),
        name="mlstm_short_intra",
    )(proj, proj, proj, g, gt)


def _mlstm_state_step(q_ref, k_ref, v_ref, o_ref, z_ref, num_ref, g_ref, stat_ref, c0_ref, n0_ref,
                      xb_ref, c_ref, n_ref):
    kscale = MLSTM_HEAD_DIM ** -0.5
    seq_len = q_ref.shape[0]
    last = slice(seq_len - 1, seq_len)
    for h in range(HEADS):
        cs = slice(h * MLSTM_HEAD_DIM, (h + 1) * MLSTM_HEAD_DIM)
        qf = q_ref[:, cs]
        kf = k_ref[:, cs] * kscale
        vf = v_ref[:, cs]
        ig_col = g_ref[:, h:h + 1]
        b_col = g_ref[:, HEADS + h:HEADS + h + 1]
        b_last = g_ref[last, HEADS + h:HEADS + h + 1]
        m0 = g_ref[last, 2 * HEADS + h:2 * HEADS + h + 1]
        den_intra = stat_ref[:, STAT_DEN + h:STAT_DEN + h + 1]
        inter_w = stat_ref[:, STAT_IW + h:STAT_IW + h + 1]
        m_col = stat_ref[:, STAT_M + h:STAT_M + h + 1]
        m_new = stat_ref[last, STAT_M + h:STAT_M + h + 1]
        c0 = c0_ref[0, h]
        n0 = n0_ref[0, h:h + 1, :]
        num = num_ref[:, cs] + inter_w * lax.dot_general(
            qf.astype(BF16), c0.astype(BF16), (((1,), (1,)), ((), ())), preferred_element_type=F32)
        den = den_intra + inter_w * jnp.sum(qf * n0, axis=1, keepdims=True)
        hval = num * (1.0 / jnp.maximum(jnp.abs(den), jnp.exp(-m_col)))
        xb_ref[:, cs] = (_sigmoid(o_ref[:, cs]) * hval * _silu(z_ref[:, cs])).astype(xb_ref.dtype)
        w_col = jnp.exp(b_last - b_col + ig_col - m_new)
        decay = jnp.exp(b_last + m0 - m_new)
        dc = lax.dot_general((w_col * vf).astype(BF16), kf.astype(BF16), (((0,), (0,)), ((), ())),
                             preferred_element_type=F32)
        c_ref[0, h] = decay * c0 + dc
        n_ref[0, h:h + 1, :] = decay * n0 + jnp.sum(w_col * kf, axis=0, keepdims=True)


def _inproj_state_kernel(h_ref, w_ref, *refs):
    state_refs, o_ref, out_state_refs = refs[:10], refs[10], refs[11:]
    o_ref[...] = lax.dot_general(h_ref[...], w_ref[...], (((1,), (1,)), ((), ())), preferred_element_type=F32)
    _mlstm_state_step(*state_refs, *out_state_refs)


def _inproj_with_state(h, w_bf, tm, tn, proj_s, num_s, g_s, stat_s, c0, n0, seq_len):
    t, d = h.shape
    n_cols = w_bf.shape[0]
    n_seq = c0.shape[0]
    n_inner = t // tm
    assert n_cols % tn == 0 and t % tm == 0 and (n_cols // tn) * n_inner == n_seq
    w = MLSTM_WIDTH
    hd = MLSTM_HEAD_DIM

    def seq(j, i):
        return j * n_inner + i

    def colspec(col0):
        return pl.BlockSpec((seq_len, w), lambda j, i: (seq(j, i), col0 // w))

    rowspec = pl.BlockSpec((seq_len, w), lambda j, i: (seq(j, i), 0))
    lanespec = pl.BlockSpec((seq_len, LANES), lambda j, i: (seq(j, i), 0))
    state4 = pl.BlockSpec((1, HEADS, hd, hd), lambda j, i: (seq(j, i), 0, 0, 0))
    state3 = pl.BlockSpec((1, HEADS, hd), lambda j, i: (seq(j, i), 0, 0))
    return pl.pallas_call(
        _inproj_state_kernel,
        grid=(n_cols // tn, n_inner),
        in_specs=[pl.BlockSpec((tm, d), lambda j, i: (i, 0)),
                  pl.BlockSpec((tn, d), lambda j, i: (j, 0)),
                  colspec(COL_Q), colspec(COL_K), colspec(COL_V), colspec(COL_O), colspec(COL_Z),
                  rowspec, lanespec, lanespec, state4, state3],
        out_specs=[pl.BlockSpec((tm, tn), lambda j, i: (i, j)), rowspec, state4, state3],
        out_shape=[jax.ShapeDtypeStruct((t, n_cols), F32),
                   jax.ShapeDtypeStruct((n_seq * seq_len, w), F32),
                   jax.ShapeDtypeStruct((n_seq, HEADS, hd, hd), F32),
                   jax.ShapeDtypeStruct((n_seq, HEADS, hd), F32)],
        compiler_params=_cparams(("arbitrary", "arbitrary")),
        name="inproj_state",
    )(h, w_bf, proj_s, proj_s, proj_s, proj_s, proj_s, num_s, g_s, stat_s, c0, n0)


def _final_kernel(xa_ref, xb_ref, xc_ref, ga_ref, gb_ref, gc_ref, x_ref, gpost_ref,
                  wa_ref, wb_ref, wc_ref, wo_ref, y_ref):
    ya = jnp.dot(xa_ref[...].astype(BF16), wa_ref[...], preferred_element_type=F32)
    merged = _sigmoid(ga_ref[...]) * ya
    yb = jnp.dot(xb_ref[...].astype(BF16), wb_ref[...], preferred_element_type=F32)
    merged = merged + _sigmoid(gb_ref[...]) * yb
    yc = jnp.dot(xc_ref[...].astype(BF16), wc_ref[...], preferred_element_type=F32)
    merged = merged + _sigmoid(gc_ref[...]) * yc
    out = jnp.dot(merged.astype(BF16), wo_ref[...], preferred_element_type=F32)
    ms = jnp.mean(out * out, axis=-1, keepdims=True)
    y_ref[...] = x_ref[...] + out * lax.rsqrt(ms + EPS) * gpost_ref[...]


def _final(xa, xb, xc, proj, x2d, gpost_row, wa, wb, wc, wo, tm):
    t, d = x2d.shape

    def rowspec(width, colblk=0):
        return pl.BlockSpec((tm, width), lambda i: (i, colblk))

    def resident(shape):
        return pl.BlockSpec(shape, lambda i: (0, 0), pipeline_mode=pl.Buffered(1))

    return pl.pallas_call(
        _final_kernel,
        grid=(t // tm,),
        in_specs=[rowspec(POOL_WIDTH), rowspec(MLSTM_WIDTH), rowspec(MEM_WIDTH),
                  rowspec(d, COL_GA // d), rowspec(d, COL_GB // d), rowspec(d, COL_GC // d),
                  rowspec(d), resident((1, d)),
                  resident(wa.shape), resident(wb.shape), resident(wc.shape), resident(wo.shape)],
        out_specs=rowspec(d),
        out_shape=jax.ShapeDtypeStruct((t, d), F32),
        compiler_params=_cparams(("parallel",)),
        name="merge_out",
    )(xa, xb, xc, proj, proj, proj, x2d, gpost_row, wa, wb, wc, wo)


B_ROW0 = GATE_COL0 + N_GATE_COLS


def _pool_and_merge(x2d, nb, seq_len, proj_a, proj_b, xb, xc, pool_buf, wts, start):
    if seq_len >= 512:
        xa, new_buf = _pool_branch(proj_a, pool_buf, wts["w_pool_grp"], wts["pool_scale"], nb, seq_len,
                                   nseq=1, tm=512, start=start)
    else:
        xa, new_buf = _pool_branch(proj_a, pool_buf, wts["w_pool_grp"], wts["pool_scale"], nb, seq_len,
                                   nseq=32, tm=seq_len, start=start)
    y = _final(xa, xb, xc, proj_b, x2d, wts["g_post"], wts["w_br_pool"], wts["w_br_mlstm"],
               wts["w_br_mem"], wts["w_out"], tm=256)
    return y.reshape(nb, seq_len, x2d.shape[1]), new_buf


def kernel(x_prompt, x_sample, state_pool, state_mlstm_C, state_mlstm_n, state_mlstm_m, cache_mem_k,
           cache_mem_v, mem_prompt, g_pre, g_post, w_in, b_mlstm_i, b_mlstm_f, w_pool_grp, pool_scale,
           g_mem, w_mem_kv, w_br_pool, w_br_mlstm, w_br_mem, w_out):
    nbp = x_prompt.shape[0]
    assert w_in.shape[0] == 1, "single-layer problem: the kernels index layer 0 of the stacked weights"
    l = 0
    wts = {
        "g_pre": g_pre[l][None, :],
        "g_post": g_post[l][None, :],
        "w_in": jnp.transpose(w_in[l]),
        "gate_bias": jnp.pad(jnp.concatenate([b_mlstm_i[l], b_mlstm_f[l]]), (0, LANES - N_GATE_COLS))[None, :],
        "w_pool_grp": w_pool_grp[l].astype(BF16),
        "pool_scale": pool_scale[l][None, :],
        "w_br_pool": w_br_pool[l].astype(BF16),
        "w_br_mlstm": w_br_mlstm[l].astype(BF16),
        "w_br_mem": w_br_mem[l].astype(BF16),
        "w_out": w_out[l].astype(BF16),
    }
    mk2d, mv2d = _mem_kv(mem_prompt.reshape(nbp * MEM_TOKENS, D_MODEL), g_mem[l][None, :],
                         w_mem_kv[l].astype(BF16), tm=256)
    mk3, mv3 = mk2d.reshape(nbp, MEM_TOKENS, MEM_WIDTH), mv2d.reshape(nbp, MEM_TOKENS, MEM_WIDTH)
    w_t = wts["w_in"]

    nbs, ls, d = x_sample.shape
    ts = nbs * ls
    xs2d = x_sample.reshape(ts, d)
    m0_rows = jnp.pad(jnp.repeat(state_mlstm_m[l], ls, axis=0), ((0, 0), (2 * HEADS, LANES - 3 * HEADS)))
    hs, gs, gts = _norm_gate_prep(xs2d, wts["g_pre"], w_t, wts["gate_bias"], m0_rows, ls)
    proj_a_s, w_a_bf = _inproj(hs, w_t, 0, N_PROJ_A, tm=ts, tn=1024)
    proj_b_s, w_b_bf = _inproj(hs, w_t, B_ROW0, N_PROJ_B, tm=ts, tn=1024)
    num_s, stat_s = _mlstm_short_intra(proj_a_s, gs, gts, ls)
    m_s = stat_s.reshape(nbs, ls, LANES)[:, ls - 1, STAT_M:STAT_M + HEADS]

    _, lp, _ = x_prompt.shape
    tp = nbp * lp
    xp2d = x_prompt.reshape(tp, d)
    hp, gp, gtp = _norm_gate_prep(xp2d, wts["g_pre"], w_t, wts["gate_bias"], jnp.zeros((tp, LANES), F32),
                                  PROMPT_CHUNK)
    proj_a_p, xb_s, c_s, n_s = _inproj_with_state(hp, w_a_bf, 512, 1536, proj_a_s, num_s, gs, stat_s,
                                                  state_mlstm_C[l], state_mlstm_n[l], ls)
    proj_b_p, xc_s = _inproj_with_attn(hp, w_b_bf, 1024, 1024, proj_b_s, cache_mem_k[l], cache_mem_v[l], ls)
    xb_p, c_p, n_p, mcol_p = _mlstm_prompt(proj_a_p, gp, gtp, nbp, lp)
    m_p = mcol_p.reshape(nbp, lp, LANES)[:, lp - 1, :HEADS]
    xc_p = _mem_attn(proj_b_p, mk3, mv3, nbp, lp, nseq=1, tm=512)

    yp, pool_p = _pool_and_merge(xp2d, nbp, lp, proj_a_p, proj_b_p, xb_p, xc_p,
                                 jnp.zeros((nbp, POOL_BUF, POOL_WIDTH), F32), wts, 0)
    ys, pool_s = _pool_and_merge(xs2d, nbs, ls, proj_a_s, proj_b_s, xb_s, xc_s, state_pool[l], wts, PAST_LEN)
    kv_shape = (1, nbp, MEM_TOKENS, HEADS, MEM_HEAD_DIM)
    return (yp, ys, pool_p[None], c_p[None], n_p[None], m_p[None],
            mk2d.reshape(kv_shape), mv2d.reshape(kv_shape),
            pool_s[None], c_s[None], n_s[None], m_s[None])
```

```python
import functools

import jax
import jax.numpy as jnp
from jax import lax
from jax.experimental import pallas as pl
from jax.experimental.pallas import tpu as pltpu

F32 = jnp.float32
BF16 = jnp.bfloat16

D_MODEL = 2048
PAST_LEN = 16384
POOL_WINDOWS = (2, 4, 8, 16)
POOL_GROUP_WIDTH = 256
POOL_WIDTH = 1024
POOL_BUF = 15
HEADS = 4
MLSTM_HEAD_DIM = 512
MLSTM_WIDTH = 2048
PROMPT_CHUNK = 256
MEM_TOKENS = 256
MEM_HEAD_DIM = 256
MEM_WIDTH = 1024
EPS = 1e-6
N_GATE_COLS = 2 * HEADS
GATE_COL0 = 2 * POOL_WIDTH + 5 * MLSTM_WIDTH
LANES = 128
SUBLANES = 8
COL_PV, COL_PZ = 0, 1024
COL_Q, COL_K, COL_V, COL_O, COL_Z = 2048, 4096, 6144, 8192, 10240
COL_CQ, COL_CZ = 0, 1024
COL_GA, COL_GB, COL_GC = 2048, 4096, 6144
N_PROJ_A = GATE_COL0
N_PROJ_B = 2 * MEM_WIDTH + 3 * D_MODEL
VMEM_LIMIT = 56 * 1024 * 1024


def _cparams(sem):
    return pltpu.CompilerParams(dimension_semantics=sem, vmem_limit_bytes=VMEM_LIMIT)


def _sigmoid(x):
    return 0.5 * jnp.tanh(0.5 * x) + 0.5


def _silu(x):
    return x * _sigmoid(x)


def _inproj_kernel(h_ref, w_ref, o_ref, wbf_ref):
    @pl.when(pl.program_id(1) == 0)
    def _():
        wbf_ref[...] = w_ref[...].astype(BF16)

    o_ref[...] = lax.dot_general(h_ref[...], wbf_ref[...], (((1,), (1,)), ((), ())), preferred_element_type=F32)


def _inproj(h, w_t, row0, n_cols, tm, tn):
    t, d = h.shape
    assert row0 % SUBLANES == 0 and n_cols % tn == 0 and t % tm == 0 and row0 + n_cols <= w_t.shape[0]
    return pl.pallas_call(
        _inproj_kernel,
        grid=(n_cols // tn, t // tm),
        in_specs=[pl.BlockSpec((tm, d), lambda j, i: (i, 0)),
                  pl.BlockSpec((pl.Element(tn), pl.Element(d)),
                               lambda j, i: (pl.multiple_of(row0 + j * tn, SUBLANES), 0))],
        out_specs=[pl.BlockSpec((tm, tn), lambda j, i: (i, j)),
                   pl.BlockSpec((tn, d), lambda j, i: (j, 0))],
        out_shape=[jax.ShapeDtypeStruct((t, n_cols), F32),
                   jax.ShapeDtypeStruct((n_cols, d), BF16)],
        compiler_params=_cparams(("arbitrary", "arbitrary")),
        name="inproj",
    )(h, w_t)


GATE_TILE = 512


def _gate_kernel(x_ref, gpre_ref, wg_ref, bias_ref, m0_ref, h_ref, g_ref, gt_ref, *, seg, ct):
    x = x_ref[...]
    ms = jnp.mean(x * x, axis=-1, keepdims=True)
    h_ref[...] = (x * lax.rsqrt(ms + EPS) * gpre_ref[...]).astype(h_ref.dtype)
    row = lax.broadcasted_iota(jnp.int32, (ct, ct), 0)
    col = lax.broadcasted_iota(jnp.int32, (ct, ct), 1)
    same = col <= row
    if seg < ct:
        shift = seg.bit_length() - 1
        same = same & (lax.shift_right_logical(col, shift) == lax.shift_right_logical(row, shift))
    same = same.astype(F32)
    lane = lax.broadcasted_iota(jnp.int32, (ct, LANES), 1)
    wg = wg_ref[...].astype(BF16)
    for r in range(GATE_TILE // ct):
        rs = slice(r * ct, (r + 1) * ct)
        x = lax.dot_general(h_ref[rs, :], wg, (((1,), (1,)), ((), ())), preferred_element_type=F32) + bias_ref[...]
        lf = jnp.minimum(x, 0.0) - jnp.log1p(jnp.exp(-jnp.abs(x)))
        cums = jnp.dot(same, lf, precision=lax.Precision.HIGHEST, preferred_element_type=F32)
        g = jnp.where(lane < HEADS, x, jnp.where(lane < 2 * HEADS, cums, m0_ref[rs, :]))
        g_ref[rs, :] = g
        gt_ref[:, rs] = g.T[:8, :]


def _norm_gate_prep(x2d, gpre_row, w_t, bias_row, m0_rows, seg):
    t, d = x2d.shape
    tg = GATE_TILE
    ct = max(seg, LANES)
    assert GATE_COL0 % LANES == 0 and tg % ct == 0 and ct % seg == 0 and seg & (seg - 1) == 0
    return pl.pallas_call(
        functools.partial(_gate_kernel, seg=seg, ct=ct),
        grid=(t // tg,),
        in_specs=[pl.BlockSpec((tg, d), lambda i: (i, 0)),
                  pl.BlockSpec((1, d), lambda i: (0, 0)),
                  pl.BlockSpec((LANES, d), lambda i: (GATE_COL0 // LANES, 0)),
                  pl.BlockSpec((1, LANES), lambda i: (0, 0)),
                  pl.BlockSpec((tg, LANES), lambda i: (i, 0))],
        out_specs=[pl.BlockSpec((tg, d), lambda i: (i, 0)),
                   pl.BlockSpec((tg, LANES), lambda i: (i, 0)),
                   pl.BlockSpec((8, tg), lambda i: (0, i))],
        out_shape=[jax.ShapeDtypeStruct((t, d), BF16),
                   jax.ShapeDtypeStruct((t, LANES), F32),
                   jax.ShapeDtypeStruct((8, t), F32)],
        compiler_params=_cparams(("parallel",)),
        name="norm_gate_prep",
    )(x2d, gpre_row, w_t, bias_row, m0_rows)


def _pool_kernel(pv_ref, pz_ref, buf_ref, wg_ref, scale_ref, xa_ref, nbuf_ref, s_ref, p_ref,
                 *, nseq, tm, nt, start):
    j = pl.program_id(1)

    def one_seq(b, r0):
        if nt > 1:
            @pl.when(j == 0)
            def _():
                s_ref[1:16, :] = buf_ref[b]
        else:
            s_ref[1:16, :] = buf_ref[b]
        s_ref[16:16 + tm, :] = pv_ref[pl.ds(r0, tm), :]
        pos = start + j * tm + lax.broadcasted_iota(jnp.int32, (tm, 1), 0)
        for g, w in enumerate(POOL_WINDOWS):
            cs = slice(g * POOL_GROUP_WIDTH, (g + 1) * POOL_GROUP_WIDTH)
            acc = s_ref[16:16 + tm, cs]
            for i in range(1, w):
                acc = acc + s_ref[16 - i:16 - i + tm, cs]
            inv_cnt = 1.0 / jnp.minimum(w, pos + 1).astype(F32)
            p_ref[pl.ds(r0, tm), cs] = acc * inv_cnt - s_ref[16:16 + tm, cs]
        tail = s_ref[tm + 1:tm + 16, :]
        nbuf_ref[b] = tail
        if nt > 1:
            s_ref[1:16, :] = tail

    if nseq == 1:
        one_seq(0, 0)
    else:
        def body(b, carry):
            one_seq(b, pl.multiple_of(b * tm, 8))
            return carry
        lax.fori_loop(0, nseq, body, 0)

    for g in range(len(POOL_WINDOWS)):
        cs = slice(g * POOL_GROUP_WIDTH, (g + 1) * POOL_GROUP_WIDTH)
        pa = jnp.dot(p_ref[:, cs].astype(BF16), wg_ref[g], preferred_element_type=F32) * scale_ref[:, cs]
        xa_ref[:, cs] = (pa * _silu(pz_ref[:, cs])).astype(xa_ref.dtype)


def _pool_branch(proj, buf, w_grp, scale_row, nb, seq_len, nseq, tm, start):
    assert nseq == 1 or tm == seq_len
    nt = seq_len // tm
    rows = nseq * tm
    t = nb * seq_len
    pvb, pzb = COL_PV // POOL_WIDTH, COL_PZ // POOL_WIDTH
    return pl.pallas_call(
        functools.partial(_pool_kernel, nseq=nseq, tm=tm, nt=nt, start=start),
        grid=(nb // nseq, nt),
        in_specs=[pl.BlockSpec((rows, POOL_WIDTH), lambda i, j: (i * nt + j, pvb)),
                  pl.BlockSpec((rows, POOL_WIDTH), lambda i, j: (i * nt + j, pzb)),
                  pl.BlockSpec((nseq, POOL_BUF, POOL_WIDTH), lambda i, j: (i, 0, 0)),
                  pl.BlockSpec((4, POOL_GROUP_WIDTH, POOL_GROUP_WIDTH), lambda i, j: (0, 0, 0)),
                  pl.BlockSpec((1, POOL_WIDTH), lambda i, j: (0, 0))],
        out_specs=[pl.BlockSpec((rows, POOL_WIDTH), lambda i, j: (i * nt + j, 0)),
                   pl.BlockSpec((nseq, POOL_BUF, POOL_WIDTH), lambda i, j: (i, 0, 0))],
        out_shape=[jax.ShapeDtypeStruct((t, POOL_WIDTH), BF16),
                   jax.ShapeDtypeStruct((nb, POOL_BUF, POOL_WIDTH), F32)],
        scratch_shapes=[pltpu.VMEM((16 + tm, POOL_WIDTH), F32),
                        pltpu.VMEM((rows, POOL_WIDTH), F32)],
        compiler_params=_cparams(("arbitrary", "arbitrary")),
        name="pool_branch",
    )(proj, proj, buf, w_grp, scale_row)


def _memkv_kernel(m_ref, g_ref, w_ref, k_ref, v_ref):
    x = m_ref[...]
    ms = jnp.mean(x * x, axis=-1, keepdims=True)
    h = (x * lax.rsqrt(ms + EPS) * g_ref[...]).astype(BF16)
    kv = jnp.dot(h, w_ref[...], preferred_element_type=F32)
    k_ref[...] = kv[:, :MEM_WIDTH]
    v_ref[...] = kv[:, MEM_WIDTH:]


def _mem_kv(mem2d, g_row, w_bf, tm):
    t, d = mem2d.shape
    return pl.pallas_call(
        _memkv_kernel,
        grid=(t // tm,),
        in_specs=[pl.BlockSpec((tm, d), lambda i: (i, 0)),
                  pl.BlockSpec((1, d), lambda i: (0, 0)),
                  pl.BlockSpec((d, 2 * MEM_WIDTH), lambda i: (0, 0))],
        out_specs=[pl.BlockSpec((tm, MEM_WIDTH), lambda i: (i, 0)),
                   pl.BlockSpec((tm, MEM_WIDTH), lambda i: (i, 0))],
        out_shape=[jax.ShapeDtypeStruct((t, MEM_WIDTH), F32),
                   jax.ShapeDtypeStruct((t, MEM_WIDTH), F32)],
        compiler_params=_cparams(("parallel",)),
        name="mem_kv",
    )(mem2d, g_row, w_bf)


def _attn_kernel(cq_ref, cz_ref, k_ref, v_ref, xc_ref, *, nseq, tm):
    scale = MEM_HEAD_DIM ** -0.5
    for b in range(nseq):
        rs = slice(b * tm, (b + 1) * tm)
        for h in range(HEADS):
            cs = slice(h * MEM_HEAD_DIM, (h + 1) * MEM_HEAD_DIM)
            q = cq_ref[rs, cs].astype(BF16)
            k = k_ref[b, :, cs].astype(BF16)
            v = v_ref[b, :, cs].astype(BF16)
            s = lax.dot_general(q, k, (((1,), (1,)), ((), ())), preferred_element_type=F32) * scale
            p = jnp.exp(s - jnp.max(s, axis=-1, keepdims=True))
            a = p * (1.0 / jnp.sum(p, axis=-1, keepdims=True))
            o = jnp.dot(a.astype(BF16), v, preferred_element_type=F32)
            xc_ref[rs, cs] = (o * _silu(cz_ref[rs, cs])).astype(xc_ref.dtype)


def _mem_attn(proj, mem_k, mem_v, nb, seq_len, nseq, tm):
    assert nseq == 1 or tm == seq_len
    nt = seq_len // tm
    rows = nseq * tm
    t = nb * seq_len
    cqb, czb = COL_CQ // MEM_WIDTH, COL_CZ // MEM_WIDTH
    return pl.pallas_call(
        functools.partial(_attn_kernel, nseq=nseq, tm=tm),
        grid=(nb // nseq, nt),
        in_specs=[pl.BlockSpec((rows, MEM_WIDTH), lambda i, j: (i * nt + j, cqb)),
                  pl.BlockSpec((rows, MEM_WIDTH), lambda i, j: (i * nt + j, czb)),
                  pl.BlockSpec((nseq, MEM_TOKENS, MEM_WIDTH), lambda i, j: (i, 0, 0)),
                  pl.BlockSpec((nseq, MEM_TOKENS, MEM_WIDTH), lambda i, j: (i, 0, 0))],
        out_specs=pl.BlockSpec((rows, MEM_WIDTH), lambda i, j: (i * nt + j, 0)),
        out_shape=jax.ShapeDtypeStruct((t, MEM_WIDTH), BF16),
        compiler_params=_cparams(("parallel", "arbitrary")),
        name="mem_attn",
    )(proj, proj, mem_k, mem_v)


def _attn_short_kernel(cq_ref, cz_ref, k_ref, v_ref, xc_ref, *, nseq, tm):
    scale = MEM_HEAD_DIM ** -0.5
    nr, nc = tm * HEADS, MEM_TOKENS * HEADS
    row_head = lax.broadcasted_iota(jnp.int32, (nr, nc), 0) // tm
    col_head = lax.broadcasted_iota(jnp.int32, (nr, nc), 1) % HEADS
    same_head = row_head == col_head
    for b in range(nseq):
        rs = slice(b * tm, (b + 1) * tm)
        kf = k_ref[b].reshape(nc, MEM_HEAD_DIM).astype(BF16)
        vf = v_ref[b].reshape(nc, MEM_HEAD_DIM).astype(BF16)
        q = jnp.concatenate([cq_ref[rs, h * MEM_HEAD_DIM:(h + 1) * MEM_HEAD_DIM] for h in range(HEADS)], axis=0)
        s = lax.dot_general(q.astype(BF16), kf, (((1,), (1,)), ((), ())), preferred_element_type=F32) * scale
        s = jnp.where(same_head, s, -jnp.inf)
        p = jnp.exp(s - jnp.max(s, axis=-1, keepdims=True))
        a = p * (1.0 / jnp.sum(p, axis=-1, keepdims=True))
        o = jnp.dot(a.astype(BF16), vf, preferred_element_type=F32)
        for h in range(HEADS):
            cs = slice(h * MEM_HEAD_DIM, (h + 1) * MEM_HEAD_DIM)
            xc_ref[rs, cs] = o[h * tm:(h + 1) * tm, :] * _silu(cz_ref[rs, cs])


def _inproj_attn_kernel(h_ref, w_ref, cq_ref, cz_ref, k_ref, v_ref, o_ref, xc_ref, *, nseq, seq_len):
    o_ref[...] = lax.dot_general(h_ref[...], w_ref[...], (((1,), (1,)), ((), ())), preferred_element_type=F32)
    _attn_short_kernel(cq_ref, cz_ref, k_ref, v_ref, xc_ref, nseq=nseq, tm=seq_len)


def _inproj_with_attn(h, w_bf, tm, tn, proj_s, mem_k, mem_v, seq_len):
    t, d = h.shape
    n_cols = w_bf.shape[0]
    nb = mem_k.shape[0]
    n_inner = t // tm
    n_steps = (n_cols // tn) * n_inner
    nseq = nb // n_steps
    rows = nseq * seq_len
    assert n_cols % tn == 0 and t % tm == 0 and nseq * n_steps == nb and rows % SUBLANES == 0
    cqb, czb = COL_CQ // MEM_WIDTH, COL_CZ // MEM_WIDTH
    kv_spec = pl.BlockSpec((nseq, MEM_TOKENS, HEADS, MEM_HEAD_DIM), lambda j, i: (j * n_inner + i, 0, 0, 0))
    return pl.pallas_call(
        functools.partial(_inproj_attn_kernel, nseq=nseq, seq_len=seq_len),
        grid=(n_cols // tn, n_inner),
        in_specs=[pl.BlockSpec((tm, d), lambda j, i: (i, 0)),
                  pl.BlockSpec((tn, d), lambda j, i: (j, 0)),
                  pl.BlockSpec((rows, MEM_WIDTH), lambda j, i: (j * n_inner + i, cqb)),
                  pl.BlockSpec((rows, MEM_WIDTH), lambda j, i: (j * n_inner + i, czb)),
                  kv_spec, kv_spec],
        out_specs=[pl.BlockSpec((tm, tn), lambda j, i: (i, j)),
                   pl.BlockSpec((rows, MEM_WIDTH), lambda j, i: (j * n_inner + i, 0))],
        out_shape=[jax.ShapeDtypeStruct((t, n_cols), F32),
                   jax.ShapeDtypeStruct((nb * seq_len, MEM_WIDTH), F32)],
        compiler_params=_cparams(("arbitrary", "arbitrary")),
        name="inproj_attn",
    )(h, w_bf, proj_s, proj_s, mem_k, mem_v)


def _mlstm_intra(q_bf, ks_bf, v_bf, ig_row, b_row, b_col, m0_col, mask):
    dlog = jnp.where(mask, b_col - b_row + ig_row, -jnp.inf)
    inter = b_col + m0_col
    m_col = jnp.maximum(inter, jnp.max(dlog, axis=1, keepdims=True))
    dw = jnp.exp(dlog - m_col)
    inter_w = jnp.exp(inter - m_col)
    s = lax.dot_general(q_bf, ks_bf, (((1,), (1,)), ((), ())), preferred_element_type=F32) * dw
    num = jnp.dot(s.astype(BF16), v_bf, preferred_element_type=F32)
    den = jnp.sum(s, axis=1, keepdims=True)
    return num, den, m_col, inter_w


def _mlstm_prompt_kernel(q_ref, k_ref, v_ref, o_ref, z_ref, g_ref, gt_ref,
                         xb_ref, c_ref, n_ref, mcol_ref, m_scr):
    @pl.when(pl.program_id(1) == 0)
    def _():
        c_ref[...] = jnp.zeros_like(c_ref)
        n_ref[...] = jnp.zeros_like(n_ref)
        m_scr[...] = jnp.zeros_like(m_scr)

    cl = PROMPT_CHUNK
    row = lax.broadcasted_iota(jnp.int32, (cl, cl), 0)
    col = lax.broadcasted_iota(jnp.int32, (cl, cl), 1)
    mask = col <= row
    kscale = MLSTM_HEAD_DIM ** -0.5
    mcol_ref[...] = jnp.zeros_like(mcol_ref)
    for h in range(HEADS):
        cs = slice(h * MLSTM_HEAD_DIM, (h + 1) * MLSTM_HEAD_DIM)
        ig_col = g_ref[:, h:h + 1]
        b_col = g_ref[:, HEADS + h:HEADS + h + 1]
        ig_row = gt_ref[h:h + 1, :]
        b_row = gt_ref[HEADS + h:HEADS + h + 1, :]
        m0 = m_scr[h:h + 1, 0:1]
        qf = q_ref[:, cs]
        kf = k_ref[:, cs] * kscale
        vf = v_ref[:, cs]
        q_bf, ks_bf = qf.astype(BF16), kf.astype(BF16)
        num, den, m_col, inter_w = _mlstm_intra(q_bf, ks_bf, vf.astype(BF16), ig_row, b_row, b_col, m0, mask)
        c0 = c_ref[0, h]
        n0 = n_ref[0, h:h + 1, :]
        num = num + inter_w * lax.dot_general(q_bf, c0.astype(BF16), (((1,), (1,)), ((), ())),
                                              preferred_element_type=F32)
        den = den + inter_w * jnp.sum(qf * n0, axis=1, keepdims=True)
        hval = num * (1.0 / jnp.maximum(jnp.abs(den), jnp.exp(-m_col)))
        xb_ref[:, cs] = (_sigmoid(o_ref[:, cs]) * hval * _silu(z_ref[:, cs])).astype(xb_ref.dtype)
        m_new = m_col[cl - 1:cl, :]
        b_last = b_col[cl - 1:cl, :]
        w_col = jnp.exp(b_last - b_col + ig_col - m_new)
        decay = jnp.exp(b_last + m0 - m_new)
        dc = lax.dot_general((w_col * vf).astype(BF16), ks_bf, (((0,), (0,)), ((), ())),
                             preferred_element_type=F32)
        c_ref[0, h] = decay * c0 + dc
        n_ref[0, h:h + 1, :] = decay * n0 + jnp.sum(w_col * kf, axis=0, keepdims=True)
        m_scr[h:h + 1, :] = jnp.broadcast_to(m_new, (1, LANES))
        mcol_ref[:, h:h + 1] = m_col


def _mlstm_prompt(proj, g, gt, nb, seq_len):
    cl = PROMPT_CHUNK
    nc = seq_len // cl
    t = nb * seq_len
    w = MLSTM_WIDTH

    def colspec(col0):
        return pl.BlockSpec((cl, w), lambda b, c: (b * nc + c, col0 // w))

    return pl.pallas_call(
        _mlstm_prompt_kernel,
        grid=(nb, nc),
        in_specs=[colspec(COL_Q), colspec(COL_K), colspec(COL_V), colspec(COL_O), colspec(COL_Z),
                  pl.BlockSpec((cl, LANES), lambda b, c: (b * nc + c, 0)),
                  pl.BlockSpec((8, cl), lambda b, c: (0, b * nc + c))],
        out_specs=[pl.BlockSpec((cl, w), lambda b, c: (b * nc + c, 0)),
                   pl.BlockSpec((1, HEADS, MLSTM_HEAD_DIM, MLSTM_HEAD_DIM), lambda b, c: (b, 0, 0, 0)),
                   pl.BlockSpec((1, HEADS, MLSTM_HEAD_DIM), lambda b, c: (b, 0, 0)),
                   pl.BlockSpec((cl, LANES), lambda b, c: (b * nc + c, 0))],
        out_shape=[jax.ShapeDtypeStruct((t, w), BF16),
                   jax.ShapeDtypeStruct((nb, HEADS, MLSTM_HEAD_DIM, MLSTM_HEAD_DIM), F32),
                   jax.ShapeDtypeStruct((nb, HEADS, MLSTM_HEAD_DIM), F32),
                   jax.ShapeDtypeStruct((t, LANES), F32)],
        scratch_shapes=[pltpu.VMEM((8, LANES), F32)],
        compiler_params=_cparams(("arbitrary", "arbitrary")),
        name="mlstm_prompt",
    )(proj, proj, proj, proj, proj, g, gt)


STAT_DEN, STAT_IW, STAT_M = 0, HEADS, 2 * HEADS


def _mlstm_short_intra_kernel(q_ref, k_ref, v_ref, g_ref, gt_ref, num_ref, stat_ref, *, seq_len):
    tl = LANES
    kscale = MLSTM_HEAD_DIM ** -0.5
    shift = seq_len.bit_length() - 1
    row = lax.broadcasted_iota(jnp.int32, (tl, tl), 0)
    col = lax.broadcasted_iota(jnp.int32, (tl, tl), 1)
    mask = (col <= row) & (lax.shift_right_logical(col, shift) == lax.shift_right_logical(row, shift))
    stat_ref[...] = jnp.zeros_like(stat_ref)
    for h in range(HEADS):
        cs = slice(h * MLSTM_HEAD_DIM, (h + 1) * MLSTM_HEAD_DIM)
        b_col = g_ref[:, HEADS + h:HEADS + h + 1]
        m0_col = g_ref[:, 2 * HEADS + h:2 * HEADS + h + 1]
        ig_row = gt_ref[h:h + 1, :]
        b_row = gt_ref[HEADS + h:HEADS + h + 1, :]
        q_bf = q_ref[:, cs].astype(BF16)
        ks_bf = (k_ref[:, cs] * kscale).astype(BF16)
        num, den, m_col, inter_w = _mlstm_intra(q_bf, ks_bf, v_ref[:, cs].astype(BF16),
                                                ig_row, b_row, b_col, m0_col, mask)
        num_ref[:, cs] = num
        stat_ref[:, STAT_DEN + h:STAT_DEN + h + 1] = den
        stat_ref[:, STAT_IW + h:STAT_IW + h + 1] = inter_w
        stat_ref[:, STAT_M + h:STAT_M + h + 1] = m_col


def _mlstm_short_intra(proj, g, gt, seq_len):
    t = proj.shape[0]
    tl = LANES
    w = MLSTM_WIDTH
    assert tl % seq_len == 0 and t % tl == 0

    def colspec(col0):
        return pl.BlockSpec((tl, w), lambda i: (i, col0 // w))

    return pl.pallas_call(
        functools.partial(_mlstm_short_intra_kernel, seq_len=seq_len),
        grid=(t // tl,),
        in_specs=[colspec(COL_Q), colspec(COL_K), colspec(COL_V),
                  pl.BlockSpec((tl, LANES), lambda i: (i, 0)),
                  pl.BlockSpec((8, tl), lambda i: (0, i))],
        out_specs=[pl.BlockSpec((tl, w), lambda i: (i, 0)),
                   pl.BlockSpec((tl, LANES), lambda i: (i, 0))],
        out_shape=[jax.ShapeDtypeStruct((t, w), F32),
                   jax.ShapeDtypeStruct((t, LANES), F32)],
        compiler_params=_cparams(("parallel",)),
        name="mlstm_short_intra",
    )(proj, proj, proj, g, gt)


def _mlstm_state_step(q_ref, k_ref, v_ref, o_ref, z_ref, num_ref, g_ref, stat_ref, c0_ref, n0_ref,
                      xb_ref, c_ref, n_ref):
    kscale = MLSTM_HEAD_DIM ** -0.5
    seq_len = q_ref.shape[0]
    last = slice(seq_len - 1, seq_len)
    for h in range(HEADS):
        cs = slice(h * MLSTM_HEAD_DIM, (h + 1) * MLSTM_HEAD_DIM)
        qf = q_ref[:, cs]
        kf = k_ref[:, cs] * kscale
        vf = v_ref[:, cs]
        ig_col = g_ref[:, h:h + 1]
        b_col = g_ref[:, HEADS + h:HEADS + h + 1]
        b_last = g_ref[last, HEADS + h:HEADS + h + 1]
        m0 = g_ref[last, 2 * HEADS + h:2 * HEADS + h + 1]
        den_intra = stat_ref[:, STAT_DEN + h:STAT_DEN + h + 1]
        inter_w = stat_ref[:, STAT_IW + h:STAT_IW + h + 1]
        m_col = stat_ref[:, STAT_M + h:STAT_M + h + 1]
        m_new = stat_ref[last, STAT_M + h:STAT_M + h + 1]
        c0 = c0_ref[0, h]
        n0 = n0_ref[0, h:h + 1, :]
        num = num_ref[:, cs] + inter_w * lax.dot_general(
            qf.astype(BF16), c0.astype(BF16), (((1,), (1,)), ((), ())), preferred_element_type=F32)
        den = den_intra + inter_w * jnp.sum(qf * n0, axis=1, keepdims=True)
        hval = num * (1.0 / jnp.maximum(jnp.abs(den), jnp.exp(-m_col)))
        xb_ref[:, cs] = (_sigmoid(o_ref[:, cs]) * hval * _silu(z_ref[:, cs])).astype(xb_ref.dtype)
        w_col = jnp.exp(b_last - b_col + ig_col - m_new)
        decay = jnp.exp(b_last + m0 - m_new)
        dc = lax.dot_general((w_col * vf).astype(BF16), kf.astype(BF16), (((0,), (0,)), ((), ())),
                             preferred_element_type=F32)
        c_ref[0, h] = decay * c0 + dc
        n_ref[0, h:h + 1, :] = decay * n0 + jnp.sum(w_col * kf, axis=0, keepdims=True)


def _inproj_state_kernel(h_ref, w_ref, *refs):
    state_refs, o_ref, out_state_refs = refs[:10], refs[10], refs[11:]
    o_ref[...] = lax.dot_general(h_ref[...], w_ref[...], (((1,), (1,)), ((), ())), preferred_element_type=F32)
    _mlstm_state_step(*state_refs, *out_state_refs)


def _inproj_with_state(h, w_bf, tm, tn, proj_s, num_s, g_s, stat_s, c0, n0, seq_len):
    t, d = h.shape
    n_cols = w_bf.shape[0]
    n_seq = c0.shape[0]
    n_inner = t // tm
    assert n_cols % tn == 0 and t % tm == 0 and (n_cols // tn) * n_inner == n_seq
    w = MLSTM_WIDTH
    hd = MLSTM_HEAD_DIM

    def seq(j, i):
        return j * n_inner + i

    def colspec(col0):
        return pl.BlockSpec((seq_len, w), lambda j, i: (seq(j, i), col0 // w))

    rowspec = pl.BlockSpec((seq_len, w), lambda j, i: (seq(j, i), 0))
    lanespec = pl.BlockSpec((seq_len, LANES), lambda j, i: (seq(j, i), 0))
    state4 = pl.BlockSpec((1, HEADS, hd, hd), lambda j, i: (seq(j, i), 0, 0, 0))
    state3 = pl.BlockSpec((1, HEADS, hd), lambda j, i: (seq(j, i), 0, 0))
    return pl.pallas_call(
        _inproj_state_kernel,
        grid=(n_cols // tn, n_inner),
        in_specs=[pl.BlockSpec((tm, d), lambda j, i: (i, 0)),
                  pl.BlockSpec((tn, d), lambda j, i: (j, 0)),
                  colspec(COL_Q), colspec(COL_K), colspec(COL_V), colspec(COL_O), colspec(COL_Z),
                  rowspec, lanespec, lanespec, state4, state3],
        out_specs=[pl.BlockSpec((tm, tn), lambda j, i: (i, j)), rowspec, state4, state3],
        out_shape=[jax.ShapeDtypeStruct((t, n_cols), F32),
                   jax.ShapeDtypeStruct((n_seq * seq_len, w), F32),
                   jax.ShapeDtypeStruct((n_seq, HEADS, hd, hd), F32),
                   jax.ShapeDtypeStruct((n_seq, HEADS, hd), F32)],
        compiler_params=_cparams(("arbitrary", "arbitrary")),
        name="inproj_state",
    )(h, w_bf, proj_s, proj_s, proj_s, proj_s, proj_s, num_s, g_s, stat_s, c0, n0)


def _final_kernel(xa_ref, xb_ref, xc_ref, ga_ref, gb_ref, gc_ref, x_ref, gpost_ref,
                  wa_ref, wb_ref, wc_ref, wo_ref, y_ref):
    ya = jnp.dot(xa_ref[...].astype(BF16), wa_ref[...], preferred_element_type=F32)
    merged = _sigmoid(ga_ref[...]) * ya
    yb = jnp.dot(xb_ref[...].astype(BF16), wb_ref[...], preferred_element_type=F32)
    merged = merged + _sigmoid(gb_ref[...]) * yb
    yc = jnp.dot(xc_ref[...].astype(BF16), wc_ref[...], preferred_element_type=F32)
    merged = merged + _sigmoid(gc_ref[...]) * yc
    out = jnp.dot(merged.astype(BF16), wo_ref[...], preferred_element_type=F32)
    ms = jnp.mean(out * out, axis=-1, keepdims=True)
    y_ref[...] = x_ref[...] + out * lax.rsqrt(ms + EPS) * gpost_ref[...]


def _final(xa, xb, xc, proj, x2d, gpost_row, wa, wb, wc, wo, tm):
    t, d = x2d.shape

    def rowspec(width, colblk=0):
        return pl.BlockSpec((tm, width), lambda i: (i, colblk))

    def resident(shape):
        return pl.BlockSpec(shape, lambda i: (0, 0), pipeline_mode=pl.Buffered(1))

    return pl.pallas_call(
        _final_kernel,
        grid=(t // tm,),
        in_specs=[rowspec(POOL_WIDTH), rowspec(MLSTM_WIDTH), rowspec(MEM_WIDTH),
                  rowspec(d, COL_GA // d), rowspec(d, COL_GB // d), rowspec(d, COL_GC // d),
                  rowspec(d), resident((1, d)),
                  resident(wa.shape), resident(wb.shape), resident(wc.shape), resident(wo.shape)],
        out_specs=rowspec(d),
        out_shape=jax.ShapeDtypeStruct((t, d), F32),
        compiler_params=_cparams(("parallel",)),
        name="merge_out",
    )(xa, xb, xc, proj, proj, proj, x2d, gpost_row, wa, wb, wc, wo)


B_ROW0 = GATE_COL0 + N_GATE_COLS


def _pool_and_merge(x2d, nb, seq_len, proj_a, proj_b, xb, xc, pool_buf, wts, start):
    if seq_len >= 512:
        xa, new_buf = _pool_branch(proj_a, pool_buf, wts["w_pool_grp"], wts["pool_scale"], nb, seq_len,
                                   nseq=1, tm=512, start=start)
    else:
        xa, new_buf = _pool_branch(proj_a, pool_buf, wts["w_pool_grp"], wts["pool_scale"], nb, seq_len,
                                   nseq=32, tm=seq_len, start=start)
    y = _final(xa, xb, xc, proj_b, x2d, wts["g_post"], wts["w_br_pool"], wts["w_br_mlstm"],
               wts["w_br_mem"], wts["w_out"], tm=256)
    return y.reshape(nb, seq_len, x2d.shape[1]), new_buf


def kernel(x_prompt, x_sample, state_pool, state_mlstm_C, state_mlstm_n, state_mlstm_m, cache_mem_k,
           cache_mem_v, mem_prompt, g_pre, g_post, w_in, b_mlstm_i, b_mlstm_f, w_pool_grp, pool_scale,
           g_mem, w_mem_kv, w_br_pool, w_br_mlstm, w_br_mem, w_out):
    nbp = x_prompt.shape[0]
    assert w_in.shape[0] == 1, "single-layer problem: the kernels index layer 0 of the stacked weights"
    l = 0
    wts = {
        "g_pre": g_pre[l][None, :],
        "g_post": g_post[l][None, :],
        "w_in": jnp.transpose(w_in[l]),
        "gate_bias": jnp.pad(jnp.concatenate([b_mlstm_i[l], b_mlstm_f[l]]), (0, LANES - N_GATE_COLS))[None, :],
        "w_pool_grp": w_pool_grp[l].astype(BF16),
        "pool_scale": pool_scale[l][None, :],
        "w_br_pool": w_br_pool[l].astype(BF16),
        "w_br_mlstm": w_br_mlstm[l].astype(BF16),
        "w_br_mem": w_br_mem[l].astype(BF16),
        "w_out": w_out[l].astype(BF16),
    }
    mk2d, mv2d = _mem_kv(mem_prompt.reshape(nbp * MEM_TOKENS, D_MODEL), g_mem[l][None, :],
                         w_mem_kv[l].astype(BF16), tm=256)
    mk3, mv3 = mk2d.reshape(nbp, MEM_TOKENS, MEM_WIDTH), mv2d.reshape(nbp, MEM_TOKENS, MEM_WIDTH)
    w_t = wts["w_in"]

    nbs, ls, d = x_sample.shape
    ts = nbs * ls
    xs2d = x_sample.reshape(ts, d)
    m0_rows = jnp.pad(jnp.repeat(state_mlstm_m[l], ls, axis=0), ((0, 0), (2 * HEADS, LANES - 3 * HEADS)))
    hs, gs, gts = _norm_gate_prep(xs2d, wts["g_pre"], w_t, wts["gate_bias"], m0_rows, ls)
    proj_a_s, w_a_bf = _inproj(hs, w_t, 0, N_PROJ_A, tm=ts, tn=1024)
    proj_b_s, w_b_bf = _inproj(hs, w_t, B_ROW0, N_PROJ_B, tm=ts, tn=1024)
    num_s, stat_s = _mlstm_short_intra(proj_a_s, gs, gts, ls)
    m_s = stat_s.reshape(nbs, ls, LANES)[:, ls - 1, STAT_M:STAT_M + HEADS]

    _, lp, _ = x_prompt.shape
    tp = nbp * lp
    xp2d = x_prompt.reshape(tp, d)
    hp, gp, gtp = _norm_gate_prep(xp2d, wts["g_pre"], w_t, wts["gate_bias"], jnp.zeros((tp, LANES), F32),
                                  PROMPT_CHUNK)
    proj_a_p, xb_s, c_s, n_s = _inproj_with_state(hp, w_a_bf, 256, 3072, proj_a_s, num_s, gs, stat_s,
                                                  state_mlstm_C[l], state_mlstm_n[l], ls)
    proj_b_p, xc_s = _inproj_with_attn(hp, w_b_bf, 1024, 1024, proj_b_s, cache_mem_k[l], cache_mem_v[l], ls)
    xb_p, c_p, n_p, mcol_p = _mlstm_prompt(proj_a_p, gp, gtp, nbp, lp)
    m_p = mcol_p.reshape(nbp, lp, LANES)[:, lp - 1, :HEADS]
    xc_p = _mem_attn(proj_b_p, mk3, mv3, nbp, lp, nseq=1, tm=512)

    yp, pool_p = _pool_and_merge(xp2d, nbp, lp, proj_a_p, proj_b_p, xb_p, xc_p,
                                 jnp.zeros((nbp, POOL_BUF, POOL_WIDTH), F32), wts, 0)
    ys, pool_s = _pool_and_merge(xs2d, nbs, ls, proj_a_s, proj_b_s, xb_s, xc_s, state_pool[l], wts, PAST_LEN)
    kv_shape = (1, nbp, MEM_TOKENS, HEADS, MEM_HEAD_DIM)
    return (yp, ys, pool_p[None], c_p[None], n_p[None], m_p[None],
            mk2d.reshape(kv_shape), mv2d.reshape(kv_shape),
            pool_s[None], c_s[None], n_s[None], m_s[None])
```

```python
import functools

import jax
import jax.numpy as jnp
from jax import lax
from jax.experimental import pallas as pl
from jax.experimental.pallas import tpu as pltpu

F32 = jnp.float32
BF16 = jnp.bfloat16

D_MODEL = 2048
PAST_LEN = 16384
POOL_WINDOWS = (2, 4, 8, 16)
POOL_GROUP_WIDTH = 256
POOL_WIDTH = 1024
POOL_BUF = 15
HEADS = 4
MLSTM_HEAD_DIM = 512
MLSTM_WIDTH = 2048
PROMPT_CHUNK = 256
MEM_TOKENS = 256
MEM_HEAD_DIM = 256
MEM_WIDTH = 1024
EPS = 1e-6
N_GATE_COLS = 2 * HEADS
GATE_COL0 = 2 * POOL_WIDTH + 5 * MLSTM_WIDTH
LANES = 128
SUBLANES = 8
COL_PV, COL_PZ = 0, 1024
COL_Q, COL_K, COL_V, COL_O, COL_Z = 2048, 4096, 6144, 8192, 10240
COL_CQ, COL_CZ = 0, 1024
COL_GA, COL_GB, COL_GC = 2048, 4096, 6144
N_PROJ_A = GATE_COL0
N_PROJ_B = 2 * MEM_WIDTH + 3 * D_MODEL
VMEM_LIMIT = 56 * 1024 * 1024


def _cparams(sem):
    return pltpu.CompilerParams(dimension_semantics=sem, vmem_limit_bytes=VMEM_LIMIT)


def _sigmoid(x):
    return 0.5 * jnp.tanh(0.5 * x) + 0.5


def _silu(x):
    return x * _sigmoid(x)


def _inproj_kernel(h_ref, w_ref, o_ref, wbf_ref):
    @pl.when(pl.program_id(1) == 0)
    def _():
        wbf_ref[...] = w_ref[...].astype(BF16)

    o_ref[...] = lax.dot_general(h_ref[...], wbf_ref[...], (((1,), (1,)), ((), ())), preferred_element_type=F32)


def _inproj(h, w_t, row0, n_cols, tm, tn):
    t, d = h.shape
    assert row0 % SUBLANES == 0 and n_cols % tn == 0 and t % tm == 0 and row0 + n_cols <= w_t.shape[0]
    return pl.pallas_call(
        _inproj_kernel,
        grid=(n_cols // tn, t // tm),
        in_specs=[pl.BlockSpec((tm, d), lambda j, i: (i, 0)),
                  pl.BlockSpec((pl.Element(tn), pl.Element(d)),
                               lambda j, i: (pl.multiple_of(row0 + j * tn, SUBLANES), 0))],
        out_specs=[pl.BlockSpec((tm, tn), lambda j, i: (i, j)),
                   pl.BlockSpec((tn, d), lambda j, i: (j, 0))],
        out_shape=[jax.ShapeDtypeStruct((t, n_cols), F32),
                   jax.ShapeDtypeStruct((n_cols, d), BF16)],
        compiler_params=_cparams(("arbitrary", "arbitrary")),
        name="inproj",
    )(h, w_t)


GATE_TILE = 512


def _gate_kernel(x_ref, gpre_ref, wg_ref, bias_ref, m0_ref, h_ref, g_ref, gt_ref, *, seg, ct):
    x = x_ref[...]
    ms = jnp.mean(x * x, axis=-1, keepdims=True)
    h_ref[...] = (x * lax.rsqrt(ms + EPS) * gpre_ref[...]).astype(h_ref.dtype)
    row = lax.broadcasted_iota(jnp.int32, (ct, ct), 0)
    col = lax.broadcasted_iota(jnp.int32, (ct, ct), 1)
    same = col <= row
    if seg < ct:
        shift = seg.bit_length() - 1
        same = same & (lax.shift_right_logical(col, shift) == lax.shift_right_logical(row, shift))
    same = same.astype(F32)
    lane = lax.broadcasted_iota(jnp.int32, (ct, LANES), 1)
    wg = wg_ref[...].astype(BF16)
    for r in range(GATE_TILE // ct):
        rs = slice(r * ct, (r + 1) * ct)
        x = lax.dot_general(h_ref[rs, :], wg, (((1,), (1,)), ((), ())), preferred_element_type=F32) + bias_ref[...]
        lf = jnp.minimum(x, 0.0) - jnp.log1p(jnp.exp(-jnp.abs(x)))
        cums = jnp.dot(same, lf, precision=lax.Precision.HIGHEST, preferred_element_type=F32)
        g = jnp.where(lane < HEADS, x, jnp.where(lane < 2 * HEADS, cums, m0_ref[rs, :]))
        g_ref[rs, :] = g
        gt_ref[:, rs] = g.T[:8, :]


def _norm_gate_prep(x2d, gpre_row, w_t, bias_row, m0_rows, seg):
    t, d = x2d.shape
    tg = GATE_TILE
    ct = max(seg, LANES)
    assert GATE_COL0 % LANES == 0 and tg % ct == 0 and ct % seg == 0 and seg & (seg - 1) == 0
    return pl.pallas_call(
        functools.partial(_gate_kernel, seg=seg, ct=ct),
        grid=(t // tg,),
        in_specs=[pl.BlockSpec((tg, d), lambda i: (i, 0)),
                  pl.BlockSpec((1, d), lambda i: (0, 0)),
                  pl.BlockSpec((LANES, d), lambda i: (GATE_COL0 // LANES, 0)),
                  pl.BlockSpec((1, LANES), lambda i: (0, 0)),
                  pl.BlockSpec((tg, LANES), lambda i: (i, 0))],
        out_specs=[pl.BlockSpec((tg, d), lambda i: (i, 0)),
                   pl.BlockSpec((tg, LANES), lambda i: (i, 0)),
                   pl.BlockSpec((8, tg), lambda i: (0, i))],
        out_shape=[jax.ShapeDtypeStruct((t, d), BF16),
                   jax.ShapeDtypeStruct((t, LANES), F32),
                   jax.ShapeDtypeStruct((8, t), F32)],
        compiler_params=_cparams(("parallel",)),
        name="norm_gate_prep",
    )(x2d, gpre_row, w_t, bias_row, m0_rows)


def _pool_kernel(pv_ref, pz_ref, buf_ref, wg_ref, scale_ref, xa_ref, nbuf_ref, s_ref, p_ref,
                 *, nseq, tm, nt, start):
    j = pl.program_id(1)

    def one_seq(b, r0):
        if nt > 1:
            @pl.when(j == 0)
            def _():
                s_ref[1:16, :] = buf_ref[b]
        else:
            s_ref[1:16, :] = buf_ref[b]
        s_ref[16:16 + tm, :] = pv_ref[pl.ds(r0, tm), :]
        pos = start + j * tm + lax.broadcasted_iota(jnp.int32, (tm, 1), 0)
        for g, w in enumerate(POOL_WINDOWS):
            cs = slice(g * POOL_GROUP_WIDTH, (g + 1) * POOL_GROUP_WIDTH)
            acc = s_ref[16:16 + tm, cs]
            for i in range(1, w):
                acc = acc + s_ref[16 - i:16 - i + tm, cs]
            inv_cnt = 1.0 / jnp.minimum(w, pos + 1).astype(F32)
            p_ref[pl.ds(r0, tm), cs] = acc * inv_cnt - s_ref[16:16 + tm, cs]
        tail = s_ref[tm + 1:tm + 16, :]
        nbuf_ref[b] = tail
        if nt > 1:
            s_ref[1:16, :] = tail

    if nseq == 1:
        one_seq(0, 0)
    else:
        def body(b, carry):
            one_seq(b, pl.multiple_of(b * tm, 8))
            return carry
        lax.fori_loop(0, nseq, body, 0)

    for g in range(len(POOL_WINDOWS)):
        cs = slice(g * POOL_GROUP_WIDTH, (g + 1) * POOL_GROUP_WIDTH)
        pa = jnp.dot(p_ref[:, cs].astype(BF16), wg_ref[g], preferred_element_type=F32) * scale_ref[:, cs]
        xa_ref[:, cs] = (pa * _silu(pz_ref[:, cs])).astype(xa_ref.dtype)


def _pool_branch(proj, buf, w_grp, scale_row, nb, seq_len, nseq, tm, start):
    assert nseq == 1 or tm == seq_len
    nt = seq_len // tm
    rows = nseq * tm
    t = nb * seq_len
    pvb, pzb = COL_PV // POOL_WIDTH, COL_PZ // POOL_WIDTH
    return pl.pallas_call(
        functools.partial(_pool_kernel, nseq=nseq, tm=tm, nt=nt, start=start),
        grid=(nb // nseq, nt),
        in_specs=[pl.BlockSpec((rows, POOL_WIDTH), lambda i, j: (i * nt + j, pvb)),
                  pl.BlockSpec((rows, POOL_WIDTH), lambda i, j: (i * nt + j, pzb)),
                  pl.BlockSpec((nseq, POOL_BUF, POOL_WIDTH), lambda i, j: (i, 0, 0)),
                  pl.BlockSpec((4, POOL_GROUP_WIDTH, POOL_GROUP_WIDTH), lambda i, j: (0, 0, 0)),
                  pl.BlockSpec((1, POOL_WIDTH), lambda i, j: (0, 0))],
        out_specs=[pl.BlockSpec((rows, POOL_WIDTH), lambda i, j: (i * nt + j, 0)),
                   pl.BlockSpec((nseq, POOL_BUF, POOL_WIDTH), lambda i, j: (i, 0, 0))],
        out_shape=[jax.ShapeDtypeStruct((t, POOL_WIDTH), BF16),
                   jax.ShapeDtypeStruct((nb, POOL_BUF, POOL_WIDTH), F32)],
        scratch_shapes=[pltpu.VMEM((16 + tm, POOL_WIDTH), F32),
                        pltpu.VMEM((rows, POOL_WIDTH), F32)],
        compiler_params=_cparams(("arbitrary", "arbitrary")),
        name="pool_branch",
    )(proj, proj, buf, w_grp, scale_row)


def _memkv_kernel(m_ref, g_ref, w_ref, k_ref, v_ref):
    x = m_ref[...]
    ms = jnp.mean(x * x, axis=-1, keepdims=True)
    h = (x * lax.rsqrt(ms + EPS) * g_ref[...]).astype(BF16)
    kv = jnp.dot(h, w_ref[...], preferred_element_type=F32)
    k_ref[...] = kv[:, :MEM_WIDTH]
    v_ref[...] = kv[:, MEM_WIDTH:]


def _mem_kv(mem2d, g_row, w_bf, tm):
    t, d = mem2d.shape
    return pl.pallas_call(
        _memkv_kernel,
        grid=(t // tm,),
        in_specs=[pl.BlockSpec((tm, d), lambda i: (i, 0)),
                  pl.BlockSpec((1, d), lambda i: (0, 0)),
                  pl.BlockSpec((d, 2 * MEM_WIDTH), lambda i: (0, 0))],
        out_specs=[pl.BlockSpec((tm, MEM_WIDTH), lambda i: (i, 0)),
                   pl.BlockSpec((tm, MEM_WIDTH), lambda i: (i, 0))],
        out_shape=[jax.ShapeDtypeStruct((t, MEM_WIDTH), F32),
                   jax.ShapeDtypeStruct((t, MEM_WIDTH), F32)],
        compiler_params=_cparams(("parallel",)),
        name="mem_kv",
    )(mem2d, g_row, w_bf)


def _attn_kernel(cq_ref, cz_ref, k_ref, v_ref, xc_ref, *, nseq, tm):
    scale = MEM_HEAD_DIM ** -0.5
    for b in range(nseq):
        rs = slice(b * tm, (b + 1) * tm)
        for h in range(HEADS):
            cs = slice(h * MEM_HEAD_DIM, (h + 1) * MEM_HEAD_DIM)
            q = cq_ref[rs, cs].astype(BF16)
            k = k_ref[b, :, cs].astype(BF16)
            v = v_ref[b, :, cs].astype(BF16)
            s = lax.dot_general(q, k, (((1,), (1,)), ((), ())), preferred_element_type=F32) * scale
            p = jnp.exp(s - jnp.max(s, axis=-1, keepdims=True))
            a = p * (1.0 / jnp.sum(p, axis=-1, keepdims=True))
            o = jnp.dot(a.astype(BF16), v, preferred_element_type=F32)
            xc_ref[rs, cs] = (o * _silu(cz_ref[rs, cs])).astype(xc_ref.dtype)


def _mem_attn(proj, mem_k, mem_v, nb, seq_len, nseq, tm):
    assert nseq == 1 or tm == seq_len
    nt = seq_len // tm
    rows = nseq * tm
    t = nb * seq_len
    cqb, czb = COL_CQ // MEM_WIDTH, COL_CZ // MEM_WIDTH
    return pl.pallas_call(
        functools.partial(_attn_kernel, nseq=nseq, tm=tm),
        grid=(nb // nseq, nt),
        in_specs=[pl.BlockSpec((rows, MEM_WIDTH), lambda i, j: (i * nt + j, cqb)),
                  pl.BlockSpec((rows, MEM_WIDTH), lambda i, j: (i * nt + j, czb)),
                  pl.BlockSpec((nseq, MEM_TOKENS, MEM_WIDTH), lambda i, j: (i, 0, 0)),
                  pl.BlockSpec((nseq, MEM_TOKENS, MEM_WIDTH), lambda i, j: (i, 0, 0))],
        out_specs=pl.BlockSpec((rows, MEM_WIDTH), lambda i, j: (i * nt + j, 0)),
        out_shape=jax.ShapeDtypeStruct((t, MEM_WIDTH), BF16),
        compiler_params=_cparams(("parallel", "arbitrary")),
        name="mem_attn",
    )(proj, proj, mem_k, mem_v)


def _attn_short_weights(cq_ref, k_ref, b, tm):
    scale = MEM_HEAD_DIM ** -0.5
    nr, nc = tm * HEADS, MEM_TOKENS * HEADS
    row_head = lax.broadcasted_iota(jnp.int32, (nr, nc), 0) // tm
    col_head = lax.broadcasted_iota(jnp.int32, (nr, nc), 1) % HEADS
    rs = slice(b * tm, (b + 1) * tm)
    kf = k_ref[b].reshape(nc, MEM_HEAD_DIM).astype(BF16)
    q = jnp.concatenate([cq_ref[rs, h * MEM_HEAD_DIM:(h + 1) * MEM_HEAD_DIM] for h in range(HEADS)], axis=0)
    s = lax.dot_general(q.astype(BF16), kf, (((1,), (1,)), ((), ())), preferred_element_type=F32) * scale
    s = jnp.where(row_head == col_head, s, -jnp.inf)
    p = jnp.exp(s - jnp.max(s, axis=-1, keepdims=True))
    return (p * (1.0 / jnp.sum(p, axis=-1, keepdims=True))).astype(BF16)


def _attn_short_output(a, cz_ref, v_ref, xc_ref, b, tm):
    rs = slice(b * tm, (b + 1) * tm)
    vf = v_ref[b].reshape(MEM_TOKENS * HEADS, MEM_HEAD_DIM).astype(BF16)
    o = jnp.dot(a, vf, preferred_element_type=F32)
    for h in range(HEADS):
        cs = slice(h * MEM_HEAD_DIM, (h + 1) * MEM_HEAD_DIM)
        xc_ref[rs, cs] = o[h * tm:(h + 1) * tm, :] * _silu(cz_ref[rs, cs])


def _inproj_attn_kernel(h_ref, w_ref, cq_ref, cz_ref, k_ref, v_ref, o_ref, xc_ref, *, nseq, seq_len):
    half = w_ref.shape[0] // 2
    nt = (((1,), (1,)), ((), ()))
    a = [_attn_short_weights(cq_ref, k_ref, b, seq_len) for b in range(nseq)]
    o_ref[:, :half] = lax.dot_general(h_ref[...], w_ref[:half, :], nt, preferred_element_type=F32)
    for b in range(nseq):
        _attn_short_output(a[b], cz_ref, v_ref, xc_ref, b, seq_len)
    o_ref[:, half:] = lax.dot_general(h_ref[...], w_ref[half:, :], nt, preferred_element_type=F32)


def _inproj_with_attn(h, w_bf, tm, tn, proj_s, mem_k, mem_v, seq_len):
    t, d = h.shape
    n_cols = w_bf.shape[0]
    nb = mem_k.shape[0]
    n_inner = t // tm
    n_steps = (n_cols // tn) * n_inner
    nseq = nb // n_steps
    rows = nseq * seq_len
    assert n_cols % tn == 0 and t % tm == 0 and nseq * n_steps == nb and rows % SUBLANES == 0
    cqb, czb = COL_CQ // MEM_WIDTH, COL_CZ // MEM_WIDTH
    kv_spec = pl.BlockSpec((nseq, MEM_TOKENS, HEADS, MEM_HEAD_DIM), lambda j, i: (j * n_inner + i, 0, 0, 0))
    return pl.pallas_call(
        functools.partial(_inproj_attn_kernel, nseq=nseq, seq_len=seq_len),
        grid=(n_cols // tn, n_inner),
        in_specs=[pl.BlockSpec((tm, d), lambda j, i: (i, 0)),
                  pl.BlockSpec((tn, d), lambda j, i: (j, 0)),
                  pl.BlockSpec((rows, MEM_WIDTH), lambda j, i: (j * n_inner + i, cqb)),
                  pl.BlockSpec((rows, MEM_WIDTH), lambda j, i: (j * n_inner + i, czb)),
                  kv_spec, kv_spec],
        out_specs=[pl.BlockSpec((tm, tn), lambda j, i: (i, j)),
                   pl.BlockSpec((rows, MEM_WIDTH), lambda j, i: (j * n_inner + i, 0))],
        out_shape=[jax.ShapeDtypeStruct((t, n_cols), F32),
                   jax.ShapeDtypeStruct((nb * seq_len, MEM_WIDTH), F32)],
        compiler_params=_cparams(("arbitrary", "arbitrary")),
        name="inproj_attn",
    )(h, w_bf, proj_s, proj_s, mem_k, mem_v)


def _mlstm_intra(q_bf, ks_bf, v_bf, ig_row, b_row, b_col, m0_col, mask):
    dlog = jnp.where(mask, b_col - b_row + ig_row, -jnp.inf)
    inter = b_col + m0_col
    m_col = jnp.maximum(inter, jnp.max(dlog, axis=1, keepdims=True))
    dw = jnp.exp(dlog - m_col)
    inter_w = jnp.exp(inter - m_col)
    s = lax.dot_general(q_bf, ks_bf, (((1,), (1,)), ((), ())), preferred_element_type=F32) * dw
    num = jnp.dot(s.astype(BF16), v_bf, preferred_element_type=F32)
    den = jnp.sum(s, axis=1, keepdims=True)
    return num, den, m_col, inter_w


def _mlstm_prompt_kernel(q_ref, k_ref, v_ref, g_ref, gt_ref, hb_ref, c_ref, n_ref, mcol_ref, m_scr):
    @pl.when(pl.program_id(1) == 0)
    def _():
        c_ref[...] = jnp.zeros_like(c_ref)
        n_ref[...] = jnp.zeros_like(n_ref)
        m_scr[...] = jnp.zeros_like(m_scr)

    cl = PROMPT_CHUNK
    row = lax.broadcasted_iota(jnp.int32, (cl, cl), 0)
    col = lax.broadcasted_iota(jnp.int32, (cl, cl), 1)
    mask = col <= row
    kscale = MLSTM_HEAD_DIM ** -0.5
    mcol_ref[...] = jnp.zeros_like(mcol_ref)
    for h in range(HEADS):
        cs = slice(h * MLSTM_HEAD_DIM, (h + 1) * MLSTM_HEAD_DIM)
        ig_col = g_ref[:, h:h + 1]
        b_col = g_ref[:, HEADS + h:HEADS + h + 1]
        ig_row = gt_ref[h:h + 1, :]
        b_row = gt_ref[HEADS + h:HEADS + h + 1, :]
        m0 = m_scr[h:h + 1, 0:1]
        qf = q_ref[:, cs]
        kf = k_ref[:, cs] * kscale
        vf = v_ref[:, cs]
        q_bf, ks_bf = qf.astype(BF16), kf.astype(BF16)
        num, den, m_col, inter_w = _mlstm_intra(q_bf, ks_bf, vf.astype(BF16), ig_row, b_row, b_col, m0, mask)
        c0 = c_ref[0, h]
        n0 = n_ref[0, h:h + 1, :]
        num = num + inter_w * lax.dot_general(q_bf, c0.astype(BF16), (((1,), (1,)), ((), ())),
                                              preferred_element_type=F32)
        den = den + inter_w * jnp.sum(qf * n0, axis=1, keepdims=True)
        hval = num * (1.0 / jnp.maximum(jnp.abs(den), jnp.exp(-m_col)))
        hb_ref[:, cs] = hval
        m_new = m_col[cl - 1:cl, :]
        b_last = b_col[cl - 1:cl, :]
        w_col = jnp.exp(b_last - b_col + ig_col - m_new)
        decay = jnp.exp(b_last + m0 - m_new)
        dc = lax.dot_general((w_col * vf).astype(BF16), ks_bf, (((0,), (0,)), ((), ())),
                             preferred_element_type=F32)
        c_ref[0, h] = decay * c0 + dc
        n_ref[0, h:h + 1, :] = decay * n0 + jnp.sum(w_col * kf, axis=0, keepdims=True)
        m_scr[h:h + 1, :] = jnp.broadcast_to(m_new, (1, LANES))
        mcol_ref[:, h:h + 1] = m_col


def _mlstm_prompt(proj, g, gt, nb, seq_len):
    cl = PROMPT_CHUNK
    nc = seq_len // cl
    t = nb * seq_len
    w = MLSTM_WIDTH

    def colspec(col0):
        return pl.BlockSpec((cl, w), lambda b, c: (b * nc + c, col0 // w))

    return pl.pallas_call(
        _mlstm_prompt_kernel,
        grid=(nb, nc),
        in_specs=[colspec(COL_Q), colspec(COL_K), colspec(COL_V),
                  pl.BlockSpec((cl, LANES), lambda b, c: (b * nc + c, 0)),
                  pl.BlockSpec((8, cl), lambda b, c: (0, b * nc + c))],
        out_specs=[pl.BlockSpec((cl, w), lambda b, c: (b * nc + c, 0)),
                   pl.BlockSpec((1, HEADS, MLSTM_HEAD_DIM, MLSTM_HEAD_DIM), lambda b, c: (b, 0, 0, 0)),
                   pl.BlockSpec((1, HEADS, MLSTM_HEAD_DIM), lambda b, c: (b, 0, 0)),
                   pl.BlockSpec((cl, LANES), lambda b, c: (b * nc + c, 0))],
        out_shape=[jax.ShapeDtypeStruct((t, w), F32),
                   jax.ShapeDtypeStruct((nb, HEADS, MLSTM_HEAD_DIM, MLSTM_HEAD_DIM), F32),
                   jax.ShapeDtypeStruct((nb, HEADS, MLSTM_HEAD_DIM), F32),
                   jax.ShapeDtypeStruct((t, LANES), F32)],
        scratch_shapes=[pltpu.VMEM((8, LANES), F32)],
        compiler_params=_cparams(("arbitrary", "arbitrary")),
        name="mlstm_prompt",
    )(proj, proj, proj, g, gt)


STAT_DEN, STAT_IW, STAT_M = 0, HEADS, 2 * HEADS


def _mlstm_short_intra_kernel(q_ref, k_ref, v_ref, g_ref, gt_ref, num_ref, stat_ref, *, seq_len):
    tl = LANES
    kscale = MLSTM_HEAD_DIM ** -0.5
    shift = seq_len.bit_length() - 1
    row = lax.broadcasted_iota(jnp.int32, (tl, tl), 0)
    col = lax.broadcasted_iota(jnp.int32, (tl, tl), 1)
    mask = (col <= row) & (lax.shift_right_logical(col, shift) == lax.shift_right_logical(row, shift))
    stat_ref[...] = jnp.zeros_like(stat_ref)
    for h in range(HEADS):
        cs = slice(h * MLSTM_HEAD_DIM, (h + 1) * MLSTM_HEAD_DIM)
        b_col = g_ref[:, HEADS + h:HEADS + h + 1]
        m0_col = g_ref[:, 2 * HEADS + h:2 * HEADS + h + 1]
        ig_row = gt_ref[h:h + 1, :]
        b_row = gt_ref[HEADS + h:HEADS + h + 1, :]
        q_bf = q_ref[:, cs].astype(BF16)
        ks_bf = (k_ref[:, cs] * kscale).astype(BF16)
        num, den, m_col, inter_w = _mlstm_intra(q_bf, ks_bf, v_ref[:, cs].astype(BF16),
                                                ig_row, b_row, b_col, m0_col, mask)
        num_ref[:, cs] = num
        stat_ref[:, STAT_DEN + h:STAT_DEN + h + 1] = den
        stat_ref[:, STAT_IW + h:STAT_IW + h + 1] = inter_w
        stat_ref[:, STAT_M + h:STAT_M + h + 1] = m_col


def _mlstm_short_intra(proj, g, gt, seq_len):
    t = proj.shape[0]
    tl = LANES
    w = MLSTM_WIDTH
    assert tl % seq_len == 0 and t % tl == 0

    def colspec(col0):
        return pl.BlockSpec((tl, w), lambda i: (i, col0 // w))

    return pl.pallas_call(
        functools.partial(_mlstm_short_intra_kernel, seq_len=seq_len),
        grid=(t // tl,),
        in_specs=[colspec(COL_Q), colspec(COL_K), colspec(COL_V),
                  pl.BlockSpec((tl, LANES), lambda i: (i, 0)),
                  pl.BlockSpec((8, tl), lambda i: (0, i))],
        out_specs=[pl.BlockSpec((tl, w), lambda i: (i, 0)),
                   pl.BlockSpec((tl, LANES), lambda i: (i, 0))],
        out_shape=[jax.ShapeDtypeStruct((t, w), F32),
                   jax.ShapeDtypeStruct((t, LANES), F32)],
        compiler_params=_cparams(("parallel",)),
        name="mlstm_short_intra",
    )(proj, proj, proj, g, gt)


def _mlstm_state_head(h, q_ref, k_ref, v_ref, num_ref, g_ref, stat_ref, c0_ref, n0_ref, hb_ref, c_ref, n_ref):
    kscale = MLSTM_HEAD_DIM ** -0.5
    seq_len = q_ref.shape[0]
    last = slice(seq_len - 1, seq_len)
    cs = slice(h * MLSTM_HEAD_DIM, (h + 1) * MLSTM_HEAD_DIM)
    qf = q_ref[:, cs]
    kf = k_ref[:, cs] * kscale
    vf = v_ref[:, cs]
    ig_col = g_ref[:, h:h + 1]
    b_col = g_ref[:, HEADS + h:HEADS + h + 1]
    b_last = g_ref[last, HEADS + h:HEADS + h + 1]
    m0 = g_ref[last, 2 * HEADS + h:2 * HEADS + h + 1]
    den_intra = stat_ref[:, STAT_DEN + h:STAT_DEN + h + 1]
    inter_w = stat_ref[:, STAT_IW + h:STAT_IW + h + 1]
    m_col = stat_ref[:, STAT_M + h:STAT_M + h + 1]
    m_new = stat_ref[last, STAT_M + h:STAT_M + h + 1]
    c0 = c0_ref[0, h]
    n0 = n0_ref[0, h:h + 1, :]
    num = num_ref[:, cs] + inter_w * lax.dot_general(
        qf.astype(BF16), c0.astype(BF16), (((1,), (1,)), ((), ())), preferred_element_type=F32)
    den = den_intra + inter_w * jnp.sum(qf * n0, axis=1, keepdims=True)
    hval = num * (1.0 / jnp.maximum(jnp.abs(den), jnp.exp(-m_col)))
    hb_ref[:, cs] = hval
    w_col = jnp.exp(b_last - b_col + ig_col - m_new)
    decay = jnp.exp(b_last + m0 - m_new)
    dc = lax.dot_general((w_col * vf).astype(BF16), kf.astype(BF16), (((0,), (0,)), ((), ())),
                         preferred_element_type=F32)
    c_ref[0, h] = decay * c0 + dc
    n_ref[0, h:h + 1, :] = decay * n0 + jnp.sum(w_col * kf, axis=0, keepdims=True)


def _inproj_state_kernel(h_ref, w_ref, *refs):
    state_refs, o_ref, out_state_refs = refs[:8], refs[8], refs[9:]
    half = w_ref.shape[0] // 2
    nt = (((1,), (1,)), ((), ()))
    o_ref[:, :half] = lax.dot_general(h_ref[...], w_ref[:half, :], nt, preferred_element_type=F32)
    for h in range(HEADS):
        _mlstm_state_head(h, *state_refs, *out_state_refs)
    o_ref[:, half:] = lax.dot_general(h_ref[...], w_ref[half:, :], nt, preferred_element_type=F32)


def _inproj_with_state(h, w_bf, tm, tn, proj_s, num_s, g_s, stat_s, c0, n0, seq_len):
    t, d = h.shape
    n_cols = w_bf.shape[0]
    n_seq = c0.shape[0]
    n_inner = t // tm
    assert n_cols % tn == 0 and t % tm == 0 and (n_cols // tn) * n_inner == n_seq
    w = MLSTM_WIDTH
    hd = MLSTM_HEAD_DIM

    def seq(j, i):
        return j * n_inner + i

    def colspec(col0):
        return pl.BlockSpec((seq_len, w), lambda j, i: (seq(j, i), col0 // w))

    rowspec = pl.BlockSpec((seq_len, w), lambda j, i: (seq(j, i), 0))
    lanespec = pl.BlockSpec((seq_len, LANES), lambda j, i: (seq(j, i), 0))
    state4 = pl.BlockSpec((1, HEADS, hd, hd), lambda j, i: (seq(j, i), 0, 0, 0))
    state3 = pl.BlockSpec((1, HEADS, hd), lambda j, i: (seq(j, i), 0, 0))
    return pl.pallas_call(
        _inproj_state_kernel,
        grid=(n_cols // tn, n_inner),
        in_specs=[pl.BlockSpec((tm, d), lambda j, i: (i, 0)),
                  pl.BlockSpec((tn, d), lambda j, i: (j, 0)),
                  colspec(COL_Q), colspec(COL_K), colspec(COL_V),
                  rowspec, lanespec, lanespec, state4, state3],
        out_specs=[pl.BlockSpec((tm, tn), lambda j, i: (i, j)), rowspec, state4, state3],
        out_shape=[jax.ShapeDtypeStruct((t, n_cols), F32),
                   jax.ShapeDtypeStruct((n_seq * seq_len, w), F32),
                   jax.ShapeDtypeStruct((n_seq, HEADS, hd, hd), F32),
                   jax.ShapeDtypeStruct((n_seq, HEADS, hd), F32)],
        compiler_params=_cparams(("arbitrary", "arbitrary")),
        name="inproj_state",
    )(h, w_bf, proj_s, proj_s, proj_s, num_s, g_s, stat_s, c0, n0)


def _resident(shape):
    return pl.BlockSpec(shape, lambda i: (0, 0), pipeline_mode=pl.Buffered(1))


def _merge_kernel(xa_ref, hb_ref, o_ref, z_ref, xc_ref, ga_ref, gb_ref, gc_ref, wa_ref, wb_ref, wc_ref, m_ref):
    ya = jnp.dot(xa_ref[...].astype(BF16), wa_ref[...], preferred_element_type=F32)
    merged = _sigmoid(ga_ref[...]) * ya
    xb = _sigmoid(o_ref[...]) * hb_ref[...] * _silu(z_ref[...])
    yb = jnp.dot(xb.astype(BF16), wb_ref[...], preferred_element_type=F32)
    merged = merged + _sigmoid(gb_ref[...]) * yb
    yc = jnp.dot(xc_ref[...].astype(BF16), wc_ref[...], preferred_element_type=F32)
    merged = merged + _sigmoid(gc_ref[...]) * yc
    m_ref[...] = merged.astype(m_ref.dtype)


def _out_kernel(m_ref, x_ref, gpost_ref, wo_ref, y_ref):
    out = jnp.dot(m_ref[...], wo_ref[...], preferred_element_type=F32)
    ms = jnp.mean(out * out, axis=-1, keepdims=True)
    y_ref[...] = x_ref[...] + out * lax.rsqrt(ms + EPS) * gpost_ref[...]


def _final(xa, hb, xc, proj_a, proj_b, x2d, gpost_row, wa, wb, wc, wo, tm_merge, tm_out):
    t, d = x2d.shape

    def rowspec(tm, width, colblk=0):
        return pl.BlockSpec((tm, width), lambda i: (i, colblk))

    tm = tm_merge
    merged = pl.pallas_call(
        _merge_kernel,
        grid=(t // tm,),
        in_specs=[rowspec(tm, POOL_WIDTH), rowspec(tm, MLSTM_WIDTH),
                  rowspec(tm, MLSTM_WIDTH, COL_O // MLSTM_WIDTH), rowspec(tm, MLSTM_WIDTH, COL_Z // MLSTM_WIDTH),
                  rowspec(tm, MEM_WIDTH),
                  rowspec(tm, d, COL_GA // d), rowspec(tm, d, COL_GB // d), rowspec(tm, d, COL_GC // d),
                  _resident(wa.shape), _resident(wb.shape), _resident(wc.shape)],
        out_specs=rowspec(tm, d),
        out_shape=jax.ShapeDtypeStruct((t, d), BF16),
        compiler_params=_cparams(("parallel",)),
        name="merge",
    )(xa, hb, proj_a, proj_a, xc, proj_b, proj_b, proj_b, wa, wb, wc)
    tm = tm_out
    return pl.pallas_call(
        _out_kernel,
        grid=(t // tm,),
        in_specs=[rowspec(tm, d), rowspec(tm, d), _resident((1, d)), _resident(wo.shape)],
        out_specs=rowspec(tm, d),
        out_shape=jax.ShapeDtypeStruct((t, d), F32),
        compiler_params=_cparams(("parallel",)),
        name="out_proj",
    )(merged, x2d, gpost_row, wo)


B_ROW0 = GATE_COL0 + N_GATE_COLS


def _pool_and_merge(x2d, nb, seq_len, proj_a, proj_b, xb, xc, pool_buf, wts, start):
    if seq_len >= 512:
        xa, new_buf = _pool_branch(proj_a, pool_buf, wts["w_pool_grp"], wts["pool_scale"], nb, seq_len,
                                   nseq=1, tm=512, start=start)
    else:
        xa, new_buf = _pool_branch(proj_a, pool_buf, wts["w_pool_grp"], wts["pool_scale"], nb, seq_len,
                                   nseq=32, tm=seq_len, start=start)
    y = _final(xa, xb, xc, proj_a, proj_b, x2d, wts["g_post"], wts["w_br_pool"], wts["w_br_mlstm"],
               wts["w_br_mem"], wts["w_out"], tm_merge=256, tm_out=512)
    return y.reshape(nb, seq_len, x2d.shape[1]), new_buf


def kernel(x_prompt, x_sample, state_pool, state_mlstm_C, state_mlstm_n, state_mlstm_m, cache_mem_k,
           cache_mem_v, mem_prompt, g_pre, g_post, w_in, b_mlstm_i, b_mlstm_f, w_pool_grp, pool_scale,
           g_mem, w_mem_kv, w_br_pool, w_br_mlstm, w_br_mem, w_out):
    nbp = x_prompt.shape[0]
    assert w_in.shape[0] == 1, "single-layer problem: the kernels index layer 0 of the stacked weights"
    l = 0
    wts = {
        "g_pre": g_pre[l][None, :],
        "g_post": g_post[l][None, :],
        "w_in": jnp.transpose(w_in[l]),
        "gate_bias": jnp.pad(jnp.concatenate([b_mlstm_i[l], b_mlstm_f[l]]), (0, LANES - N_GATE_COLS))[None, :],
        "w_pool_grp": w_pool_grp[l].astype(BF16),
        "pool_scale": pool_scale[l][None, :],
        "w_br_pool": w_br_pool[l].astype(BF16),
        "w_br_mlstm": w_br_mlstm[l].astype(BF16),
        "w_br_mem": w_br_mem[l].astype(BF16),
        "w_out": w_out[l].astype(BF16),
    }
    mk2d, mv2d = _mem_kv(mem_prompt.reshape(nbp * MEM_TOKENS, D_MODEL), g_mem[l][None, :],
                         w_mem_kv[l].astype(BF16), tm=256)
    mk3, mv3 = mk2d.reshape(nbp, MEM_TOKENS, MEM_WIDTH), mv2d.reshape(nbp, MEM_TOKENS, MEM_WIDTH)
    w_t = wts["w_in"]

    nbs, ls, d = x_sample.shape
    ts = nbs * ls
    xs2d = x_sample.reshape(ts, d)
    m0_rows = jnp.pad(jnp.repeat(state_mlstm_m[l], ls, axis=0), ((0, 0), (2 * HEADS, LANES - 3 * HEADS)))
    hs, gs, gts = _norm_gate_prep(xs2d, wts["g_pre"], w_t, wts["gate_bias"], m0_rows, ls)
    proj_a_s, w_a_bf = _inproj(hs, w_t, 0, N_PROJ_A, tm=ts, tn=1024)
    proj_b_s, w_b_bf = _inproj(hs, w_t, B_ROW0, N_PROJ_B, tm=ts, tn=1024)
    num_s, stat_s = _mlstm_short_intra(proj_a_s, gs, gts, ls)
    m_s = stat_s.reshape(nbs, ls, LANES)[:, ls - 1, STAT_M:STAT_M + HEADS]

    _, lp, _ = x_prompt.shape
    tp = nbp * lp
    xp2d = x_prompt.reshape(tp, d)
    hp, gp, gtp = _norm_gate_prep(xp2d, wts["g_pre"], w_t, wts["gate_bias"], jnp.zeros((tp, LANES), F32),
                                  PROMPT_CHUNK)
    proj_a_p, xb_s, c_s, n_s = _inproj_with_state(hp, w_a_bf, 256, 3072, proj_a_s, num_s, gs, stat_s,
                                                  state_mlstm_C[l], state_mlstm_n[l], ls)
    proj_b_p, xc_s = _inproj_with_attn(hp, w_b_bf, 1024, 1024, proj_b_s, cache_mem_k[l], cache_mem_v[l], ls)
    xb_p, c_p, n_p, mcol_p = _mlstm_prompt(proj_a_p, gp, gtp, nbp, lp)
    m_p = mcol_p.reshape(nbp, lp, LANES)[:, lp - 1, :HEADS]
    xc_p = _mem_attn(proj_b_p, mk3, mv3, nbp, lp, nseq=1, tm=512)

    yp, pool_p = _pool_and_merge(xp2d, nbp, lp, proj_a_p, proj_b_p, xb_p, xc_p,
                                 jnp.zeros((nbp, POOL_BUF, POOL_WIDTH), F32), wts, 0)
    ys, pool_s = _pool_and_merge(xs2d, nbs, ls, proj_a_s, proj_b_s, xb_s, xc_s, state_pool[l], wts, PAST_LEN)
    kv_shape = (1, nbp, MEM_TOKENS, HEADS, MEM_HEAD_DIM)
    return (yp, ys, pool_p[None], c_p[None], n_p[None], m_p[None],
            mk2d.reshape(kv_shape), mv2d.reshape(kv_shape),
            pool_s[None], c_s[None], n_s[None], m_s[None])
```

```python
import functools

import jax
import jax.numpy as jnp
from jax import lax
from jax.experimental import pallas as pl
from jax.experimental.pallas import tpu as pltpu

F32 = jnp.float32
BF16 = jnp.bfloat16

D_MODEL = 2048
PAST_LEN = 16384
POOL_WINDOWS = (2, 4, 8, 16)
POOL_GROUP_WIDTH = 256
POOL_WIDTH = 1024
POOL_BUF = 15
HEADS = 4
MLSTM_HEAD_DIM = 512
MLSTM_WIDTH = 2048
PROMPT_CHUNK = 256
MEM_TOKENS = 256
MEM_HEAD_DIM = 256
MEM_WIDTH = 1024
EPS = 1e-6
N_GATE_COLS = 2 * HEADS
GATE_COL0 = 2 * POOL_WIDTH + 5 * MLSTM_WIDTH
LANES = 128
SUBLANES = 8
COL_PV, COL_PZ = 0, 1024
COL_Q, COL_K, COL_V, COL_O, COL_Z = 2048, 4096, 6144, 8192, 10240
COL_CQ, COL_CZ = 0, 1024
COL_GA, COL_GB, COL_GC = 2048, 4096, 6144
N_PROJ_A = GATE_COL0
N_PROJ_B = 2 * MEM_WIDTH + 3 * D_MODEL
VMEM_LIMIT = 56 * 1024 * 1024


def _cparams(sem):
    return pltpu.CompilerParams(dimension_semantics=sem, vmem_limit_bytes=VMEM_LIMIT)


def _sigmoid(x):
    return 0.5 * jnp.tanh(0.5 * x) + 0.5


def _silu(x):
    return x * _sigmoid(x)


def _inproj_kernel(h_ref, w_ref, o_ref, wbf_ref):
    @pl.when(pl.program_id(1) == 0)
    def _():
        wbf_ref[...] = w_ref[...].astype(BF16)

    o_ref[...] = lax.dot_general(h_ref[...], wbf_ref[...], (((1,), (1,)), ((), ())), preferred_element_type=F32)


def _inproj(h, w_t, row0, n_cols, tm, tn):
    t, d = h.shape
    assert row0 % SUBLANES == 0 and n_cols % tn == 0 and t % tm == 0 and row0 + n_cols <= w_t.shape[0]
    return pl.pallas_call(
        _inproj_kernel,
        grid=(n_cols // tn, t // tm),
        in_specs=[pl.BlockSpec((tm, d), lambda j, i: (i, 0)),
                  pl.BlockSpec((pl.Element(tn), pl.Element(d)),
                               lambda j, i: (pl.multiple_of(row0 + j * tn, SUBLANES), 0))],
        out_specs=[pl.BlockSpec((tm, tn), lambda j, i: (i, j)),
                   pl.BlockSpec((tn, d), lambda j, i: (j, 0))],
        out_shape=[jax.ShapeDtypeStruct((t, n_cols), F32),
                   jax.ShapeDtypeStruct((n_cols, d), BF16)],
        compiler_params=_cparams(("arbitrary", "arbitrary")),
        name="inproj",
    )(h, w_t)


GATE_TILE = 512


def _gate_kernel(x_ref, gpre_ref, wg_ref, bias_ref, m0_ref, h_ref, g_ref, gt_ref, *, seg, ct):
    x = x_ref[...]
    ms = jnp.mean(x * x, axis=-1, keepdims=True)
    h_ref[...] = (x * lax.rsqrt(ms + EPS) * gpre_ref[...]).astype(h_ref.dtype)
    row = lax.broadcasted_iota(jnp.int32, (ct, ct), 0)
    col = lax.broadcasted_iota(jnp.int32, (ct, ct), 1)
    same = col <= row
    if seg < ct:
        shift = seg.bit_length() - 1
        same = same & (lax.shift_right_logical(col, shift) == lax.shift_right_logical(row, shift))
    same = same.astype(F32)
    lane = lax.broadcasted_iota(jnp.int32, (ct, LANES), 1)
    wg = wg_ref[...].astype(BF16)
    for r in range(GATE_TILE // ct):
        rs = slice(r * ct, (r + 1) * ct)
        x = lax.dot_general(h_ref[rs, :], wg, (((1,), (1,)), ((), ())), preferred_element_type=F32) + bias_ref[...]
        lf = jnp.minimum(x, 0.0) - jnp.log1p(jnp.exp(-jnp.abs(x)))
        cums = jnp.dot(same, lf, precision=lax.Precision.HIGHEST, preferred_element_type=F32)
        g = jnp.where(lane < HEADS, x, jnp.where(lane < 2 * HEADS, cums, m0_ref[rs, :]))
        g_ref[rs, :] = g
        gt_ref[:, rs] = g.T[:8, :]


def _norm_gate_prep(x2d, gpre_row, w_t, bias_row, m0_rows, seg):
    t, d = x2d.shape
    tg = GATE_TILE
    ct = max(seg, LANES)
    assert GATE_COL0 % LANES == 0 and tg % ct == 0 and ct % seg == 0 and seg & (seg - 1) == 0
    return pl.pallas_call(
        functools.partial(_gate_kernel, seg=seg, ct=ct),
        grid=(t // tg,),
        in_specs=[pl.BlockSpec((tg, d), lambda i: (i, 0)),
                  pl.BlockSpec((1, d), lambda i: (0, 0)),
                  pl.BlockSpec((LANES, d), lambda i: (GATE_COL0 // LANES, 0)),
                  pl.BlockSpec((1, LANES), lambda i: (0, 0)),
                  pl.BlockSpec((tg, LANES), lambda i: (i, 0))],
        out_specs=[pl.BlockSpec((tg, d), lambda i: (i, 0)),
                   pl.BlockSpec((tg, LANES), lambda i: (i, 0)),
                   pl.BlockSpec((8, tg), lambda i: (0, i))],
        out_shape=[jax.ShapeDtypeStruct((t, d), BF16),
                   jax.ShapeDtypeStruct((t, LANES), F32),
                   jax.ShapeDtypeStruct((8, t), F32)],
        compiler_params=_cparams(("parallel",)),
        name="norm_gate_prep",
    )(x2d, gpre_row, w_t, bias_row, m0_rows)


def _pool_kernel(pv_ref, pz_ref, buf_ref, wg_ref, scale_ref, xa_ref, nbuf_ref, s_ref, p_ref,
                 *, nseq, tm, nt, start):
    j = pl.program_id(1)

    def one_seq(b, r0):
        if nt > 1:
            @pl.when(j == 0)
            def _():
                s_ref[1:16, :] = buf_ref[b]
        else:
            s_ref[1:16, :] = buf_ref[b]
        s_ref[16:16 + tm, :] = pv_ref[pl.ds(r0, tm), :]
        pos = start + j * tm + lax.broadcasted_iota(jnp.int32, (tm, 1), 0)
        for g, w in enumerate(POOL_WINDOWS):
            cs = slice(g * POOL_GROUP_WIDTH, (g + 1) * POOL_GROUP_WIDTH)
            acc = s_ref[16:16 + tm, cs]
            for i in range(1, w):
                acc = acc + s_ref[16 - i:16 - i + tm, cs]
            inv_cnt = 1.0 / jnp.minimum(w, pos + 1).astype(F32)
            p_ref[pl.ds(r0, tm), cs] = acc * inv_cnt - s_ref[16:16 + tm, cs]
        tail = s_ref[tm + 1:tm + 16, :]
        nbuf_ref[b] = tail
        if nt > 1:
            s_ref[1:16, :] = tail

    if nseq == 1:
        one_seq(0, 0)
    else:
        def body(b, carry):
            one_seq(b, pl.multiple_of(b * tm, 8))
            return carry
        lax.fori_loop(0, nseq, body, 0)

    for g in range(len(POOL_WINDOWS)):
        cs = slice(g * POOL_GROUP_WIDTH, (g + 1) * POOL_GROUP_WIDTH)
        pa = jnp.dot(p_ref[:, cs].astype(BF16), wg_ref[g], preferred_element_type=F32) * scale_ref[:, cs]
        xa_ref[:, cs] = (pa * _silu(pz_ref[:, cs])).astype(xa_ref.dtype)


def _pool_branch(proj, buf, w_grp, scale_row, nb, seq_len, nseq, tm, start):
    assert nseq == 1 or tm == seq_len
    nt = seq_len // tm
    rows = nseq * tm
    t = nb * seq_len
    pvb, pzb = COL_PV // POOL_WIDTH, COL_PZ // POOL_WIDTH
    return pl.pallas_call(
        functools.partial(_pool_kernel, nseq=nseq, tm=tm, nt=nt, start=start),
        grid=(nb // nseq, nt),
        in_specs=[pl.BlockSpec((rows, POOL_WIDTH), lambda i, j: (i * nt + j, pvb)),
                  pl.BlockSpec((rows, POOL_WIDTH), lambda i, j: (i * nt + j, pzb)),
                  pl.BlockSpec((nseq, POOL_BUF, POOL_WIDTH), lambda i, j: (i, 0, 0)),
                  pl.BlockSpec((4, POOL_GROUP_WIDTH, POOL_GROUP_WIDTH), lambda i, j: (0, 0, 0)),
                  pl.BlockSpec((1, POOL_WIDTH), lambda i, j: (0, 0))],
        out_specs=[pl.BlockSpec((rows, POOL_WIDTH), lambda i, j: (i * nt + j, 0)),
                   pl.BlockSpec((nseq, POOL_BUF, POOL_WIDTH), lambda i, j: (i, 0, 0))],
        out_shape=[jax.ShapeDtypeStruct((t, POOL_WIDTH), BF16),
                   jax.ShapeDtypeStruct((nb, POOL_BUF, POOL_WIDTH), F32)],
        scratch_shapes=[pltpu.VMEM((16 + tm, POOL_WIDTH), F32),
                        pltpu.VMEM((rows, POOL_WIDTH), F32)],
        compiler_params=_cparams(("arbitrary", "arbitrary")),
        name="pool_branch",
    )(proj, proj, buf, w_grp, scale_row)


def _memkv_kernel(m_ref, g_ref, w_ref, k_ref, v_ref):
    x = m_ref[...]
    ms = jnp.mean(x * x, axis=-1, keepdims=True)
    h = (x * lax.rsqrt(ms + EPS) * g_ref[...]).astype(BF16)
    kv = jnp.dot(h, w_ref[...], preferred_element_type=F32)
    k_ref[...] = kv[:, :MEM_WIDTH]
    v_ref[...] = kv[:, MEM_WIDTH:]


def _mem_kv(mem2d, g_row, w_bf, tm):
    t, d = mem2d.shape
    return pl.pallas_call(
        _memkv_kernel,
        grid=(t // tm,),
        in_specs=[pl.BlockSpec((tm, d), lambda i: (i, 0)),
                  pl.BlockSpec((1, d), lambda i: (0, 0)),
                  pl.BlockSpec((d, 2 * MEM_WIDTH), lambda i: (0, 0))],
        out_specs=[pl.BlockSpec((tm, MEM_WIDTH), lambda i: (i, 0)),
                   pl.BlockSpec((tm, MEM_WIDTH), lambda i: (i, 0))],
        out_shape=[jax.ShapeDtypeStruct((t, MEM_WIDTH), F32),
                   jax.ShapeDtypeStruct((t, MEM_WIDTH), F32)],
        compiler_params=_cparams(("parallel",)),
        name="mem_kv",
    )(mem2d, g_row, w_bf)


def _attn_kernel(cq_ref, cz_ref, k_ref, v_ref, xc_ref, *, nseq, tm):
    scale = MEM_HEAD_DIM ** -0.5
    for b in range(nseq):
        rs = slice(b * tm, (b + 1) * tm)
        for h in range(HEADS):
            cs = slice(h * MEM_HEAD_DIM, (h + 1) * MEM_HEAD_DIM)
            q = cq_ref[rs, cs].astype(BF16)
            k = k_ref[b, :, cs].astype(BF16)
            v = v_ref[b, :, cs].astype(BF16)
            s = lax.dot_general(q, k, (((1,), (1,)), ((), ())), preferred_element_type=F32) * scale
            p = jnp.exp(s - jnp.max(s, axis=-1, keepdims=True))
            a = p * (1.0 / jnp.sum(p, axis=-1, keepdims=True))
            o = jnp.dot(a.astype(BF16), v, preferred_element_type=F32)
            xc_ref[rs, cs] = (o * _silu(cz_ref[rs, cs])).astype(xc_ref.dtype)


def _mem_attn(proj, mem_k, mem_v, nb, seq_len, nseq, tm):
    assert nseq == 1 or tm == seq_len
    nt = seq_len // tm
    rows = nseq * tm
    t = nb * seq_len
    cqb, czb = COL_CQ // MEM_WIDTH, COL_CZ // MEM_WIDTH
    return pl.pallas_call(
        functools.partial(_attn_kernel, nseq=nseq, tm=tm),
        grid=(nb // nseq, nt),
        in_specs=[pl.BlockSpec((rows, MEM_WIDTH), lambda i, j: (i * nt + j, cqb)),
                  pl.BlockSpec((rows, MEM_WIDTH), lambda i, j: (i * nt + j, czb)),
                  pl.BlockSpec((nseq, MEM_TOKENS, MEM_WIDTH), lambda i, j: (i, 0, 0)),
                  pl.BlockSpec((nseq, MEM_TOKENS, MEM_WIDTH), lambda i, j: (i, 0, 0))],
        out_specs=pl.BlockSpec((rows, MEM_WIDTH), lambda i, j: (i * nt + j, 0)),
        out_shape=jax.ShapeDtypeStruct((t, MEM_WIDTH), BF16),
        compiler_params=_cparams(("parallel", "arbitrary")),
        name="mem_attn",
    )(proj, proj, mem_k, mem_v)


def _attn_short_weights(cq_ref, k_ref, b, tm):
    scale = MEM_HEAD_DIM ** -0.5
    nr, nc = tm * HEADS, MEM_TOKENS * HEADS
    row_head = lax.broadcasted_iota(jnp.int32, (nr, nc), 0) // tm
    col_head = lax.broadcasted_iota(jnp.int32, (nr, nc), 1) % HEADS
    rs = slice(b * tm, (b + 1) * tm)
    kf = k_ref[b].reshape(nc, MEM_HEAD_DIM).astype(BF16)
    q = jnp.concatenate([cq_ref[rs, h * MEM_HEAD_DIM:(h + 1) * MEM_HEAD_DIM] for h in range(HEADS)], axis=0)
    s = lax.dot_general(q.astype(BF16), kf, (((1,), (1,)), ((), ())), preferred_element_type=F32) * scale
    s = jnp.where(row_head == col_head, s, -jnp.inf)
    p = jnp.exp(s - jnp.max(s, axis=-1, keepdims=True))
    return (p * (1.0 / jnp.sum(p, axis=-1, keepdims=True))).astype(BF16)


def _attn_short_output(a, cz_ref, v_ref, xc_ref, b, tm):
    rs = slice(b * tm, (b + 1) * tm)
    vf = v_ref[b].reshape(MEM_TOKENS * HEADS, MEM_HEAD_DIM).astype(BF16)
    o = jnp.dot(a, vf, preferred_element_type=F32)
    for h in range(HEADS):
        cs = slice(h * MEM_HEAD_DIM, (h + 1) * MEM_HEAD_DIM)
        xc_ref[rs, cs] = o[h * tm:(h + 1) * tm, :] * _silu(cz_ref[rs, cs])


def _inproj_attn_kernel(h_ref, w_ref, cq_ref, cz_ref, k_ref, v_ref, o_ref, xc_ref, *, nseq, seq_len):
    half = w_ref.shape[0] // 2
    nt = (((1,), (1,)), ((), ()))
    a = [_attn_short_weights(cq_ref, k_ref, b, seq_len) for b in range(nseq)]
    o_ref[:, :half] = lax.dot_general(h_ref[...], w_ref[:half, :], nt, preferred_element_type=F32)
    for b in range(nseq):
        _attn_short_output(a[b], cz_ref, v_ref, xc_ref, b, seq_len)
    o_ref[:, half:] = lax.dot_general(h_ref[...], w_ref[half:, :], nt, preferred_element_type=F32)


def _inproj_with_attn(h, w_bf, tm, tn, proj_s, mem_k, mem_v, seq_len):
    t, d = h.shape
    n_cols = w_bf.shape[0]
    nb = mem_k.shape[0]
    n_inner = t // tm
    n_steps = (n_cols // tn) * n_inner
    nseq = nb // n_steps
    rows = nseq * seq_len
    assert n_cols % tn == 0 and t % tm == 0 and nseq * n_steps == nb and rows % SUBLANES == 0
    cqb, czb = COL_CQ // MEM_WIDTH, COL_CZ // MEM_WIDTH
    kv_spec = pl.BlockSpec((nseq, MEM_TOKENS, HEADS, MEM_HEAD_DIM), lambda j, i: (j * n_inner + i, 0, 0, 0))
    return pl.pallas_call(
        functools.partial(_inproj_attn_kernel, nseq=nseq, seq_len=seq_len),
        grid=(n_cols // tn, n_inner),
        in_specs=[pl.BlockSpec((tm, d), lambda j, i: (i, 0)),
                  pl.BlockSpec((tn, d), lambda j, i: (j, 0)),
                  pl.BlockSpec((rows, MEM_WIDTH), lambda j, i: (j * n_inner + i, cqb)),
                  pl.BlockSpec((rows, MEM_WIDTH), lambda j, i: (j * n_inner + i, czb)),
                  kv_spec, kv_spec],
        out_specs=[pl.BlockSpec((tm, tn), lambda j, i: (i, j)),
                   pl.BlockSpec((rows, MEM_WIDTH), lambda j, i: (j * n_inner + i, 0))],
        out_shape=[jax.ShapeDtypeStruct((t, n_cols), F32),
                   jax.ShapeDtypeStruct((nb * seq_len, MEM_WIDTH), F32)],
        compiler_params=_cparams(("arbitrary", "arbitrary")),
        name="inproj_attn",
    )(h, w_bf, proj_s, proj_s, mem_k, mem_v)


def _mlstm_intra(q_bf, ks_bf, v_bf, ig_row, b_row, b_col, m0_col, mask):
    dlog = jnp.where(mask, b_col - b_row + ig_row, -jnp.inf)
    inter = b_col + m0_col
    m_col = jnp.maximum(inter, jnp.max(dlog, axis=1, keepdims=True))
    dw = jnp.exp(dlog - m_col)
    inter_w = jnp.exp(inter - m_col)
    s = lax.dot_general(q_bf, ks_bf, (((1,), (1,)), ((), ())), preferred_element_type=F32) * dw
    num = jnp.dot(s.astype(BF16), v_bf, preferred_element_type=F32)
    den = jnp.sum(s, axis=1, keepdims=True)
    return num, den, m_col, inter_w


def _mlstm_prompt_kernel(q_ref, k_ref, v_ref, g_ref, gt_ref, hb_ref, c_ref, n_ref, mcol_ref, m_scr):
    @pl.when(pl.program_id(1) == 0)
    def _():
        c_ref[...] = jnp.zeros_like(c_ref)
        n_ref[...] = jnp.zeros_like(n_ref)
        m_scr[...] = jnp.zeros_like(m_scr)

    cl = PROMPT_CHUNK
    row = lax.broadcasted_iota(jnp.int32, (cl, cl), 0)
    col = lax.broadcasted_iota(jnp.int32, (cl, cl), 1)
    mask = col <= row
    kscale = MLSTM_HEAD_DIM ** -0.5
    mcol_ref[...] = jnp.zeros_like(mcol_ref)
    for h in range(HEADS):
        cs = slice(h * MLSTM_HEAD_DIM, (h + 1) * MLSTM_HEAD_DIM)
        ig_col = g_ref[:, h:h + 1]
        b_col = g_ref[:, HEADS + h:HEADS + h + 1]
        ig_row = gt_ref[h:h + 1, :]
        b_row = gt_ref[HEADS + h:HEADS + h + 1, :]
        m0 = m_scr[h:h + 1, 0:1]
        qf = q_ref[:, cs]
        kf = k_ref[:, cs] * kscale
        vf = v_ref[:, cs]
        q_bf, ks_bf = qf.astype(BF16), kf.astype(BF16)
        num, den, m_col, inter_w = _mlstm_intra(q_bf, ks_bf, vf.astype(BF16), ig_row, b_row, b_col, m0, mask)
        c0 = c_ref[0, h]
        n0 = n_ref[0, h:h + 1, :]
        num = num + inter_w * lax.dot_general(q_bf, c0.astype(BF16), (((1,), (1,)), ((), ())),
                                              preferred_element_type=F32)
        den = den + inter_w * jnp.sum(qf * n0, axis=1, keepdims=True)
        hval = num * (1.0 / jnp.maximum(jnp.abs(den), jnp.exp(-m_col)))
        hb_ref[:, cs] = hval
        m_new = m_col[cl - 1:cl, :]
        b_last = b_col[cl - 1:cl, :]
        w_col = jnp.exp(b_last - b_col + ig_col - m_new)
        decay = jnp.exp(b_last + m0 - m_new)
        dc = lax.dot_general((w_col * vf).astype(BF16), ks_bf, (((0,), (0,)), ((), ())),
                             preferred_element_type=F32)
        c_ref[0, h] = decay * c0 + dc
        n_ref[0, h:h + 1, :] = decay * n0 + jnp.sum(w_col * kf, axis=0, keepdims=True)
        m_scr[h:h + 1, :] = jnp.broadcast_to(m_new, (1, LANES))
        mcol_ref[:, h:h + 1] = m_col


def _mlstm_prompt(proj, g, gt, nb, seq_len):
    cl = PROMPT_CHUNK
    nc = seq_len // cl
    t = nb * seq_len
    w = MLSTM_WIDTH

    def colspec(col0):
        return pl.BlockSpec((cl, w), lambda b, c: (b * nc + c, col0 // w))

    return pl.pallas_call(
        _mlstm_prompt_kernel,
        grid=(nb, nc),
        in_specs=[colspec(COL_Q), colspec(COL_K), colspec(COL_V),
                  pl.BlockSpec((cl, LANES), lambda b, c: (b * nc + c, 0)),
                  pl.BlockSpec((8, cl), lambda b, c: (0, b * nc + c))],
        out_specs=[pl.BlockSpec((cl, w), lambda b, c: (b * nc + c, 0)),
                   pl.BlockSpec((1, HEADS, MLSTM_HEAD_DIM, MLSTM_HEAD_DIM), lambda b, c: (b, 0, 0, 0)),
                   pl.BlockSpec((1, HEADS, MLSTM_HEAD_DIM), lambda b, c: (b, 0, 0)),
                   pl.BlockSpec((cl, LANES), lambda b, c: (b * nc + c, 0))],
        out_shape=[jax.ShapeDtypeStruct((t, w), F32),
                   jax.ShapeDtypeStruct((nb, HEADS, MLSTM_HEAD_DIM, MLSTM_HEAD_DIM), F32),
                   jax.ShapeDtypeStruct((nb, HEADS, MLSTM_HEAD_DIM), F32),
                   jax.ShapeDtypeStruct((t, LANES), F32)],
        scratch_shapes=[pltpu.VMEM((8, LANES), F32)],
        compiler_params=_cparams(("arbitrary", "arbitrary")),
        name="mlstm_prompt",
    )(proj, proj, proj, g, gt)


STAT_DEN, STAT_IW, STAT_M = 0, HEADS, 2 * HEADS


def _mlstm_short_intra_kernel(q_ref, k_ref, v_ref, g_ref, gt_ref, num_ref, stat_ref, *, seq_len):
    tl = LANES
    kscale = MLSTM_HEAD_DIM ** -0.5
    shift = seq_len.bit_length() - 1
    row = lax.broadcasted_iota(jnp.int32, (tl, tl), 0)
    col = lax.broadcasted_iota(jnp.int32, (tl, tl), 1)
    mask = (col <= row) & (lax.shift_right_logical(col, shift) == lax.shift_right_logical(row, shift))
    stat_ref[...] = jnp.zeros_like(stat_ref)
    for h in range(HEADS):
        cs = slice(h * MLSTM_HEAD_DIM, (h + 1) * MLSTM_HEAD_DIM)
        b_col = g_ref[:, HEADS + h:HEADS + h + 1]
        m0_col = g_ref[:, 2 * HEADS + h:2 * HEADS + h + 1]
        ig_row = gt_ref[h:h + 1, :]
        b_row = gt_ref[HEADS + h:HEADS + h + 1, :]
        q_bf = q_ref[:, cs].astype(BF16)
        ks_bf = (k_ref[:, cs] * kscale).astype(BF16)
        num, den, m_col, inter_w = _mlstm_intra(q_bf, ks_bf, v_ref[:, cs].astype(BF16),
                                                ig_row, b_row, b_col, m0_col, mask)
        num_ref[:, cs] = num
        stat_ref[:, STAT_DEN + h:STAT_DEN + h + 1] = den
        stat_ref[:, STAT_IW + h:STAT_IW + h + 1] = inter_w
        stat_ref[:, STAT_M + h:STAT_M + h + 1] = m_col


def _mlstm_short_intra(proj, g, gt, seq_len):
    t = proj.shape[0]
    tl = LANES
    w = MLSTM_WIDTH
    assert tl % seq_len == 0 and t % tl == 0

    def colspec(col0):
        return pl.BlockSpec((tl, w), lambda i: (i, col0 // w))

    return pl.pallas_call(
        functools.partial(_mlstm_short_intra_kernel, seq_len=seq_len),
        grid=(t // tl,),
        in_specs=[colspec(COL_Q), colspec(COL_K), colspec(COL_V),
                  pl.BlockSpec((tl, LANES), lambda i: (i, 0)),
                  pl.BlockSpec((8, tl), lambda i: (0, i))],
        out_specs=[pl.BlockSpec((tl, w), lambda i: (i, 0)),
                   pl.BlockSpec((tl, LANES), lambda i: (i, 0))],
        out_shape=[jax.ShapeDtypeStruct((t, w), F32),
                   jax.ShapeDtypeStruct((t, LANES), F32)],
        compiler_params=_cparams(("parallel",)),
        name="mlstm_short_intra",
    )(proj, proj, proj, g, gt)


def _mlstm_state_head(h, q_ref, k_ref, v_ref, num_ref, g_ref, stat_ref, c0_ref, n0_ref, hb_ref, c_ref, n_ref):
    kscale = MLSTM_HEAD_DIM ** -0.5
    seq_len = q_ref.shape[0]
    last = slice(seq_len - 1, seq_len)
    cs = slice(h * MLSTM_HEAD_DIM, (h + 1) * MLSTM_HEAD_DIM)
    qf = q_ref[:, cs]
    kf = k_ref[:, cs] * kscale
    vf = v_ref[:, cs]
    ig_col = g_ref[:, h:h + 1]
    b_col = g_ref[:, HEADS + h:HEADS + h + 1]
    b_last = g_ref[last, HEADS + h:HEADS + h + 1]
    m0 = g_ref[last, 2 * HEADS + h:2 * HEADS + h + 1]
    den_intra = stat_ref[:, STAT_DEN + h:STAT_DEN + h + 1]
    inter_w = stat_ref[:, STAT_IW + h:STAT_IW + h + 1]
    m_col = stat_ref[:, STAT_M + h:STAT_M + h + 1]
    m_new = stat_ref[last, STAT_M + h:STAT_M + h + 1]
    c0 = c0_ref[0, 0]
    n0 = n0_ref[0, h:h + 1, :]
    num = num_ref[:, cs] + inter_w * lax.dot_general(
        qf.astype(BF16), c0.astype(BF16), (((1,), (1,)), ((), ())), preferred_element_type=F32)
    den = den_intra + inter_w * jnp.sum(qf * n0, axis=1, keepdims=True)
    hval = num * (1.0 / jnp.maximum(jnp.abs(den), jnp.exp(-m_col)))
    hb_ref[:, cs] = hval
    w_col = jnp.exp(b_last - b_col + ig_col - m_new)
    decay = jnp.exp(b_last + m0 - m_new)
    dc = lax.dot_general((w_col * vf).astype(BF16), kf.astype(BF16), (((0,), (0,)), ((), ())),
                         preferred_element_type=F32)
    c_ref[0, h] = decay * c0 + dc
    n_ref[0, h:h + 1, :] = decay * n0 + jnp.sum(w_col * kf, axis=0, keepdims=True)


def _inproj_state_kernel(h_ref, w_ref, *refs):
    tok_refs, c0_refs, n0_ref = refs[:6], refs[6:6 + HEADS], refs[6 + HEADS]
    o_ref, out_state_refs = refs[7 + HEADS], refs[8 + HEADS:]
    half = w_ref.shape[0] // 2
    nt = (((1,), (1,)), ((), ()))
    o_ref[:, :half] = lax.dot_general(h_ref[...], w_ref[:half, :], nt, preferred_element_type=F32)
    for h in range(HEADS):
        _mlstm_state_head(h, *tok_refs, c0_refs[h], n0_ref, *out_state_refs)
    o_ref[:, half:] = lax.dot_general(h_ref[...], w_ref[half:, :], nt, preferred_element_type=F32)


def _inproj_with_state(h, w_bf, tm, tn, proj_s, num_s, g_s, stat_s, c0, n0, seq_len):
    t, d = h.shape
    n_cols = w_bf.shape[0]
    n_seq = c0.shape[0]
    n_inner = t // tm
    assert n_cols % tn == 0 and t % tm == 0 and (n_cols // tn) * n_inner == n_seq
    w = MLSTM_WIDTH
    hd = MLSTM_HEAD_DIM

    def seq(j, i):
        return j * n_inner + i

    def colspec(col0):
        return pl.BlockSpec((seq_len, w), lambda j, i: (seq(j, i), col0 // w))

    rowspec = pl.BlockSpec((seq_len, w), lambda j, i: (seq(j, i), 0))
    lanespec = pl.BlockSpec((seq_len, LANES), lambda j, i: (seq(j, i), 0))
    state4 = pl.BlockSpec((1, HEADS, hd, hd), lambda j, i: (seq(j, i), 0, 0, 0))
    state3 = pl.BlockSpec((1, HEADS, hd), lambda j, i: (seq(j, i), 0, 0))
    return pl.pallas_call(
        _inproj_state_kernel,
        grid=(n_cols // tn, n_inner),
        in_specs=[pl.BlockSpec((tm, d), lambda j, i: (i, 0)),
                  pl.BlockSpec((tn, d), lambda j, i: (j, 0)),
                  colspec(COL_Q), colspec(COL_K), colspec(COL_V),
                  rowspec, lanespec, lanespec,
                  *[pl.BlockSpec((1, 1, hd, hd), functools.partial(lambda j, i, hh: (seq(j, i), hh, 0, 0), hh=hh))
                    for hh in range(HEADS)],
                  state3],
        out_specs=[pl.BlockSpec((tm, tn), lambda j, i: (i, j)), rowspec, state4, state3],
        out_shape=[jax.ShapeDtypeStruct((t, n_cols), F32),
                   jax.ShapeDtypeStruct((n_seq * seq_len, w), F32),
                   jax.ShapeDtypeStruct((n_seq, HEADS, hd, hd), F32),
                   jax.ShapeDtypeStruct((n_seq, HEADS, hd), F32)],
        compiler_params=_cparams(("arbitrary", "arbitrary")),
        name="inproj_state",
    )(h, w_bf, proj_s, proj_s, proj_s, num_s, g_s, stat_s, *([c0] * HEADS), n0)


def _resident(shape):
    return pl.BlockSpec(shape, lambda i: (0, 0), pipeline_mode=pl.Buffered(1))


def _merge_kernel(xa_ref, hb_ref, o_ref, z_ref, xc_ref, ga_ref, gb_ref, gc_ref, wa_ref, wb_ref, wc_ref, m_ref):
    ya = jnp.dot(xa_ref[...].astype(BF16), wa_ref[...], preferred_element_type=F32)
    merged = _sigmoid(ga_ref[...]) * ya
    xb = _sigmoid(o_ref[...]) * hb_ref[...] * _silu(z_ref[...])
    yb = jnp.dot(xb.astype(BF16), wb_ref[...], preferred_element_type=F32)
    merged = merged + _sigmoid(gb_ref[...]) * yb
    yc = jnp.dot(xc_ref[...].astype(BF16), wc_ref[...], preferred_element_type=F32)
    merged = merged + _sigmoid(gc_ref[...]) * yc
    m_ref[...] = merged.astype(m_ref.dtype)


def _out_kernel(m_ref, x_ref, gpost_ref, wo_ref, y_ref):
    out = jnp.dot(m_ref[...], wo_ref[...], preferred_element_type=F32)
    ms = jnp.mean(out * out, axis=-1, keepdims=True)
    y_ref[...] = x_ref[...] + out * lax.rsqrt(ms + EPS) * gpost_ref[...]


def _final(xa, hb, xc, proj_a, proj_b, x2d, gpost_row, wa, wb, wc, wo, tm_merge, tm_out):
    t, d = x2d.shape

    def rowspec(tm, width, colblk=0):
        return pl.BlockSpec((tm, width), lambda i: (i, colblk))

    tm = tm_merge
    merged = pl.pallas_call(
        _merge_kernel,
        grid=(t // tm,),
        in_specs=[rowspec(tm, POOL_WIDTH), rowspec(tm, MLSTM_WIDTH),
                  rowspec(tm, MLSTM_WIDTH, COL_O // MLSTM_WIDTH), rowspec(tm, MLSTM_WIDTH, COL_Z // MLSTM_WIDTH),
                  rowspec(tm, MEM_WIDTH),
                  rowspec(tm, d, COL_GA // d), rowspec(tm, d, COL_GB // d), rowspec(tm, d, COL_GC // d),
                  _resident(wa.shape), _resident(wb.shape), _resident(wc.shape)],
        out_specs=rowspec(tm, d),
        out_shape=jax.ShapeDtypeStruct((t, d), BF16),
        compiler_params=_cparams(("parallel",)),
        name="merge",
    )(xa, hb, proj_a, proj_a, xc, proj_b, proj_b, proj_b, wa, wb, wc)
    tm = tm_out
    return pl.pallas_call(
        _out_kernel,
        grid=(t // tm,),
        in_specs=[rowspec(tm, d), rowspec(tm, d), _resident((1, d)), _resident(wo.shape)],
        out_specs=rowspec(tm, d),
        out_shape=jax.ShapeDtypeStruct((t, d), F32),
        compiler_params=_cparams(("parallel",)),
        name="out_proj",
    )(merged, x2d, gpost_row, wo)


B_ROW0 = GATE_COL0 + N_GATE_COLS


def _pool_and_merge(x2d, nb, seq_len, proj_a, proj_b, xb, xc, pool_buf, wts, start):
    if seq_len >= 512:
        xa, new_buf = _pool_branch(proj_a, pool_buf, wts["w_pool_grp"], wts["pool_scale"], nb, seq_len,
                                   nseq=1, tm=512, start=start)
    else:
        xa, new_buf = _pool_branch(proj_a, pool_buf, wts["w_pool_grp"], wts["pool_scale"], nb, seq_len,
                                   nseq=32, tm=seq_len, start=start)
    y = _final(xa, xb, xc, proj_a, proj_b, x2d, wts["g_post"], wts["w_br_pool"], wts["w_br_mlstm"],
               wts["w_br_mem"], wts["w_out"], tm_merge=256, tm_out=512)
    return y.reshape(nb, seq_len, x2d.shape[1]), new_buf


def kernel(x_prompt, x_sample, state_pool, state_mlstm_C, state_mlstm_n, state_mlstm_m, cache_mem_k,
           cache_mem_v, mem_prompt, g_pre, g_post, w_in, b_mlstm_i, b_mlstm_f, w_pool_grp, pool_scale,
           g_mem, w_mem_kv, w_br_pool, w_br_mlstm, w_br_mem, w_out):
    nbp = x_prompt.shape[0]
    assert w_in.shape[0] == 1, "single-layer problem: the kernels index layer 0 of the stacked weights"
    l = 0
    wts = {
        "g_pre": g_pre[l][None, :],
        "g_post": g_post[l][None, :],
        "w_in": jnp.transpose(w_in[l]),
        "gate_bias": jnp.pad(jnp.concatenate([b_mlstm_i[l], b_mlstm_f[l]]), (0, LANES - N_GATE_COLS))[None, :],
        "w_pool_grp": w_pool_grp[l].astype(BF16),
        "pool_scale": pool_scale[l][None, :],
        "w_br_pool": w_br_pool[l].astype(BF16),
        "w_br_mlstm": w_br_mlstm[l].astype(BF16),
        "w_br_mem": w_br_mem[l].astype(BF16),
        "w_out": w_out[l].astype(BF16),
    }
    mk2d, mv2d = _mem_kv(mem_prompt.reshape(nbp * MEM_TOKENS, D_MODEL), g_mem[l][None, :],
                         w_mem_kv[l].astype(BF16), tm=256)
    mk3, mv3 = mk2d.reshape(nbp, MEM_TOKENS, MEM_WIDTH), mv2d.reshape(nbp, MEM_TOKENS, MEM_WIDTH)
    w_t = wts["w_in"]

    nbs, ls, d = x_sample.shape
    ts = nbs * ls
    xs2d = x_sample.reshape(ts, d)
    m0_rows = jnp.pad(jnp.repeat(state_mlstm_m[l], ls, axis=0), ((0, 0), (2 * HEADS, LANES - 3 * HEADS)))
    hs, gs, gts = _norm_gate_prep(xs2d, wts["g_pre"], w_t, wts["gate_bias"], m0_rows, ls)
    proj_a_s, w_a_bf = _inproj(hs, w_t, 0, N_PROJ_A, tm=ts, tn=1024)
    proj_b_s, w_b_bf = _inproj(hs, w_t, B_ROW0, N_PROJ_B, tm=ts, tn=1024)
    num_s, stat_s = _mlstm_short_intra(proj_a_s, gs, gts, ls)
    m_s = stat_s.reshape(nbs, ls, LANES)[:, ls - 1, STAT_M:STAT_M + HEADS]

    _, lp, _ = x_prompt.shape
    tp = nbp * lp
    xp2d = x_prompt.reshape(tp, d)
    hp, gp, gtp = _norm_gate_prep(xp2d, wts["g_pre"], w_t, wts["gate_bias"], jnp.zeros((tp, LANES), F32),
                                  PROMPT_CHUNK)
    proj_a_p, xb_s, c_s, n_s = _inproj_with_state(hp, w_a_bf, 256, 3072, proj_a_s, num_s, gs, stat_s,
                                                  state_mlstm_C[l], state_mlstm_n[l], ls)
    proj_b_p, xc_s = _inproj_with_attn(hp, w_b_bf, 1024, 1024, proj_b_s, cache_mem_k[l], cache_mem_v[l], ls)
    xb_p, c_p, n_p, mcol_p = _mlstm_prompt(proj_a_p, gp, gtp, nbp, lp)
    m_p = mcol_p.reshape(nbp, lp, LANES)[:, lp - 1, :HEADS]
    xc_p = _mem_attn(proj_b_p, mk3, mv3, nbp, lp, nseq=1, tm=512)

    yp, pool_p = _pool_and_merge(xp2d, nbp, lp, proj_a_p, proj_b_p, xb_p, xc_p,
                                 jnp.zeros((nbp, POOL_BUF, POOL_WIDTH), F32), wts, 0)
    ys, pool_s = _pool_and_merge(xs2d, nbs, ls, proj_a_s, proj_b_s, xb_s, xc_s, state_pool[l], wts, PAST_LEN)
    kv_shape = (1, nbp, MEM_TOKENS, HEADS, MEM_HEAD_DIM)
    return (yp, ys, pool_p[None], c_p[None], n_p[None], m_p[None],
            mk2d.reshape(kv_shape), mv2d.reshape(kv_shape),
            pool_s[None], c_s[None], n_s[None], m_s[None])
```

```python
import functools

import jax
import jax.numpy as jnp
from jax import lax
from jax.experimental import pallas as pl
from jax.experimental.pallas import tpu as pltpu

F32 = jnp.float32
BF16 = jnp.bfloat16

D_MODEL = 2048
PAST_LEN = 16384
POOL_WINDOWS = (2, 4, 8, 16)
POOL_GROUP_WIDTH = 256
POOL_WIDTH = 1024
POOL_BUF = 15
HEADS = 4
MLSTM_HEAD_DIM = 512
MLSTM_WIDTH = 2048
PROMPT_CHUNK = 256
MEM_TOKENS = 256
MEM_HEAD_DIM = 256
MEM_WIDTH = 1024
EPS = 1e-6
N_GATE_COLS = 2 * HEADS
GATE_COL0 = 2 * POOL_WIDTH + 5 * MLSTM_WIDTH
LANES = 128
SUBLANES = 8
COL_PV, COL_PZ = 0, 1024
COL_Q, COL_K, COL_V, COL_O, COL_Z = 2048, 4096, 6144, 8192, 10240
COL_CQ, COL_CZ = 0, 1024
COL_GA, COL_GB, COL_GC = 2048, 4096, 6144
N_PROJ_A = GATE_COL0
N_PROJ_B = 2 * MEM_WIDTH + 3 * D_MODEL
VMEM_LIMIT = 56 * 1024 * 1024


def _cparams(sem):
    return pltpu.CompilerParams(dimension_semantics=sem, vmem_limit_bytes=VMEM_LIMIT)


def _sigmoid(x):
    return 0.5 * jnp.tanh(0.5 * x) + 0.5


def _silu(x):
    return x * _sigmoid(x)


def _inproj_kernel(h_ref, w_ref, o_ref, wbf_ref):
    @pl.when(pl.program_id(1) == 0)
    def _():
        wbf_ref[...] = w_ref[...].astype(BF16)

    o_ref[...] = lax.dot_general(h_ref[...], wbf_ref[...], (((1,), (1,)), ((), ())), preferred_element_type=F32)


def _inproj(h, w_t, row0, n_cols, tm, tn):
    t, d = h.shape
    assert row0 % SUBLANES == 0 and n_cols % tn == 0 and t % tm == 0 and row0 + n_cols <= w_t.shape[0]
    return pl.pallas_call(
        _inproj_kernel,
        grid=(n_cols // tn, t // tm),
        in_specs=[pl.BlockSpec((tm, d), lambda j, i: (i, 0)),
                  pl.BlockSpec((pl.Element(tn), pl.Element(d)),
                               lambda j, i: (pl.multiple_of(row0 + j * tn, SUBLANES), 0))],
        out_specs=[pl.BlockSpec((tm, tn), lambda j, i: (i, j)),
                   pl.BlockSpec((tn, d), lambda j, i: (j, 0))],
        out_shape=[jax.ShapeDtypeStruct((t, n_cols), F32),
                   jax.ShapeDtypeStruct((n_cols, d), BF16)],
        compiler_params=_cparams(("arbitrary", "arbitrary")),
        name="inproj",
    )(h, w_t)


GATE_TILE = 512


def _gate_kernel(x_ref, gpre_ref, wg_ref, bias_ref, m0_ref, h_ref, g_ref, gt_ref, *, seg, ct):
    x = x_ref[...]
    ms = jnp.mean(x * x, axis=-1, keepdims=True)
    h_ref[...] = (x * lax.rsqrt(ms + EPS) * gpre_ref[...]).astype(h_ref.dtype)
    src = lax.broadcasted_iota(jnp.int32, (ct, ct), 0)
    dst = lax.broadcasted_iota(jnp.int32, (ct, ct), 1)
    upto = src <= dst
    if seg < ct:
        shift = seg.bit_length() - 1
        upto = upto & (lax.shift_right_logical(src, shift) == lax.shift_right_logical(dst, shift))
    upto = upto.astype(F32)
    lane = lax.broadcasted_iota(jnp.int32, (ct, LANES), 1)
    gate_row = lax.broadcasted_iota(jnp.int32, (2 * HEADS, ct), 0)
    wg = wg_ref[...].astype(BF16)
    for r in range(GATE_TILE // ct):
        rs = slice(r * ct, (r + 1) * ct)
        x = lax.dot_general(h_ref[rs, :], wg, (((1,), (1,)), ((), ())), preferred_element_type=F32) + bias_ref[...]
        xt = x.T[:2 * HEADS, :]
        lf = jnp.minimum(xt, 0.0) - jnp.log1p(jnp.exp(-jnp.abs(xt)))
        cums = jnp.dot(lf, upto, precision=lax.Precision.HIGHEST, preferred_element_type=F32)
        gt_ref[:, rs] = jnp.where(gate_row < HEADS, xt, cums)
        cums_col = jnp.concatenate([cums, jnp.zeros((LANES - 2 * HEADS, ct), F32)], axis=0).T
        g_ref[rs, :] = jnp.where(lane < HEADS, x, jnp.where(lane < 2 * HEADS, cums_col, m0_ref[rs, :]))


def _norm_gate_prep(x2d, gpre_row, w_t, bias_row, m0_rows, seg):
    t, d = x2d.shape
    tg = GATE_TILE
    ct = max(seg, LANES)
    assert GATE_COL0 % LANES == 0 and tg % ct == 0 and ct % seg == 0 and seg & (seg - 1) == 0
    return pl.pallas_call(
        functools.partial(_gate_kernel, seg=seg, ct=ct),
        grid=(t // tg,),
        in_specs=[pl.BlockSpec((tg, d), lambda i: (i, 0)),
                  pl.BlockSpec((1, d), lambda i: (0, 0)),
                  pl.BlockSpec((LANES, d), lambda i: (GATE_COL0 // LANES, 0)),
                  pl.BlockSpec((1, LANES), lambda i: (0, 0)),
                  pl.BlockSpec((tg, LANES), lambda i: (i, 0))],
        out_specs=[pl.BlockSpec((tg, d), lambda i: (i, 0)),
                   pl.BlockSpec((tg, LANES), lambda i: (i, 0)),
                   pl.BlockSpec((8, tg), lambda i: (0, i))],
        out_shape=[jax.ShapeDtypeStruct((t, d), BF16),
                   jax.ShapeDtypeStruct((t, LANES), F32),
                   jax.ShapeDtypeStruct((8, t), F32)],
        compiler_params=_cparams(("parallel",)),
        name="norm_gate_prep",
    )(x2d, gpre_row, w_t, bias_row, m0_rows)


def _pool_kernel(pv_ref, pz_ref, buf_ref, wg_ref, scale_ref, xa_ref, nbuf_ref, s_ref, p_ref,
                 *, nseq, tm, nt, start):
    j = pl.program_id(1)

    def one_seq(b, r0):
        if nt > 1:
            @pl.when(j == 0)
            def _():
                s_ref[1:16, :] = buf_ref[b]
        else:
            s_ref[1:16, :] = buf_ref[b]
        s_ref[16:16 + tm, :] = pv_ref[pl.ds(r0, tm), :]
        pos = start + j * tm + lax.broadcasted_iota(jnp.int32, (tm, 1), 0)
        for g, w in enumerate(POOL_WINDOWS):
            cs = slice(g * POOL_GROUP_WIDTH, (g + 1) * POOL_GROUP_WIDTH)
            acc = s_ref[16:16 + tm, cs]
            for i in range(1, w):
                acc = acc + s_ref[16 - i:16 - i + tm, cs]
            inv_cnt = 1.0 / jnp.minimum(w, pos + 1).astype(F32)
            p_ref[pl.ds(r0, tm), cs] = acc * inv_cnt - s_ref[16:16 + tm, cs]
        tail = s_ref[tm + 1:tm + 16, :]
        nbuf_ref[b] = tail
        if nt > 1:
            s_ref[1:16, :] = tail

    if nseq == 1:
        one_seq(0, 0)
    else:
        def body(b, carry):
            one_seq(b, pl.multiple_of(b * tm, 8))
            return carry
        lax.fori_loop(0, nseq, body, 0)

    for g in range(len(POOL_WINDOWS)):
        cs = slice(g * POOL_GROUP_WIDTH, (g + 1) * POOL_GROUP_WIDTH)
        pa = jnp.dot(p_ref[:, cs].astype(BF16), wg_ref[g], preferred_element_type=F32) * scale_ref[:, cs]
        xa_ref[:, cs] = (pa * _silu(pz_ref[:, cs])).astype(xa_ref.dtype)


def _pool_branch(proj, buf, w_grp, scale_row, nb, seq_len, nseq, tm, start):
    assert nseq == 1 or tm == seq_len
    nt = seq_len // tm
    rows = nseq * tm
    t = nb * seq_len
    pvb, pzb = COL_PV // POOL_WIDTH, COL_PZ // POOL_WIDTH
    return pl.pallas_call(
        functools.partial(_pool_kernel, nseq=nseq, tm=tm, nt=nt, start=start),
        grid=(nb // nseq, nt),
        in_specs=[pl.BlockSpec((rows, POOL_WIDTH), lambda i, j: (i * nt + j, pvb)),
                  pl.BlockSpec((rows, POOL_WIDTH), lambda i, j: (i * nt + j, pzb)),
                  pl.BlockSpec((nseq, POOL_BUF, POOL_WIDTH), lambda i, j: (i, 0, 0)),
                  pl.BlockSpec((4, POOL_GROUP_WIDTH, POOL_GROUP_WIDTH), lambda i, j: (0, 0, 0)),
                  pl.BlockSpec((1, POOL_WIDTH), lambda i, j: (0, 0))],
        out_specs=[pl.BlockSpec((rows, POOL_WIDTH), lambda i, j: (i * nt + j, 0)),
                   pl.BlockSpec((nseq, POOL_BUF, POOL_WIDTH), lambda i, j: (i, 0, 0))],
        out_shape=[jax.ShapeDtypeStruct((t, POOL_WIDTH), BF16),
                   jax.ShapeDtypeStruct((nb, POOL_BUF, POOL_WIDTH), F32)],
        scratch_shapes=[pltpu.VMEM((16 + tm, POOL_WIDTH), F32),
                        pltpu.VMEM((rows, POOL_WIDTH), F32)],
        compiler_params=_cparams(("arbitrary", "arbitrary")),
        name="pool_branch",
    )(proj, proj, buf, w_grp, scale_row)


def _memkv_kernel(m_ref, g_ref, w_ref, k_ref, v_ref):
    x = m_ref[...]
    ms = jnp.mean(x * x, axis=-1, keepdims=True)
    h = (x * lax.rsqrt(ms + EPS) * g_ref[...]).astype(BF16)
    kv = jnp.dot(h, w_ref[...], preferred_element_type=F32)
    k_ref[...] = kv[:, :MEM_WIDTH]
    v_ref[...] = kv[:, MEM_WIDTH:]


def _mem_kv(mem2d, g_row, w_bf, tm):
    t, d = mem2d.shape
    return pl.pallas_call(
        _memkv_kernel,
        grid=(t // tm,),
        in_specs=[pl.BlockSpec((tm, d), lambda i: (i, 0)),
                  pl.BlockSpec((1, d), lambda i: (0, 0)),
                  pl.BlockSpec((d, 2 * MEM_WIDTH), lambda i: (0, 0))],
        out_specs=[pl.BlockSpec((tm, MEM_WIDTH), lambda i: (i, 0)),
                   pl.BlockSpec((tm, MEM_WIDTH), lambda i: (i, 0))],
        out_shape=[jax.ShapeDtypeStruct((t, MEM_WIDTH), F32),
                   jax.ShapeDtypeStruct((t, MEM_WIDTH), F32)],
        compiler_params=_cparams(("parallel",)),
        name="mem_kv",
    )(mem2d, g_row, w_bf)


def _attn_kernel(cq_ref, cz_ref, k_ref, v_ref, xc_ref, *, nseq, tm):
    scale = MEM_HEAD_DIM ** -0.5
    for b in range(nseq):
        rs = slice(b * tm, (b + 1) * tm)
        for h in range(HEADS):
            cs = slice(h * MEM_HEAD_DIM, (h + 1) * MEM_HEAD_DIM)
            q = cq_ref[rs, cs].astype(BF16)
            k = k_ref[b, :, cs].astype(BF16)
            v = v_ref[b, :, cs].astype(BF16)
            s = lax.dot_general(q, k, (((1,), (1,)), ((), ())), preferred_element_type=F32) * scale
            p = jnp.exp(s - jnp.max(s, axis=-1, keepdims=True))
            a = p * (1.0 / jnp.sum(p, axis=-1, keepdims=True))
            o = jnp.dot(a.astype(BF16), v, preferred_element_type=F32)
            xc_ref[rs, cs] = (o * _silu(cz_ref[rs, cs])).astype(xc_ref.dtype)


def _mem_attn(proj, mem_k, mem_v, nb, seq_len, nseq, tm):
    assert nseq == 1 or tm == seq_len
    nt = seq_len // tm
    rows = nseq * tm
    t = nb * seq_len
    cqb, czb = COL_CQ // MEM_WIDTH, COL_CZ // MEM_WIDTH
    return pl.pallas_call(
        functools.partial(_attn_kernel, nseq=nseq, tm=tm),
        grid=(nb // nseq, nt),
        in_specs=[pl.BlockSpec((rows, MEM_WIDTH), lambda i, j: (i * nt + j, cqb)),
                  pl.BlockSpec((rows, MEM_WIDTH), lambda i, j: (i * nt + j, czb)),
                  pl.BlockSpec((nseq, MEM_TOKENS, MEM_WIDTH), lambda i, j: (i, 0, 0)),
                  pl.BlockSpec((nseq, MEM_TOKENS, MEM_WIDTH), lambda i, j: (i, 0, 0))],
        out_specs=pl.BlockSpec((rows, MEM_WIDTH), lambda i, j: (i * nt + j, 0)),
        out_shape=jax.ShapeDtypeStruct((t, MEM_WIDTH), BF16),
        compiler_params=_cparams(("parallel", "arbitrary")),
        name="mem_attn",
    )(proj, proj, mem_k, mem_v)


def _attn_short_weights(cq_ref, k_ref, b, tm):
    scale = MEM_HEAD_DIM ** -0.5
    nr, nc = tm * HEADS, MEM_TOKENS * HEADS
    row_head = lax.broadcasted_iota(jnp.int32, (nr, nc), 0) // tm
    col_head = lax.broadcasted_iota(jnp.int32, (nr, nc), 1) % HEADS
    rs = slice(b * tm, (b + 1) * tm)
    kf = k_ref[b].reshape(nc, MEM_HEAD_DIM).astype(BF16)
    q = jnp.concatenate([cq_ref[rs, h * MEM_HEAD_DIM:(h + 1) * MEM_HEAD_DIM] for h in range(HEADS)], axis=0)
    s = lax.dot_general(q.astype(BF16), kf, (((1,), (1,)), ((), ())), preferred_element_type=F32) * scale
    s = jnp.where(row_head == col_head, s, -jnp.inf)
    p = jnp.exp(s - jnp.max(s, axis=-1, keepdims=True))
    return (p * (1.0 / jnp.sum(p, axis=-1, keepdims=True))).astype(BF16)


def _attn_short_output(a, cz_ref, v_ref, xc_ref, b, tm):
    rs = slice(b * tm, (b + 1) * tm)
    vf = v_ref[b].reshape(MEM_TOKENS * HEADS, MEM_HEAD_DIM).astype(BF16)
    o = jnp.dot(a, vf, preferred_element_type=F32)
    for h in range(HEADS):
        cs = slice(h * MEM_HEAD_DIM, (h + 1) * MEM_HEAD_DIM)
        xc_ref[rs, cs] = o[h * tm:(h + 1) * tm, :] * _silu(cz_ref[rs, cs])


def _inproj_attn_kernel(h_ref, w_ref, cq_ref, cz_ref, k_ref, v_ref, o_ref, xc_ref, *, nseq, seq_len):
    half = w_ref.shape[0] // 2
    nt = (((1,), (1,)), ((), ()))
    a = [_attn_short_weights(cq_ref, k_ref, b, seq_len) for b in range(nseq)]
    o_ref[:, :half] = lax.dot_general(h_ref[...], w_ref[:half, :], nt, preferred_element_type=F32)
    for b in range(nseq):
        _attn_short_output(a[b], cz_ref, v_ref, xc_ref, b, seq_len)
    o_ref[:, half:] = lax.dot_general(h_ref[...], w_ref[half:, :], nt, preferred_element_type=F32)


def _inproj_with_attn(h, w_bf, tm, tn, proj_s, mem_k, mem_v, seq_len):
    t, d = h.shape
    n_cols = w_bf.shape[0]
    nb = mem_k.shape[0]
    n_inner = t // tm
    n_steps = (n_cols // tn) * n_inner
    nseq = nb // n_steps
    rows = nseq * seq_len
    assert n_cols % tn == 0 and t % tm == 0 and nseq * n_steps == nb and rows % SUBLANES == 0
    cqb, czb = COL_CQ // MEM_WIDTH, COL_CZ // MEM_WIDTH
    kv_spec = pl.BlockSpec((nseq, MEM_TOKENS, HEADS, MEM_HEAD_DIM), lambda j, i: (j * n_inner + i, 0, 0, 0))
    return pl.pallas_call(
        functools.partial(_inproj_attn_kernel, nseq=nseq, seq_len=seq_len),
        grid=(n_cols // tn, n_inner),
        in_specs=[pl.BlockSpec((tm, d), lambda j, i: (i, 0)),
                  pl.BlockSpec((tn, d), lambda j, i: (j, 0)),
                  pl.BlockSpec((rows, MEM_WIDTH), lambda j, i: (j * n_inner + i, cqb)),
                  pl.BlockSpec((rows, MEM_WIDTH), lambda j, i: (j * n_inner + i, czb)),
                  kv_spec, kv_spec],
        out_specs=[pl.BlockSpec((tm, tn), lambda j, i: (i, j)),
                   pl.BlockSpec((rows, MEM_WIDTH), lambda j, i: (j * n_inner + i, 0))],
        out_shape=[jax.ShapeDtypeStruct((t, n_cols), F32),
                   jax.ShapeDtypeStruct((nb * seq_len, MEM_WIDTH), F32)],
        compiler_params=_cparams(("arbitrary", "arbitrary")),
        name="inproj_attn",
    )(h, w_bf, proj_s, proj_s, mem_k, mem_v)


def _mlstm_intra(q_bf, ks_bf, v_bf, ig_row, b_row, b_col, m0_col, mask):
    dlog = jnp.where(mask, b_col - b_row + ig_row, -jnp.inf)
    inter = b_col + m0_col
    m_col = jnp.maximum(inter, jnp.max(dlog, axis=1, keepdims=True))
    dw = jnp.exp(dlog - m_col)
    inter_w = jnp.exp(inter - m_col)
    s = lax.dot_general(q_bf, ks_bf, (((1,), (1,)), ((), ())), preferred_element_type=F32) * dw
    num = jnp.dot(s.astype(BF16), v_bf, preferred_element_type=F32)
    den = jnp.sum(s, axis=1, keepdims=True)
    return num, den, m_col, inter_w


def _mlstm_prompt_kernel(q_ref, k_ref, v_ref, g_ref, gt_ref, hb_ref, c_ref, n_ref, mcol_ref, m_scr):
    @pl.when(pl.program_id(1) == 0)
    def _():
        c_ref[...] = jnp.zeros_like(c_ref)
        n_ref[...] = jnp.zeros_like(n_ref)
        m_scr[...] = jnp.zeros_like(m_scr)

    cl = PROMPT_CHUNK
    row = lax.broadcasted_iota(jnp.int32, (cl, cl), 0)
    col = lax.broadcasted_iota(jnp.int32, (cl, cl), 1)
    mask = col <= row
    kscale = MLSTM_HEAD_DIM ** -0.5
    mcol_ref[...] = jnp.zeros_like(mcol_ref)
    nt = (((1,), (1,)), ((), ()))
    gates = []
    for h in range(HEADS):
        ig_col = g_ref[:, h:h + 1]
        b_col = g_ref[:, HEADS + h:HEADS + h + 1]
        ig_row = gt_ref[h:h + 1, :]
        b_row = gt_ref[HEADS + h:HEADS + h + 1, :]
        m0 = m_scr[h:h + 1, 0:1]
        dlog = jnp.where(mask, b_col - b_row + ig_row, -jnp.inf)
        inter = b_col + m0
        m_col = jnp.maximum(inter, jnp.max(dlog, axis=1, keepdims=True))
        dw = jnp.exp(dlog - m_col)
        inter_w = jnp.exp(inter - m_col)
        m_new = m_col[cl - 1:cl, :]
        b_last = b_col[cl - 1:cl, :]
        w_col = jnp.exp(b_last - b_col + ig_col - m_new)
        decay = jnp.exp(b_last + m0 - m_new)
        m_scr[h:h + 1, :] = jnp.broadcast_to(m_new, (1, LANES))
        mcol_ref[:, h:h + 1] = m_col
        gates.append((m_col, dw, inter_w, w_col, decay))
    prods = []
    for h in range(HEADS):
        cs = slice(h * MLSTM_HEAD_DIM, (h + 1) * MLSTM_HEAD_DIM)
        w_col = gates[h][3]
        q_bf = q_ref[:, cs].astype(BF16)
        ks_bf = (k_ref[:, cs] * kscale).astype(BF16)
        qk = lax.dot_general(q_bf, ks_bf, nt, preferred_element_type=F32)
        inter_num = lax.dot_general(q_bf, c_ref[0, h].astype(BF16), nt, preferred_element_type=F32)
        dc = lax.dot_general((w_col * v_ref[:, cs]).astype(BF16), ks_bf, (((0,), (0,)), ((), ())),
                             preferred_element_type=F32)
        prods.append((qk, inter_num, dc))
    for h in range(HEADS):
        cs = slice(h * MLSTM_HEAD_DIM, (h + 1) * MLSTM_HEAD_DIM)
        m_col, dw, inter_w, w_col, decay = gates[h]
        qk, inter_num, dc = prods[h]
        n0 = n_ref[0, h:h + 1, :]
        s = qk * dw
        num = jnp.dot(s.astype(BF16), v_ref[:, cs].astype(BF16), preferred_element_type=F32) + inter_w * inter_num
        den = jnp.sum(s, axis=1, keepdims=True) + inter_w * jnp.sum(q_ref[:, cs] * n0, axis=1, keepdims=True)
        hb_ref[:, cs] = num * (1.0 / jnp.maximum(jnp.abs(den), jnp.exp(-m_col)))
        c_ref[0, h] = decay * c_ref[0, h] + dc
        n_ref[0, h:h + 1, :] = decay * n0 + jnp.sum(w_col * (k_ref[:, cs] * kscale), axis=0, keepdims=True)


def _mlstm_prompt(proj, g, gt, nb, seq_len):
    cl = PROMPT_CHUNK
    nc = seq_len // cl
    t = nb * seq_len
    w = MLSTM_WIDTH

    def colspec(col0):
        return pl.BlockSpec((cl, w), lambda b, c: (b * nc + c, col0 // w))

    return pl.pallas_call(
        _mlstm_prompt_kernel,
        grid=(nb, nc),
        in_specs=[colspec(COL_Q), colspec(COL_K), colspec(COL_V),
                  pl.BlockSpec((cl, LANES), lambda b, c: (b * nc + c, 0)),
                  pl.BlockSpec((8, cl), lambda b, c: (0, b * nc + c))],
        out_specs=[pl.BlockSpec((cl, w), lambda b, c: (b * nc + c, 0)),
                   pl.BlockSpec((1, HEADS, MLSTM_HEAD_DIM, MLSTM_HEAD_DIM), lambda b, c: (b, 0, 0, 0)),
                   pl.BlockSpec((1, HEADS, MLSTM_HEAD_DIM), lambda b, c: (b, 0, 0)),
                   pl.BlockSpec((cl, LANES), lambda b, c: (b * nc + c, 0))],
        out_shape=[jax.ShapeDtypeStruct((t, w), F32),
                   jax.ShapeDtypeStruct((nb, HEADS, MLSTM_HEAD_DIM, MLSTM_HEAD_DIM), F32),
                   jax.ShapeDtypeStruct((nb, HEADS, MLSTM_HEAD_DIM), F32),
                   jax.ShapeDtypeStruct((t, LANES), F32)],
        scratch_shapes=[pltpu.VMEM((8, LANES), F32)],
        compiler_params=_cparams(("arbitrary", "arbitrary")),
        name="mlstm_prompt",
    )(proj, proj, proj, g, gt)


STAT_DEN, STAT_IW, STAT_M = 0, HEADS, 2 * HEADS


def _mlstm_short_intra_kernel(q_ref, k_ref, v_ref, g_ref, gt_ref, num_ref, stat_ref, *, seq_len):
    tl = LANES
    kscale = MLSTM_HEAD_DIM ** -0.5
    shift = seq_len.bit_length() - 1
    row = lax.broadcasted_iota(jnp.int32, (tl, tl), 0)
    col = lax.broadcasted_iota(jnp.int32, (tl, tl), 1)
    mask = (col <= row) & (lax.shift_right_logical(col, shift) == lax.shift_right_logical(row, shift))
    stat_ref[...] = jnp.zeros_like(stat_ref)
    for h in range(HEADS):
        cs = slice(h * MLSTM_HEAD_DIM, (h + 1) * MLSTM_HEAD_DIM)
        b_col = g_ref[:, HEADS + h:HEADS + h + 1]
        m0_col = g_ref[:, 2 * HEADS + h:2 * HEADS + h + 1]
        ig_row = gt_ref[h:h + 1, :]
        b_row = gt_ref[HEADS + h:HEADS + h + 1, :]
        q_bf = q_ref[:, cs].astype(BF16)
        ks_bf = (k_ref[:, cs] * kscale).astype(BF16)
        num, den, m_col, inter_w = _mlstm_intra(q_bf, ks_bf, v_ref[:, cs].astype(BF16),
                                                ig_row, b_row, b_col, m0_col, mask)
        num_ref[:, cs] = num
        stat_ref[:, STAT_DEN + h:STAT_DEN + h + 1] = den
        stat_ref[:, STAT_IW + h:STAT_IW + h + 1] = inter_w
        stat_ref[:, STAT_M + h:STAT_M + h + 1] = m_col


def _mlstm_short_intra(proj, g, gt, seq_len):
    t = proj.shape[0]
    tl = LANES
    w = MLSTM_WIDTH
    assert tl % seq_len == 0 and t % tl == 0

    def colspec(col0):
        return pl.BlockSpec((tl, w), lambda i: (i, col0 // w))

    return pl.pallas_call(
        functools.partial(_mlstm_short_intra_kernel, seq_len=seq_len),
        grid=(t // tl,),
        in_specs=[colspec(COL_Q), colspec(COL_K), colspec(COL_V),
                  pl.BlockSpec((tl, LANES), lambda i: (i, 0)),
                  pl.BlockSpec((8, tl), lambda i: (0, i))],
        out_specs=[pl.BlockSpec((tl, w), lambda i: (i, 0)),
                   pl.BlockSpec((tl, LANES), lambda i: (i, 0))],
        out_shape=[jax.ShapeDtypeStruct((t, w), F32),
                   jax.ShapeDtypeStruct((t, LANES), F32)],
        compiler_params=_cparams(("parallel",)),
        name="mlstm_short_intra",
    )(proj, proj, proj, g, gt)


def _mlstm_state_head(h, q_ref, k_ref, v_ref, num_ref, g_ref, stat_ref, c0_ref, n0_ref, hb_ref, c_ref, n_ref):
    kscale = MLSTM_HEAD_DIM ** -0.5
    seq_len = q_ref.shape[0]
    last = slice(seq_len - 1, seq_len)
    cs = slice(h * MLSTM_HEAD_DIM, (h + 1) * MLSTM_HEAD_DIM)
    qf = q_ref[:, cs]
    kf = k_ref[:, cs] * kscale
    vf = v_ref[:, cs]
    ig_col = g_ref[:, h:h + 1]
    b_col = g_ref[:, HEADS + h:HEADS + h + 1]
    b_last = g_ref[last, HEADS + h:HEADS + h + 1]
    m0 = g_ref[last, 2 * HEADS + h:2 * HEADS + h + 1]
    den_intra = stat_ref[:, STAT_DEN + h:STAT_DEN + h + 1]
    inter_w = stat_ref[:, STAT_IW + h:STAT_IW + h + 1]
    m_col = stat_ref[:, STAT_M + h:STAT_M + h + 1]
    m_new = stat_ref[last, STAT_M + h:STAT_M + h + 1]
    c0 = c0_ref[0, 0]
    n0 = n0_ref[0, h:h + 1, :]
    num = num_ref[:, cs] + inter_w * lax.dot_general(
        qf.astype(BF16), c0.astype(BF16), (((1,), (1,)), ((), ())), preferred_element_type=F32)
    den = den_intra + inter_w * jnp.sum(qf * n0, axis=1, keepdims=True)
    hval = num * (1.0 / jnp.maximum(jnp.abs(den), jnp.exp(-m_col)))
    hb_ref[:, cs] = hval
    w_col = jnp.exp(b_last - b_col + ig_col - m_new)
    decay = jnp.exp(b_last + m0 - m_new)
    dc = lax.dot_general((w_col * vf).astype(BF16), kf.astype(BF16), (((0,), (0,)), ((), ())),
                         preferred_element_type=F32)
    c_ref[0, h] = decay * c0 + dc
    n_ref[0, h:h + 1, :] = decay * n0 + jnp.sum(w_col * kf, axis=0, keepdims=True)


def _inproj_state_kernel(h_ref, w_ref, *refs):
    tok_refs, c0_refs, n0_ref = refs[:6], refs[6:6 + HEADS], refs[6 + HEADS]
    o_ref, out_state_refs = refs[7 + HEADS], refs[8 + HEADS:]
    half = w_ref.shape[0] // 2
    nt = (((1,), (1,)), ((), ()))
    o_ref[:, :half] = lax.dot_general(h_ref[...], w_ref[:half, :], nt, preferred_element_type=F32)
    for h in range(HEADS):
        _mlstm_state_head(h, *tok_refs, c0_refs[h], n0_ref, *out_state_refs)
    o_ref[:, half:] = lax.dot_general(h_ref[...], w_ref[half:, :], nt, preferred_element_type=F32)


def _inproj_with_state(h, w_bf, tm, tn, proj_s, num_s, g_s, stat_s, c0, n0, seq_len):
    t, d = h.shape
    n_cols = w_bf.shape[0]
    n_seq = c0.shape[0]
    n_inner = t // tm
    assert n_cols % tn == 0 and t % tm == 0 and (n_cols // tn) * n_inner == n_seq
    w = MLSTM_WIDTH
    hd = MLSTM_HEAD_DIM

    def seq(j, i):
        return j * n_inner + i

    def colspec(col0):
        return pl.BlockSpec((seq_len, w), lambda j, i: (seq(j, i), col0 // w))

    rowspec = pl.BlockSpec((seq_len, w), lambda j, i: (seq(j, i), 0))
    lanespec = pl.BlockSpec((seq_len, LANES), lambda j, i: (seq(j, i), 0))
    state4 = pl.BlockSpec((1, HEADS, hd, hd), lambda j, i: (seq(j, i), 0, 0, 0))
    state3 = pl.BlockSpec((1, HEADS, hd), lambda j, i: (seq(j, i), 0, 0))
    return pl.pallas_call(
        _inproj_state_kernel,
        grid=(n_cols // tn, n_inner),
        in_specs=[pl.BlockSpec((tm, d), lambda j, i: (i, 0)),
                  pl.BlockSpec((tn, d), lambda j, i: (j, 0)),
                  colspec(COL_Q), colspec(COL_K), colspec(COL_V),
                  rowspec, lanespec, lanespec,
                  *[pl.BlockSpec((1, 1, hd, hd), functools.partial(lambda j, i, hh: (seq(j, i), hh, 0, 0), hh=hh))
                    for hh in range(HEADS)],
                  state3],
        out_specs=[pl.BlockSpec((tm, tn), lambda j, i: (i, j)), rowspec, state4, state3],
        out_shape=[jax.ShapeDtypeStruct((t, n_cols), F32),
                   jax.ShapeDtypeStruct((n_seq * seq_len, w), F32),
                   jax.ShapeDtypeStruct((n_seq, HEADS, hd, hd), F32),
                   jax.ShapeDtypeStruct((n_seq, HEADS, hd), F32)],
        compiler_params=_cparams(("arbitrary", "arbitrary")),
        name="inproj_state",
    )(h, w_bf, proj_s, proj_s, proj_s, num_s, g_s, stat_s, *([c0] * HEADS), n0)


def _resident(shape):
    return pl.BlockSpec(shape, lambda i: (0, 0), pipeline_mode=pl.Buffered(1))


def _merge_kernel(xa_ref, hb_ref, o_ref, z_ref, xc_ref, ga_ref, gb_ref, gc_ref, wa_ref, wb_ref, wc_ref, m_ref):
    ya = jnp.dot(xa_ref[...].astype(BF16), wa_ref[...], preferred_element_type=F32)
    merged = _sigmoid(ga_ref[...]) * ya
    xb = _sigmoid(o_ref[...]) * hb_ref[...] * _silu(z_ref[...])
    yb = jnp.dot(xb.astype(BF16), wb_ref[...], preferred_element_type=F32)
    merged = merged + _sigmoid(gb_ref[...]) * yb
    yc = jnp.dot(xc_ref[...].astype(BF16), wc_ref[...], preferred_element_type=F32)
    merged = merged + _sigmoid(gc_ref[...]) * yc
    m_ref[...] = merged.astype(m_ref.dtype)


def _out_kernel(m_ref, x_ref, gpost_ref, wo_ref, y_ref):
    out = jnp.dot(m_ref[...], wo_ref[...], preferred_element_type=F32)
    ms = jnp.mean(out * out, axis=-1, keepdims=True)
    y_ref[...] = x_ref[...] + out * lax.rsqrt(ms + EPS) * gpost_ref[...]


def _final(xa, hb, xc, proj_a, proj_b, x2d, gpost_row, wa, wb, wc, wo, tm_merge, tm_out):
    t, d = x2d.shape

    def rowspec(tm, width, colblk=0):
        return pl.BlockSpec((tm, width), lambda i: (i, colblk))

    tm = tm_merge
    merged = pl.pallas_call(
        _merge_kernel,
        grid=(t // tm,),
        in_specs=[rowspec(tm, POOL_WIDTH), rowspec(tm, MLSTM_WIDTH),
                  rowspec(tm, MLSTM_WIDTH, COL_O // MLSTM_WIDTH), rowspec(tm, MLSTM_WIDTH, COL_Z // MLSTM_WIDTH),
                  rowspec(tm, MEM_WIDTH),
                  rowspec(tm, d, COL_GA // d), rowspec(tm, d, COL_GB // d), rowspec(tm, d, COL_GC // d),
                  _resident(wa.shape), _resident(wb.shape), _resident(wc.shape)],
        out_specs=rowspec(tm, d),
        out_shape=jax.ShapeDtypeStruct((t, d), BF16),
        compiler_params=_cparams(("parallel",)),
        name="merge",
    )(xa, hb, proj_a, proj_a, xc, proj_b, proj_b, proj_b, wa, wb, wc)
    tm = tm_out
    return pl.pallas_call(
        _out_kernel,
        grid=(t // tm,),
        in_specs=[rowspec(tm, d), rowspec(tm, d), _resident((1, d)), _resident(wo.shape)],
        out_specs=rowspec(tm, d),
        out_shape=jax.ShapeDtypeStruct((t, d), F32),
        compiler_params=_cparams(("parallel",)),
        name="out_proj",
    )(merged, x2d, gpost_row, wo)


B_ROW0 = GATE_COL0 + N_GATE_COLS


def _pool_and_merge(x2d, nb, seq_len, proj_a, proj_b, xb, xc, pool_buf, wts, start):
    if seq_len >= 512:
        xa, new_buf = _pool_branch(proj_a, pool_buf, wts["w_pool_grp"], wts["pool_scale"], nb, seq_len,
                                   nseq=1, tm=512, start=start)
    else:
        xa, new_buf = _pool_branch(proj_a, pool_buf, wts["w_pool_grp"], wts["pool_scale"], nb, seq_len,
                                   nseq=32, tm=seq_len, start=start)
    y = _final(xa, xb, xc, proj_a, proj_b, x2d, wts["g_post"], wts["w_br_pool"], wts["w_br_mlstm"],
               wts["w_br_mem"], wts["w_out"], tm_merge=256, tm_out=512)
    return y.reshape(nb, seq_len, x2d.shape[1]), new_buf


def kernel(x_prompt, x_sample, state_pool, state_mlstm_C, state_mlstm_n, state_mlstm_m, cache_mem_k,
           cache_mem_v, mem_prompt, g_pre, g_post, w_in, b_mlstm_i, b_mlstm_f, w_pool_grp, pool_scale,
           g_mem, w_mem_kv, w_br_pool, w_br_mlstm, w_br_mem, w_out):
    nbp = x_prompt.shape[0]
    assert w_in.shape[0] == 1, "single-layer problem: the kernels index layer 0 of the stacked weights"
    l = 0
    wts = {
        "g_pre": g_pre[l][None, :],
        "g_post": g_post[l][None, :],
        "w_in": jnp.transpose(w_in[l]),
        "gate_bias": jnp.pad(jnp.concatenate([b_mlstm_i[l], b_mlstm_f[l]]), (0, LANES - N_GATE_COLS))[None, :],
        "w_pool_grp": w_pool_grp[l].astype(BF16),
        "pool_scale": pool_scale[l][None, :],
        "w_br_pool": w_br_pool[l].astype(BF16),
        "w_br_mlstm": w_br_mlstm[l].astype(BF16),
        "w_br_mem": w_br_mem[l].astype(BF16),
        "w_out": w_out[l].astype(BF16),
    }
    mk2d, mv2d = _mem_kv(mem_prompt.reshape(nbp * MEM_TOKENS, D_MODEL), g_mem[l][None, :],
                         w_mem_kv[l].astype(BF16), tm=256)
    mk3, mv3 = mk2d.reshape(nbp, MEM_TOKENS, MEM_WIDTH), mv2d.reshape(nbp, MEM_TOKENS, MEM_WIDTH)
    w_t = wts["w_in"]

    nbs, ls, d = x_sample.shape
    ts = nbs * ls
    xs2d = x_sample.reshape(ts, d)
    m0_rows = jnp.pad(jnp.repeat(state_mlstm_m[l], ls, axis=0), ((0, 0), (2 * HEADS, LANES - 3 * HEADS)))
    hs, gs, gts = _norm_gate_prep(xs2d, wts["g_pre"], w_t, wts["gate_bias"], m0_rows, ls)
    proj_a_s, w_a_bf = _inproj(hs, w_t, 0, N_PROJ_A, tm=ts, tn=1024)
    proj_b_s, w_b_bf = _inproj(hs, w_t, B_ROW0, N_PROJ_B, tm=ts, tn=1024)
    num_s, stat_s = _mlstm_short_intra(proj_a_s, gs, gts, ls)
    m_s = stat_s.reshape(nbs, ls, LANES)[:, ls - 1, STAT_M:STAT_M + HEADS]

    _, lp, _ = x_prompt.shape
    tp = nbp * lp
    xp2d = x_prompt.reshape(tp, d)
    hp, gp, gtp = _norm_gate_prep(xp2d, wts["g_pre"], w_t, wts["gate_bias"], jnp.zeros((tp, LANES), F32),
                                  PROMPT_CHUNK)
    proj_a_p, xb_s, c_s, n_s = _inproj_with_state(hp, w_a_bf, 256, 3072, proj_a_s, num_s, gs, stat_s,
                                                  state_mlstm_C[l], state_mlstm_n[l], ls)
    proj_b_p, xc_s = _inproj_with_attn(hp, w_b_bf, 1024, 1024, proj_b_s, cache_mem_k[l], cache_mem_v[l], ls)
    xb_p, c_p, n_p, mcol_p = _mlstm_prompt(proj_a_p, gp, gtp, nbp, lp)
    m_p = mcol_p.reshape(nbp, lp, LANES)[:, lp - 1, :HEADS]
    xc_p = _mem_attn(proj_b_p, mk3, mv3, nbp, lp, nseq=1, tm=512)

    yp, pool_p = _pool_and_merge(xp2d, nbp, lp, proj_a_p, proj_b_p, xb_p, xc_p,
                                 jnp.zeros((nbp, POOL_BUF, POOL_WIDTH), F32), wts, 0)
    ys, pool_s = _pool_and_merge(xs2d, nbs, ls, proj_a_s, proj_b_s, xb_s, xc_s, state_pool[l], wts, PAST_LEN)
    kv_shape = (1, nbp, MEM_TOKENS, HEADS, MEM_HEAD_DIM)
    return (yp, ys, pool_p[None], c_p[None], n_p[None], m_p[None],
            mk2d.reshape(kv_shape), mv2d.reshape(kv_shape),
            pool_s[None], c_s[None], n_s[None], m_s[None])
```

```python
import functools

import jax
import jax.numpy as jnp
from jax import lax
from jax.experimental import pallas as pl
from jax.experimental.pallas import tpu as pltpu

F32 = jnp.float32
BF16 = jnp.bfloat16

D_MODEL = 2048
PAST_LEN = 16384
POOL_WINDOWS = (2, 4, 8, 16)
POOL_GROUP_WIDTH = 256
POOL_WIDTH = 1024
POOL_BUF = 15
HEADS = 4
MLSTM_HEAD_DIM = 512
MLSTM_WIDTH = 2048
PROMPT_CHUNK = 512
MEM_TOKENS = 256
MEM_HEAD_DIM = 256
MEM_WIDTH = 1024
EPS = 1e-6
N_GATE_COLS = 2 * HEADS
GATE_COL0 = 2 * POOL_WIDTH + 5 * MLSTM_WIDTH
LANES = 128
SUBLANES = 8
COL_PV, COL_PZ = 0, 1024
COL_Q, COL_K, COL_V, COL_O, COL_Z = 2048, 4096, 6144, 8192, 10240
COL_CQ, COL_CZ = 0, 1024
COL_GA, COL_GB, COL_GC = 2048, 4096, 6144
N_PROJ_A = GATE_COL0
N_PROJ_B = 2 * MEM_WIDTH + 3 * D_MODEL
VMEM_LIMIT = 56 * 1024 * 1024


def _cparams(sem):
    return pltpu.CompilerParams(dimension_semantics=sem, vmem_limit_bytes=VMEM_LIMIT)


def _sigmoid(x):
    return 0.5 * jnp.tanh(0.5 * x) + 0.5


def _silu(x):
    return x * _sigmoid(x)


def _inproj_kernel(h_ref, w_ref, o_ref, wbf_ref):
    @pl.when(pl.program_id(1) == 0)
    def _():
        wbf_ref[...] = w_ref[...].astype(BF16)

    o_ref[...] = lax.dot_general(h_ref[...], wbf_ref[...], (((1,), (1,)), ((), ())), preferred_element_type=F32)


def _inproj(h, w_t, row0, n_cols, tm, tn):
    t, d = h.shape
    assert row0 % SUBLANES == 0 and n_cols % tn == 0 and t % tm == 0 and row0 + n_cols <= w_t.shape[0]
    return pl.pallas_call(
        _inproj_kernel,
        grid=(n_cols // tn, t // tm),
        in_specs=[pl.BlockSpec((tm, d), lambda j, i: (i, 0)),
                  pl.BlockSpec((pl.Element(tn), pl.Element(d)),
                               lambda j, i: (pl.multiple_of(row0 + j * tn, SUBLANES), 0))],
        out_specs=[pl.BlockSpec((tm, tn), lambda j, i: (i, j)),
                   pl.BlockSpec((tn, d), lambda j, i: (j, 0))],
        out_shape=[jax.ShapeDtypeStruct((t, n_cols), F32),
                   jax.ShapeDtypeStruct((n_cols, d), BF16)],
        compiler_params=_cparams(("arbitrary", "arbitrary")),
        name="inproj",
    )(h, w_t)


GATE_TILE = 512


def _gate_kernel(x_ref, gpre_ref, wg_ref, bias_ref, m0_ref, h_ref, g_ref, gt_ref, *, seg, ct):
    x = x_ref[...]
    ms = jnp.mean(x * x, axis=-1, keepdims=True)
    h_ref[...] = (x * lax.rsqrt(ms + EPS) * gpre_ref[...]).astype(h_ref.dtype)
    src = lax.broadcasted_iota(jnp.int32, (ct, ct), 0)
    dst = lax.broadcasted_iota(jnp.int32, (ct, ct), 1)
    upto = src <= dst
    if seg < ct:
        shift = seg.bit_length() - 1
        upto = upto & (lax.shift_right_logical(src, shift) == lax.shift_right_logical(dst, shift))
    upto = upto.astype(F32)
    lane = lax.broadcasted_iota(jnp.int32, (ct, LANES), 1)
    gate_row = lax.broadcasted_iota(jnp.int32, (2 * HEADS, ct), 0)
    wg = wg_ref[...].astype(BF16)
    for r in range(GATE_TILE // ct):
        rs = slice(r * ct, (r + 1) * ct)
        x = lax.dot_general(h_ref[rs, :], wg, (((1,), (1,)), ((), ())), preferred_element_type=F32) + bias_ref[...]
        xt = x.T[:2 * HEADS, :]
        lf = jnp.minimum(xt, 0.0) - jnp.log1p(jnp.exp(-jnp.abs(xt)))
        cums = jnp.dot(lf, upto, precision=lax.Precision.HIGHEST, preferred_element_type=F32)
        gt_ref[:, rs] = jnp.where(gate_row < HEADS, xt, cums)
        cums_col = jnp.concatenate([cums, jnp.zeros((LANES - 2 * HEADS, ct), F32)], axis=0).T
        g_ref[rs, :] = jnp.where(lane < HEADS, x, jnp.where(lane < 2 * HEADS, cums_col, m0_ref[rs, :]))


def _norm_gate_prep(x2d, gpre_row, w_t, bias_row, m0_rows, seg):
    t, d = x2d.shape
    tg = GATE_TILE
    ct = max(seg, LANES)
    assert GATE_COL0 % LANES == 0 and tg % ct == 0 and ct % seg == 0 and seg & (seg - 1) == 0
    return pl.pallas_call(
        functools.partial(_gate_kernel, seg=seg, ct=ct),
        grid=(t // tg,),
        in_specs=[pl.BlockSpec((tg, d), lambda i: (i, 0)),
                  pl.BlockSpec((1, d), lambda i: (0, 0)),
                  pl.BlockSpec((LANES, d), lambda i: (GATE_COL0 // LANES, 0)),
                  pl.BlockSpec((1, LANES), lambda i: (0, 0)),
                  pl.BlockSpec((tg, LANES), lambda i: (i, 0))],
        out_specs=[pl.BlockSpec((tg, d), lambda i: (i, 0)),
                   pl.BlockSpec((tg, LANES), lambda i: (i, 0)),
                   pl.BlockSpec((8, tg), lambda i: (0, i))],
        out_shape=[jax.ShapeDtypeStruct((t, d), BF16),
                   jax.ShapeDtypeStruct((t, LANES), F32),
                   jax.ShapeDtypeStruct((8, t), F32)],
        compiler_params=_cparams(("parallel",)),
        name="norm_gate_prep",
    )(x2d, gpre_row, w_t, bias_row, m0_rows)


def _pool_kernel(pv_ref, pz_ref, buf_ref, wg_ref, scale_ref, xa_ref, nbuf_ref, s_ref, p_ref,
                 *, nseq, tm, nt, start):
    j = pl.program_id(1)

    def one_seq(b, r0):
        if nt > 1:
            @pl.when(j == 0)
            def _():
                s_ref[1:16, :] = buf_ref[b]
        else:
            s_ref[1:16, :] = buf_ref[b]
        s_ref[16:16 + tm, :] = pv_ref[pl.ds(r0, tm), :]
        pos = start + j * tm + lax.broadcasted_iota(jnp.int32, (tm, 1), 0)
        for g, w in enumerate(POOL_WINDOWS):
            cs = slice(g * POOL_GROUP_WIDTH, (g + 1) * POOL_GROUP_WIDTH)
            acc = s_ref[16:16 + tm, cs]
            for i in range(1, w):
                acc = acc + s_ref[16 - i:16 - i + tm, cs]
            inv_cnt = 1.0 / jnp.minimum(w, pos + 1).astype(F32)
            p_ref[pl.ds(r0, tm), cs] = acc * inv_cnt - s_ref[16:16 + tm, cs]
        tail = s_ref[tm + 1:tm + 16, :]
        nbuf_ref[b] = tail
        if nt > 1:
            s_ref[1:16, :] = tail

    if nseq == 1:
        one_seq(0, 0)
    else:
        def body(b, carry):
            one_seq(b, pl.multiple_of(b * tm, 8))
            return carry
        lax.fori_loop(0, nseq, body, 0)

    for g in range(len(POOL_WINDOWS)):
        cs = slice(g * POOL_GROUP_WIDTH, (g + 1) * POOL_GROUP_WIDTH)
        pa = jnp.dot(p_ref[:, cs].astype(BF16), wg_ref[g], preferred_element_type=F32) * scale_ref[:, cs]
        xa_ref[:, cs] = (pa * _silu(pz_ref[:, cs])).astype(xa_ref.dtype)


def _pool_branch(proj, buf, w_grp, scale_row, nb, seq_len, nseq, tm, start):
    assert nseq == 1 or tm == seq_len
    nt = seq_len // tm
    rows = nseq * tm
    t = nb * seq_len
    pvb, pzb = COL_PV // POOL_WIDTH, COL_PZ // POOL_WIDTH
    return pl.pallas_call(
        functools.partial(_pool_kernel, nseq=nseq, tm=tm, nt=nt, start=start),
        grid=(nb // nseq, nt),
        in_specs=[pl.BlockSpec((rows, POOL_WIDTH), lambda i, j: (i * nt + j, pvb)),
                  pl.BlockSpec((rows, POOL_WIDTH), lambda i, j: (i * nt + j, pzb)),
                  pl.BlockSpec((nseq, POOL_BUF, POOL_WIDTH), lambda i, j: (i, 0, 0)),
                  pl.BlockSpec((4, POOL_GROUP_WIDTH, POOL_GROUP_WIDTH), lambda i, j: (0, 0, 0)),
                  pl.BlockSpec((1, POOL_WIDTH), lambda i, j: (0, 0))],
        out_specs=[pl.BlockSpec((rows, POOL_WIDTH), lambda i, j: (i * nt + j, 0)),
                   pl.BlockSpec((nseq, POOL_BUF, POOL_WIDTH), lambda i, j: (i, 0, 0))],
        out_shape=[jax.ShapeDtypeStruct((t, POOL_WIDTH), BF16),
                   jax.ShapeDtypeStruct((nb, POOL_BUF, POOL_WIDTH), F32)],
        scratch_shapes=[pltpu.VMEM((16 + tm, POOL_WIDTH), F32),
                        pltpu.VMEM((rows, POOL_WIDTH), F32)],
        compiler_params=_cparams(("arbitrary", "arbitrary")),
        name="pool_branch",
    )(proj, proj, buf, w_grp, scale_row)


def _memkv_kernel(m_ref, g_ref, w_ref, k_ref, v_ref):
    x = m_ref[...]
    ms = jnp.mean(x * x, axis=-1, keepdims=True)
    h = (x * lax.rsqrt(ms + EPS) * g_ref[...]).astype(BF16)
    kv = jnp.dot(h, w_ref[...], preferred_element_type=F32)
    k_ref[...] = kv[:, :MEM_WIDTH]
    v_ref[...] = kv[:, MEM_WIDTH:]


def _mem_kv(mem2d, g_row, w_bf, tm):
    t, d = mem2d.shape
    return pl.pallas_call(
        _memkv_kernel,
        grid=(t // tm,),
        in_specs=[pl.BlockSpec((tm, d), lambda i: (i, 0)),
                  pl.BlockSpec((1, d), lambda i: (0, 0)),
                  pl.BlockSpec((d, 2 * MEM_WIDTH), lambda i: (0, 0))],
        out_specs=[pl.BlockSpec((tm, MEM_WIDTH), lambda i: (i, 0)),
                   pl.BlockSpec((tm, MEM_WIDTH), lambda i: (i, 0))],
        out_shape=[jax.ShapeDtypeStruct((t, MEM_WIDTH), F32),
                   jax.ShapeDtypeStruct((t, MEM_WIDTH), F32)],
        compiler_params=_cparams(("parallel",)),
        name="mem_kv",
    )(mem2d, g_row, w_bf)


def _attn_kernel(cq_ref, cz_ref, k_ref, v_ref, xc_ref, *, nseq, tm):
    scale = MEM_HEAD_DIM ** -0.5
    for b in range(nseq):
        rs = slice(b * tm, (b + 1) * tm)
        for h in range(HEADS):
            cs = slice(h * MEM_HEAD_DIM, (h + 1) * MEM_HEAD_DIM)
            q = cq_ref[rs, cs].astype(BF16)
            k = k_ref[b, :, cs].astype(BF16)
            v = v_ref[b, :, cs].astype(BF16)
            s = lax.dot_general(q, k, (((1,), (1,)), ((), ())), preferred_element_type=F32) * scale
            p = jnp.exp(s - jnp.max(s, axis=-1, keepdims=True))
            a = p * (1.0 / jnp.sum(p, axis=-1, keepdims=True))
            o = jnp.dot(a.astype(BF16), v, preferred_element_type=F32)
            xc_ref[rs, cs] = (o * _silu(cz_ref[rs, cs])).astype(xc_ref.dtype)


def _mem_attn(proj, mem_k, mem_v, nb, seq_len, nseq, tm):
    assert nseq == 1 or tm == seq_len
    nt = seq_len // tm
    rows = nseq * tm
    t = nb * seq_len
    cqb, czb = COL_CQ // MEM_WIDTH, COL_CZ // MEM_WIDTH
    return pl.pallas_call(
        functools.partial(_attn_kernel, nseq=nseq, tm=tm),
        grid=(nb // nseq, nt),
        in_specs=[pl.BlockSpec((rows, MEM_WIDTH), lambda i, j: (i * nt + j, cqb)),
                  pl.BlockSpec((rows, MEM_WIDTH), lambda i, j: (i * nt + j, czb)),
                  pl.BlockSpec((nseq, MEM_TOKENS, MEM_WIDTH), lambda i, j: (i, 0, 0)),
                  pl.BlockSpec((nseq, MEM_TOKENS, MEM_WIDTH), lambda i, j: (i, 0, 0))],
        out_specs=pl.BlockSpec((rows, MEM_WIDTH), lambda i, j: (i * nt + j, 0)),
        out_shape=jax.ShapeDtypeStruct((t, MEM_WIDTH), BF16),
        compiler_params=_cparams(("parallel", "arbitrary")),
        name="mem_attn",
    )(proj, proj, mem_k, mem_v)


def _attn_short_weights(cq_ref, k_ref, b, tm):
    scale = MEM_HEAD_DIM ** -0.5
    nr, nc = tm * HEADS, MEM_TOKENS * HEADS
    row_head = lax.broadcasted_iota(jnp.int32, (nr, nc), 0) // tm
    col_head = lax.broadcasted_iota(jnp.int32, (nr, nc), 1) % HEADS
    rs = slice(b * tm, (b + 1) * tm)
    kf = k_ref[b].reshape(nc, MEM_HEAD_DIM).astype(BF16)
    q = jnp.concatenate([cq_ref[rs, h * MEM_HEAD_DIM:(h + 1) * MEM_HEAD_DIM] for h in range(HEADS)], axis=0)
    s = lax.dot_general(q.astype(BF16), kf, (((1,), (1,)), ((), ())), preferred_element_type=F32) * scale
    s = jnp.where(row_head == col_head, s, -jnp.inf)
    p = jnp.exp(s - jnp.max(s, axis=-1, keepdims=True))
    return (p * (1.0 / jnp.sum(p, axis=-1, keepdims=True))).astype(BF16)


def _attn_short_output(a, cz_ref, v_ref, xc_ref, b, tm):
    rs = slice(b * tm, (b + 1) * tm)
    vf = v_ref[b].reshape(MEM_TOKENS * HEADS, MEM_HEAD_DIM).astype(BF16)
    o = jnp.dot(a, vf, preferred_element_type=F32)
    for h in range(HEADS):
        cs = slice(h * MEM_HEAD_DIM, (h + 1) * MEM_HEAD_DIM)
        xc_ref[rs, cs] = o[h * tm:(h + 1) * tm, :] * _silu(cz_ref[rs, cs])


def _inproj_attn_kernel(h_ref, w_ref, cq_ref, cz_ref, k_ref, v_ref, o_ref, xc_ref, *, nseq, seq_len):
    half = w_ref.shape[0] // 2
    nt = (((1,), (1,)), ((), ()))
    a = [_attn_short_weights(cq_ref, k_ref, b, seq_len) for b in range(nseq)]
    o_ref[:, :half] = lax.dot_general(h_ref[...], w_ref[:half, :], nt, preferred_element_type=F32)
    for b in range(nseq):
        _attn_short_output(a[b], cz_ref, v_ref, xc_ref, b, seq_len)
    o_ref[:, half:] = lax.dot_general(h_ref[...], w_ref[half:, :], nt, preferred_element_type=F32)


def _inproj_with_attn(h, w_bf, tm, tn, proj_s, mem_k, mem_v, seq_len):
    t, d = h.shape
    n_cols = w_bf.shape[0]
    nb = mem_k.shape[0]
    n_inner = t // tm
    n_steps = (n_cols // tn) * n_inner
    nseq = nb // n_steps
    rows = nseq * seq_len
    assert n_cols % tn == 0 and t % tm == 0 and nseq * n_steps == nb and rows % SUBLANES == 0
    cqb, czb = COL_CQ // MEM_WIDTH, COL_CZ // MEM_WIDTH
    kv_spec = pl.BlockSpec((nseq, MEM_TOKENS, HEADS, MEM_HEAD_DIM), lambda j, i: (j * n_inner + i, 0, 0, 0))
    return pl.pallas_call(
        functools.partial(_inproj_attn_kernel, nseq=nseq, seq_len=seq_len),
        grid=(n_cols // tn, n_inner),
        in_specs=[pl.BlockSpec((tm, d), lambda j, i: (i, 0)),
                  pl.BlockSpec((tn, d), lambda j, i: (j, 0)),
                  pl.BlockSpec((rows, MEM_WIDTH), lambda j, i: (j * n_inner + i, cqb)),
                  pl.BlockSpec((rows, MEM_WIDTH), lambda j, i: (j * n_inner + i, czb)),
                  kv_spec, kv_spec],
        out_specs=[pl.BlockSpec((tm, tn), lambda j, i: (i, j)),
                   pl.BlockSpec((rows, MEM_WIDTH), lambda j, i: (j * n_inner + i, 0))],
        out_shape=[jax.ShapeDtypeStruct((t, n_cols), F32),
                   jax.ShapeDtypeStruct((nb * seq_len, MEM_WIDTH), F32)],
        compiler_params=_cparams(("arbitrary", "arbitrary")),
        name="inproj_attn",
    )(h, w_bf, proj_s, proj_s, mem_k, mem_v)


def _mlstm_intra(q_bf, ks_bf, v_bf, ig_row, b_row, b_col, m0_col, mask):
    dlog = jnp.where(mask, b_col - b_row + ig_row, -jnp.inf)
    inter = b_col + m0_col
    m_col = jnp.maximum(inter, jnp.max(dlog, axis=1, keepdims=True))
    dw = jnp.exp(dlog - m_col)
    inter_w = jnp.exp(inter - m_col)
    s = lax.dot_general(q_bf, ks_bf, (((1,), (1,)), ((), ())), preferred_element_type=F32) * dw
    num = jnp.dot(s.astype(BF16), v_bf, preferred_element_type=F32)
    den = jnp.sum(s, axis=1, keepdims=True)
    return num, den, m_col, inter_w


def _mlstm_prompt_kernel(q_ref, k_ref, v_ref, g_ref, gt_ref, hb_ref, c_ref, n_ref, mcol_ref, m_scr):
    @pl.when(pl.program_id(1) == 0)
    def _():
        c_ref[...] = jnp.zeros_like(c_ref)
        n_ref[...] = jnp.zeros_like(n_ref)
        m_scr[...] = jnp.zeros_like(m_scr)

    cl = PROMPT_CHUNK
    row = lax.broadcasted_iota(jnp.int32, (cl, cl), 0)
    col = lax.broadcasted_iota(jnp.int32, (cl, cl), 1)
    mask = col <= row
    kscale = MLSTM_HEAD_DIM ** -0.5
    mcol_ref[...] = jnp.zeros_like(mcol_ref)
    nt = (((1,), (1,)), ((), ()))
    gates = []
    for h in range(HEADS):
        ig_col = g_ref[:, h:h + 1]
        b_col = g_ref[:, HEADS + h:HEADS + h + 1]
        ig_row = gt_ref[h:h + 1, :]
        b_row = gt_ref[HEADS + h:HEADS + h + 1, :]
        m0 = m_scr[h:h + 1, 0:1]
        dlog = jnp.where(mask, b_col - b_row + ig_row, -jnp.inf)
        inter = b_col + m0
        m_col = jnp.maximum(inter, jnp.max(dlog, axis=1, keepdims=True))
        dw = jnp.exp(dlog - m_col)
        inter_w = jnp.exp(inter - m_col)
        m_new = m_col[cl - 1:cl, :]
        b_last = b_col[cl - 1:cl, :]
        w_col = jnp.exp(b_last - b_col + ig_col - m_new)
        decay = jnp.exp(b_last + m0 - m_new)
        m_scr[h:h + 1, :] = jnp.broadcast_to(m_new, (1, LANES))
        mcol_ref[:, h:h + 1] = m_col
        gates.append((m_col, dw, inter_w, w_col, decay))
    prods = []
    for h in range(HEADS):
        cs = slice(h * MLSTM_HEAD_DIM, (h + 1) * MLSTM_HEAD_DIM)
        w_col = gates[h][3]
        q_bf = q_ref[:, cs].astype(BF16)
        ks_bf = (k_ref[:, cs] * kscale).astype(BF16)
        qk = lax.dot_general(q_bf, ks_bf, nt, preferred_element_type=F32)
        inter_num = lax.dot_general(q_bf, c_ref[0, h].astype(BF16), nt, preferred_element_type=F32)
        dc = lax.dot_general((w_col * v_ref[:, cs]).astype(BF16), ks_bf, (((0,), (0,)), ((), ())),
                             preferred_element_type=F32)
        prods.append((qk, inter_num, dc))
    for h in range(HEADS):
        cs = slice(h * MLSTM_HEAD_DIM, (h + 1) * MLSTM_HEAD_DIM)
        m_col, dw, inter_w, w_col, decay = gates[h]
        qk, inter_num, dc = prods[h]
        n0 = n_ref[0, h:h + 1, :]
        s = qk * dw
        num = jnp.dot(s.astype(BF16), v_ref[:, cs].astype(BF16), preferred_element_type=F32) + inter_w * inter_num
        den = jnp.sum(s, axis=1, keepdims=True) + inter_w * jnp.sum(q_ref[:, cs] * n0, axis=1, keepdims=True)
        hb_ref[:, cs] = num * (1.0 / jnp.maximum(jnp.abs(den), jnp.exp(-m_col)))
        c_ref[0, h] = decay * c_ref[0, h] + dc
        n_ref[0, h:h + 1, :] = decay * n0 + jnp.sum(w_col * (k_ref[:, cs] * kscale), axis=0, keepdims=True)


def _mlstm_prompt(proj, g, gt, nb, seq_len):
    cl = PROMPT_CHUNK
    nc = seq_len // cl
    t = nb * seq_len
    w = MLSTM_WIDTH

    def colspec(col0):
        return pl.BlockSpec((cl, w), lambda b, c: (b * nc + c, col0 // w))

    return pl.pallas_call(
        _mlstm_prompt_kernel,
        grid=(nb, nc),
        in_specs=[colspec(COL_Q), colspec(COL_K), colspec(COL_V),
                  pl.BlockSpec((cl, LANES), lambda b, c: (b * nc + c, 0)),
                  pl.BlockSpec((8, cl), lambda b, c: (0, b * nc + c))],
        out_specs=[pl.BlockSpec((cl, w), lambda b, c: (b * nc + c, 0)),
                   pl.BlockSpec((1, HEADS, MLSTM_HEAD_DIM, MLSTM_HEAD_DIM), lambda b, c: (b, 0, 0, 0)),
                   pl.BlockSpec((1, HEADS, MLSTM_HEAD_DIM), lambda b, c: (b, 0, 0)),
                   pl.BlockSpec((cl, LANES), lambda b, c: (b * nc + c, 0))],
        out_shape=[jax.ShapeDtypeStruct((t, w), F32),
                   jax.ShapeDtypeStruct((nb, HEADS, MLSTM_HEAD_DIM, MLSTM_HEAD_DIM), F32),
                   jax.ShapeDtypeStruct((nb, HEADS, MLSTM_HEAD_DIM), F32),
                   jax.ShapeDtypeStruct((t, LANES), F32)],
        scratch_shapes=[pltpu.VMEM((8, LANES), F32)],
        compiler_params=_cparams(("arbitrary", "arbitrary")),
        name="mlstm_prompt",
    )(proj, proj, proj, g, gt)


STAT_DEN, STAT_IW, STAT_M = 0, HEADS, 2 * HEADS


def _mlstm_short_intra_kernel(q_ref, k_ref, v_ref, g_ref, gt_ref, num_ref, stat_ref, *, seq_len):
    tl = LANES
    kscale = MLSTM_HEAD_DIM ** -0.5
    shift = seq_len.bit_length() - 1
    row = lax.broadcasted_iota(jnp.int32, (tl, tl), 0)
    col = lax.broadcasted_iota(jnp.int32, (tl, tl), 1)
    mask = (col <= row) & (lax.shift_right_logical(col, shift) == lax.shift_right_logical(row, shift))
    stat_ref[...] = jnp.zeros_like(stat_ref)
    for h in range(HEADS):
        cs = slice(h * MLSTM_HEAD_DIM, (h + 1) * MLSTM_HEAD_DIM)
        b_col = g_ref[:, HEADS + h:HEADS + h + 1]
        m0_col = g_ref[:, 2 * HEADS + h:2 * HEADS + h + 1]
        ig_row = gt_ref[h:h + 1, :]
        b_row = gt_ref[HEADS + h:HEADS + h + 1, :]
        q_bf = q_ref[:, cs].astype(BF16)
        ks_bf = (k_ref[:, cs] * kscale).astype(BF16)
        num, den, m_col, inter_w = _mlstm_intra(q_bf, ks_bf, v_ref[:, cs].astype(BF16),
                                                ig_row, b_row, b_col, m0_col, mask)
        num_ref[:, cs] = num
        stat_ref[:, STAT_DEN + h:STAT_DEN + h + 1] = den
        stat_ref[:, STAT_IW + h:STAT_IW + h + 1] = inter_w
        stat_ref[:, STAT_M + h:STAT_M + h + 1] = m_col


def _mlstm_short_intra(proj, g, gt, seq_len):
    t = proj.shape[0]
    tl = LANES
    w = MLSTM_WIDTH
    assert tl % seq_len == 0 and t % tl == 0

    def colspec(col0):
        return pl.BlockSpec((tl, w), lambda i: (i, col0 // w))

    return pl.pallas_call(
        functools.partial(_mlstm_short_intra_kernel, seq_len=seq_len),
        grid=(t // tl,),
        in_specs=[colspec(COL_Q), colspec(COL_K), colspec(COL_V),
                  pl.BlockSpec((tl, LANES), lambda i: (i, 0)),
                  pl.BlockSpec((8, tl), lambda i: (0, i))],
        out_specs=[pl.BlockSpec((tl, w), lambda i: (i, 0)),
                   pl.BlockSpec((tl, LANES), lambda i: (i, 0))],
        out_shape=[jax.ShapeDtypeStruct((t, w), F32),
                   jax.ShapeDtypeStruct((t, LANES), F32)],
        compiler_params=_cparams(("parallel",)),
        name="mlstm_short_intra",
    )(proj, proj, proj, g, gt)


def _mlstm_state_head(h, q_ref, k_ref, v_ref, num_ref, g_ref, stat_ref, c0_ref, n0_ref, hb_ref, c_ref, n_ref):
    kscale = MLSTM_HEAD_DIM ** -0.5
    seq_len = q_ref.shape[0]
    last = slice(seq_len - 1, seq_len)
    cs = slice(h * MLSTM_HEAD_DIM, (h + 1) * MLSTM_HEAD_DIM)
    qf = q_ref[:, cs]
    kf = k_ref[:, cs] * kscale
    vf = v_ref[:, cs]
    ig_col = g_ref[:, h:h + 1]
    b_col = g_ref[:, HEADS + h:HEADS + h + 1]
    b_last = g_ref[last, HEADS + h:HEADS + h + 1]
    m0 = g_ref[last, 2 * HEADS + h:2 * HEADS + h + 1]
    den_intra = stat_ref[:, STAT_DEN + h:STAT_DEN + h + 1]
    inter_w = stat_ref[:, STAT_IW + h:STAT_IW + h + 1]
    m_col = stat_ref[:, STAT_M + h:STAT_M + h + 1]
    m_new = stat_ref[last, STAT_M + h:STAT_M + h + 1]
    c0 = c0_ref[0, 0]
    n0 = n0_ref[0, h:h + 1, :]
    num = num_ref[:, cs] + inter_w * lax.dot_general(
        qf.astype(BF16), c0.astype(BF16), (((1,), (1,)), ((), ())), preferred_element_type=F32)
    den = den_intra + inter_w * jnp.sum(qf * n0, axis=1, keepdims=True)
    hval = num * (1.0 / jnp.maximum(jnp.abs(den), jnp.exp(-m_col)))
    hb_ref[:, cs] = hval
    w_col = jnp.exp(b_last - b_col + ig_col - m_new)
    decay = jnp.exp(b_last + m0 - m_new)
    dc = lax.dot_general((w_col * vf).astype(BF16), kf.astype(BF16), (((0,), (0,)), ((), ())),
                         preferred_element_type=F32)
    c_ref[0, h] = decay * c0 + dc
    n_ref[0, h:h + 1, :] = decay * n0 + jnp.sum(w_col * kf, axis=0, keepdims=True)


def _inproj_state_kernel(h_ref, w_ref, *refs):
    tok_refs, c0_refs, n0_ref = refs[:6], refs[6:6 + HEADS], refs[6 + HEADS]
    o_ref, out_state_refs = refs[7 + HEADS], refs[8 + HEADS:]
    half = w_ref.shape[0] // 2
    nt = (((1,), (1,)), ((), ()))
    o_ref[:, :half] = lax.dot_general(h_ref[...], w_ref[:half, :], nt, preferred_element_type=F32)
    for h in range(HEADS):
        _mlstm_state_head(h, *tok_refs, c0_refs[h], n0_ref, *out_state_refs)
    o_ref[:, half:] = lax.dot_general(h_ref[...], w_ref[half:, :], nt, preferred_element_type=F32)


def _inproj_with_state(h, w_bf, tm, tn, proj_s, num_s, g_s, stat_s, c0, n0, seq_len):
    t, d = h.shape
    n_cols = w_bf.shape[0]
    n_seq = c0.shape[0]
    n_inner = t // tm
    assert n_cols % tn == 0 and t % tm == 0 and (n_cols // tn) * n_inner == n_seq
    w = MLSTM_WIDTH
    hd = MLSTM_HEAD_DIM

    def seq(j, i):
        return j * n_inner + i

    def colspec(col0):
        return pl.BlockSpec((seq_len, w), lambda j, i: (seq(j, i), col0 // w))

    rowspec = pl.BlockSpec((seq_len, w), lambda j, i: (seq(j, i), 0))
    lanespec = pl.BlockSpec((seq_len, LANES), lambda j, i: (seq(j, i), 0))
    state4 = pl.BlockSpec((1, HEADS, hd, hd), lambda j, i: (seq(j, i), 0, 0, 0))
    state3 = pl.BlockSpec((1, HEADS, hd), lambda j, i: (seq(j, i), 0, 0))
    return pl.pallas_call(
        _inproj_state_kernel,
        grid=(n_cols // tn, n_inner),
        in_specs=[pl.BlockSpec((tm, d), lambda j, i: (i, 0)),
                  pl.BlockSpec((tn, d), lambda j, i: (j, 0)),
                  colspec(COL_Q), colspec(COL_K), colspec(COL_V),
                  rowspec, lanespec, lanespec,
                  *[pl.BlockSpec((1, 1, hd, hd), functools.partial(lambda j, i, hh: (seq(j, i), hh, 0, 0), hh=hh))
                    for hh in range(HEADS)],
                  state3],
        out_specs=[pl.BlockSpec((tm, tn), lambda j, i: (i, j)), rowspec, state4, state3],
        out_shape=[jax.ShapeDtypeStruct((t, n_cols), F32),
                   jax.ShapeDtypeStruct((n_seq * seq_len, w), F32),
                   jax.ShapeDtypeStruct((n_seq, HEADS, hd, hd), F32),
                   jax.ShapeDtypeStruct((n_seq, HEADS, hd), F32)],
        compiler_params=_cparams(("arbitrary", "arbitrary")),
        name="inproj_state",
    )(h, w_bf, proj_s, proj_s, proj_s, num_s, g_s, stat_s, *([c0] * HEADS), n0)


def _resident(shape):
    return pl.BlockSpec(shape, lambda i: (0, 0), pipeline_mode=pl.Buffered(1))


def _merge_kernel(xa_ref, hb_ref, o_ref, z_ref, xc_ref, ga_ref, gb_ref, gc_ref, wa_ref, wb_ref, wc_ref, m_ref):
    ya = jnp.dot(xa_ref[...].astype(BF16), wa_ref[...], preferred_element_type=F32)
    merged = _sigmoid(ga_ref[...]) * ya
    xb = _sigmoid(o_ref[...]) * hb_ref[...] * _silu(z_ref[...])
    yb = jnp.dot(xb.astype(BF16), wb_ref[...], preferred_element_type=F32)
    merged = merged + _sigmoid(gb_ref[...]) * yb
    yc = jnp.dot(xc_ref[...].astype(BF16), wc_ref[...], preferred_element_type=F32)
    merged = merged + _sigmoid(gc_ref[...]) * yc
    m_ref[...] = merged.astype(m_ref.dtype)


def _out_kernel(m_ref, x_ref, gpost_ref, wo_ref, y_ref):
    out = jnp.dot(m_ref[...], wo_ref[...], preferred_element_type=F32)
    ms = jnp.mean(out * out, axis=-1, keepdims=True)
    y_ref[...] = x_ref[...] + out * lax.rsqrt(ms + EPS) * gpost_ref[...]


def _final(xa, hb, xc, proj_a, proj_b, x2d, gpost_row, wa, wb, wc, wo, tm_merge, tm_out):
    t, d = x2d.shape

    def rowspec(tm, width, colblk=0):
        return pl.BlockSpec((tm, width), lambda i: (i, colblk))

    tm = tm_merge
    merged = pl.pallas_call(
        _merge_kernel,
        grid=(t // tm,),
        in_specs=[rowspec(tm, POOL_WIDTH), rowspec(tm, MLSTM_WIDTH),
                  rowspec(tm, MLSTM_WIDTH, COL_O // MLSTM_WIDTH), rowspec(tm, MLSTM_WIDTH, COL_Z // MLSTM_WIDTH),
                  rowspec(tm, MEM_WIDTH),
                  rowspec(tm, d, COL_GA // d), rowspec(tm, d, COL_GB // d), rowspec(tm, d, COL_GC // d),
                  _resident(wa.shape), _resident(wb.shape), _resident(wc.shape)],
        out_specs=rowspec(tm, d),
        out_shape=jax.ShapeDtypeStruct((t, d), BF16),
        compiler_params=_cparams(("parallel",)),
        name="merge",
    )(xa, hb, proj_a, proj_a, xc, proj_b, proj_b, proj_b, wa, wb, wc)
    tm = tm_out
    return pl.pallas_call(
        _out_kernel,
        grid=(t // tm,),
        in_specs=[rowspec(tm, d), rowspec(tm, d), _resident((1, d)), _resident(wo.shape)],
        out_specs=rowspec(tm, d),
        out_shape=jax.ShapeDtypeStruct((t, d), F32),
        compiler_params=_cparams(("parallel",)),
        name="out_proj",
    )(merged, x2d, gpost_row, wo)


B_ROW0 = GATE_COL0 + N_GATE_COLS


def _pool_and_merge(x2d, nb, seq_len, proj_a, proj_b, xb, xc, pool_buf, wts, start):
    if seq_len >= 512:
        xa, new_buf = _pool_branch(proj_a, pool_buf, wts["w_pool_grp"], wts["pool_scale"], nb, seq_len,
                                   nseq=1, tm=512, start=start)
    else:
        xa, new_buf = _pool_branch(proj_a, pool_buf, wts["w_pool_grp"], wts["pool_scale"], nb, seq_len,
                                   nseq=32, tm=seq_len, start=start)
    y = _final(xa, xb, xc, proj_a, proj_b, x2d, wts["g_post"], wts["w_br_pool"], wts["w_br_mlstm"],
               wts["w_br_mem"], wts["w_out"], tm_merge=256, tm_out=512)
    return y.reshape(nb, seq_len, x2d.shape[1]), new_buf


def kernel(x_prompt, x_sample, state_pool, state_mlstm_C, state_mlstm_n, state_mlstm_m, cache_mem_k,
           cache_mem_v, mem_prompt, g_pre, g_post, w_in, b_mlstm_i, b_mlstm_f, w_pool_grp, pool_scale,
           g_mem, w_mem_kv, w_br_pool, w_br_mlstm, w_br_mem, w_out):
    nbp = x_prompt.shape[0]
    assert w_in.shape[0] == 1, "single-layer problem: the kernels index layer 0 of the stacked weights"
    l = 0
    wts = {
        "g_pre": g_pre[l][None, :],
        "g_post": g_post[l][None, :],
        "w_in": jnp.transpose(w_in[l]),
        "gate_bias": jnp.pad(jnp.concatenate([b_mlstm_i[l], b_mlstm_f[l]]), (0, LANES - N_GATE_COLS))[None, :],
        "w_pool_grp": w_pool_grp[l].astype(BF16),
        "pool_scale": pool_scale[l][None, :],
        "w_br_pool": w_br_pool[l].astype(BF16),
        "w_br_mlstm": w_br_mlstm[l].astype(BF16),
        "w_br_mem": w_br_mem[l].astype(BF16),
        "w_out": w_out[l].astype(BF16),
    }
    mk2d, mv2d = _mem_kv(mem_prompt.reshape(nbp * MEM_TOKENS, D_MODEL), g_mem[l][None, :],
                         w_mem_kv[l].astype(BF16), tm=256)
    mk3, mv3 = mk2d.reshape(nbp, MEM_TOKENS, MEM_WIDTH), mv2d.reshape(nbp, MEM_TOKENS, MEM_WIDTH)
    w_t = wts["w_in"]

    nbs, ls, d = x_sample.shape
    ts = nbs * ls
    xs2d = x_sample.reshape(ts, d)
    m0_rows = jnp.pad(jnp.repeat(state_mlstm_m[l], ls, axis=0), ((0, 0), (2 * HEADS, LANES - 3 * HEADS)))
    hs, gs, gts = _norm_gate_prep(xs2d, wts["g_pre"], w_t, wts["gate_bias"], m0_rows, ls)
    proj_a_s, w_a_bf = _inproj(hs, w_t, 0, N_PROJ_A, tm=ts, tn=1024)
    proj_b_s, w_b_bf = _inproj(hs, w_t, B_ROW0, N_PROJ_B, tm=ts, tn=1024)
    num_s, stat_s = _mlstm_short_intra(proj_a_s, gs, gts, ls)
    m_s = stat_s.reshape(nbs, ls, LANES)[:, ls - 1, STAT_M:STAT_M + HEADS]

    _, lp, _ = x_prompt.shape
    tp = nbp * lp
    xp2d = x_prompt.reshape(tp, d)
    hp, gp, gtp = _norm_gate_prep(xp2d, wts["g_pre"], w_t, wts["gate_bias"], jnp.zeros((tp, LANES), F32),
                                  PROMPT_CHUNK)
    proj_a_p, xb_s, c_s, n_s = _inproj_with_state(hp, w_a_bf, 256, 3072, proj_a_s, num_s, gs, stat_s,
                                                  state_mlstm_C[l], state_mlstm_n[l], ls)
    proj_b_p, xc_s = _inproj_with_attn(hp, w_b_bf, 1024, 1024, proj_b_s, cache_mem_k[l], cache_mem_v[l], ls)
    xb_p, c_p, n_p, mcol_p = _mlstm_prompt(proj_a_p, gp, gtp, nbp, lp)
    m_p = mcol_p.reshape(nbp, lp, LANES)[:, lp - 1, :HEADS]
    xc_p = _mem_attn(proj_b_p, mk3, mv3, nbp, lp, nseq=1, tm=512)

    yp, pool_p = _pool_and_merge(xp2d, nbp, lp, proj_a_p, proj_b_p, xb_p, xc_p,
                                 jnp.zeros((nbp, POOL_BUF, POOL_WIDTH), F32), wts, 0)
    ys, pool_s = _pool_and_merge(xs2d, nbs, ls, proj_a_s, proj_b_s, xb_s, xc_s, state_pool[l], wts, PAST_LEN)
    kv_shape = (1, nbp, MEM_TOKENS, HEADS, MEM_HEAD_DIM)
    return (yp, ys, pool_p[None], c_p[None], n_p[None], m_p[None],
            mk2d.reshape(kv_shape), mv2d.reshape(kv_shape),
            pool_s[None], c_s[None], n_s[None], m_s[None])
```

```python
import functools

import jax
import jax.numpy as jnp
from jax import lax
from jax.experimental import pallas as pl
from jax.experimental.pallas import tpu as pltpu

F32 = jnp.float32
BF16 = jnp.bfloat16

D_MODEL = 2048
PAST_LEN = 16384
POOL_WINDOWS = (2, 4, 8, 16)
POOL_GROUP_WIDTH = 256
POOL_WIDTH = 1024
POOL_BUF = 15
HEADS = 4
MLSTM_HEAD_DIM = 512
MLSTM_WIDTH = 2048
PROMPT_CHUNK = 512
MEM_TOKENS = 256
MEM_HEAD_DIM = 256
MEM_WIDTH = 1024
EPS = 1e-6
N_GATE_COLS = 2 * HEADS
GATE_COL0 = 2 * POOL_WIDTH + 5 * MLSTM_WIDTH
LANES = 128
SUBLANES = 8
COL_PV, COL_PZ = 0, 1024
COL_Q, COL_K, COL_V, COL_O, COL_Z = 2048, 4096, 6144, 8192, 10240
COL_CQ, COL_CZ = 0, 1024
COL_GA, COL_GB, COL_GC = 2048, 4096, 6144
N_PROJ_A = GATE_COL0
N_PROJ_B = 2 * MEM_WIDTH + 3 * D_MODEL
VMEM_LIMIT = 56 * 1024 * 1024


def _cparams(sem):
    return pltpu.CompilerParams(dimension_semantics=sem, vmem_limit_bytes=VMEM_LIMIT)


def _sigmoid(x):
    return 0.5 * jnp.tanh(0.5 * x) + 0.5


def _silu(x):
    return x * _sigmoid(x)


def _inproj_kernel(h_ref, w_ref, o_ref, wbf_ref):
    @pl.when(pl.program_id(1) == 0)
    def _():
        wbf_ref[...] = w_ref[...].astype(BF16)

    o_ref[...] = lax.dot_general(h_ref[...], wbf_ref[...], (((1,), (1,)), ((), ())), preferred_element_type=F32)


def _inproj(h, w_t, row0, n_cols, tm, tn):
    t, d = h.shape
    assert row0 % SUBLANES == 0 and n_cols % tn == 0 and t % tm == 0 and row0 + n_cols <= w_t.shape[0]
    return pl.pallas_call(
        _inproj_kernel,
        grid=(n_cols // tn, t // tm),
        in_specs=[pl.BlockSpec((tm, d), lambda j, i: (i, 0)),
                  pl.BlockSpec((pl.Element(tn), pl.Element(d)),
                               lambda j, i: (pl.multiple_of(row0 + j * tn, SUBLANES), 0))],
        out_specs=[pl.BlockSpec((tm, tn), lambda j, i: (i, j)),
                   pl.BlockSpec((tn, d), lambda j, i: (j, 0))],
        out_shape=[jax.ShapeDtypeStruct((t, n_cols), F32),
                   jax.ShapeDtypeStruct((n_cols, d), BF16)],
        compiler_params=_cparams(("arbitrary", "arbitrary")),
        name="inproj",
    )(h, w_t)


GATE_TILE = 512


def _gate_kernel(x_ref, gpre_ref, wg_ref, bias_ref, m0_ref, h_ref, g_ref, gt_ref, *, seg, ct):
    x = x_ref[...]
    ms = jnp.mean(x * x, axis=-1, keepdims=True)
    h_ref[...] = (x * lax.rsqrt(ms + EPS) * gpre_ref[...]).astype(h_ref.dtype)
    src = lax.broadcasted_iota(jnp.int32, (ct, ct), 0)
    dst = lax.broadcasted_iota(jnp.int32, (ct, ct), 1)
    upto = src <= dst
    if seg < ct:
        shift = seg.bit_length() - 1
        upto = upto & (lax.shift_right_logical(src, shift) == lax.shift_right_logical(dst, shift))
    upto = upto.astype(F32)
    lane = lax.broadcasted_iota(jnp.int32, (ct, LANES), 1)
    gate_row = lax.broadcasted_iota(jnp.int32, (2 * HEADS, ct), 0)
    wg = wg_ref[...].astype(BF16)
    for r in range(GATE_TILE // ct):
        rs = slice(r * ct, (r + 1) * ct)
        x = lax.dot_general(h_ref[rs, :], wg, (((1,), (1,)), ((), ())), preferred_element_type=F32) + bias_ref[...]
        xt = x.T[:2 * HEADS, :]
        lf = jnp.minimum(xt, 0.0) - jnp.log1p(jnp.exp(-jnp.abs(xt)))
        cums = jnp.dot(lf, upto, precision=lax.Precision.HIGHEST, preferred_element_type=F32)
        gt_ref[:, rs] = jnp.where(gate_row < HEADS, xt, cums)
        cums_col = jnp.concatenate([cums, jnp.zeros((LANES - 2 * HEADS, ct), F32)], axis=0).T
        g_ref[rs, :] = jnp.where(lane < HEADS, x, jnp.where(lane < 2 * HEADS, cums_col, m0_ref[rs, :]))


def _norm_gate_prep(x2d, gpre_row, w_t, bias_row, m0_rows, seg):
    t, d = x2d.shape
    tg = GATE_TILE
    ct = max(seg, LANES)
    assert GATE_COL0 % LANES == 0 and tg % ct == 0 and ct % seg == 0 and seg & (seg - 1) == 0
    return pl.pallas_call(
        functools.partial(_gate_kernel, seg=seg, ct=ct),
        grid=(t // tg,),
        in_specs=[pl.BlockSpec((tg, d), lambda i: (i, 0)),
                  pl.BlockSpec((1, d), lambda i: (0, 0)),
                  pl.BlockSpec((LANES, d), lambda i: (GATE_COL0 // LANES, 0)),
                  pl.BlockSpec((1, LANES), lambda i: (0, 0)),
                  pl.BlockSpec((tg, LANES), lambda i: (i, 0))],
        out_specs=[pl.BlockSpec((tg, d), lambda i: (i, 0)),
                   pl.BlockSpec((tg, LANES), lambda i: (i, 0)),
                   pl.BlockSpec((8, tg), lambda i: (0, i))],
        out_shape=[jax.ShapeDtypeStruct((t, d), BF16),
                   jax.ShapeDtypeStruct((t, LANES), F32),
                   jax.ShapeDtypeStruct((8, t), F32)],
        compiler_params=_cparams(("parallel",)),
        name="norm_gate_prep",
    )(x2d, gpre_row, w_t, bias_row, m0_rows)


def _pool_kernel(pv_ref, pz_ref, buf_ref, wg_ref, scale_ref, xa_ref, nbuf_ref, s_ref, p_ref,
                 *, nseq, tm, nt, start):
    j = pl.program_id(1)

    def one_seq(b, r0):
        if nt > 1:
            @pl.when(j == 0)
            def _():
                s_ref[1:16, :] = buf_ref[b]
        else:
            s_ref[1:16, :] = buf_ref[b]
        s_ref[16:16 + tm, :] = pv_ref[pl.ds(r0, tm), :]
        pos = start + j * tm + lax.broadcasted_iota(jnp.int32, (tm, 1), 0)
        for g, w in enumerate(POOL_WINDOWS):
            cs = slice(g * POOL_GROUP_WIDTH, (g + 1) * POOL_GROUP_WIDTH)
            acc = s_ref[16:16 + tm, cs]
            for i in range(1, w):
                acc = acc + s_ref[16 - i:16 - i + tm, cs]
            inv_cnt = 1.0 / jnp.minimum(w, pos + 1).astype(F32)
            p_ref[pl.ds(r0, tm), cs] = acc * inv_cnt - s_ref[16:16 + tm, cs]
        tail = s_ref[tm + 1:tm + 16, :]
        nbuf_ref[b] = tail
        if nt > 1:
            s_ref[1:16, :] = tail

    if nseq == 1:
        one_seq(0, 0)
    else:
        def body(b, carry):
            one_seq(b, pl.multiple_of(b * tm, 8))
            return carry
        lax.fori_loop(0, nseq, body, 0)

    for g in range(len(POOL_WINDOWS)):
        cs = slice(g * POOL_GROUP_WIDTH, (g + 1) * POOL_GROUP_WIDTH)
        pa = jnp.dot(p_ref[:, cs].astype(BF16), wg_ref[g], preferred_element_type=F32) * scale_ref[:, cs]
        xa_ref[:, cs] = (pa * _silu(pz_ref[:, cs])).astype(xa_ref.dtype)


def _pool_branch(proj, buf, w_grp, scale_row, nb, seq_len, nseq, tm, start):
    assert nseq == 1 or tm == seq_len
    nt = seq_len // tm
    rows = nseq * tm
    t = nb * seq_len
    pvb, pzb = COL_PV // POOL_WIDTH, COL_PZ // POOL_WIDTH
    return pl.pallas_call(
        functools.partial(_pool_kernel, nseq=nseq, tm=tm, nt=nt, start=start),
        grid=(nb // nseq, nt),
        in_specs=[pl.BlockSpec((rows, POOL_WIDTH), lambda i, j: (i * nt + j, pvb)),
                  pl.BlockSpec((rows, POOL_WIDTH), lambda i, j: (i * nt + j, pzb)),
                  pl.BlockSpec((nseq, POOL_BUF, POOL_WIDTH), lambda i, j: (i, 0, 0)),
                  pl.BlockSpec((4, POOL_GROUP_WIDTH, POOL_GROUP_WIDTH), lambda i, j: (0, 0, 0)),
                  pl.BlockSpec((1, POOL_WIDTH), lambda i, j: (0, 0))],
        out_specs=[pl.BlockSpec((rows, POOL_WIDTH), lambda i, j: (i * nt + j, 0)),
                   pl.BlockSpec((nseq, POOL_BUF, POOL_WIDTH), lambda i, j: (i, 0, 0))],
        out_shape=[jax.ShapeDtypeStruct((t, POOL_WIDTH), BF16),
                   jax.ShapeDtypeStruct((nb, POOL_BUF, POOL_WIDTH), F32)],
        scratch_shapes=[pltpu.VMEM((16 + tm, POOL_WIDTH), F32),
                        pltpu.VMEM((rows, POOL_WIDTH), F32)],
        compiler_params=_cparams(("arbitrary", "arbitrary")),
        name="pool_branch",
    )(proj, proj, buf, w_grp, scale_row)


def _memkv_kernel(m_ref, g_ref, w_ref, k_ref, v_ref):
    x = m_ref[...]
    ms = jnp.mean(x * x, axis=-1, keepdims=True)
    h = (x * lax.rsqrt(ms + EPS) * g_ref[...]).astype(BF16)
    kv = jnp.dot(h, w_ref[...], preferred_element_type=F32)
    k_ref[...] = kv[:, :MEM_WIDTH]
    v_ref[...] = kv[:, MEM_WIDTH:]


def _mem_kv(mem2d, g_row, w_bf, tm):
    t, d = mem2d.shape
    return pl.pallas_call(
        _memkv_kernel,
        grid=(t // tm,),
        in_specs=[pl.BlockSpec((tm, d), lambda i: (i, 0)),
                  pl.BlockSpec((1, d), lambda i: (0, 0)),
                  pl.BlockSpec((d, 2 * MEM_WIDTH), lambda i: (0, 0))],
        out_specs=[pl.BlockSpec((tm, MEM_WIDTH), lambda i: (i, 0)),
                   pl.BlockSpec((tm, MEM_WIDTH), lambda i: (i, 0))],
        out_shape=[jax.ShapeDtypeStruct((t, MEM_WIDTH), F32),
                   jax.ShapeDtypeStruct((t, MEM_WIDTH), F32)],
        compiler_params=_cparams(("parallel",)),
        name="mem_kv",
    )(mem2d, g_row, w_bf)


def _attn_kernel(cq_ref, cz_ref, k_ref, v_ref, xc_ref, *, nseq, tm):
    scale = MEM_HEAD_DIM ** -0.5
    for b in range(nseq):
        rs = slice(b * tm, (b + 1) * tm)
        for h in range(HEADS):
            cs = slice(h * MEM_HEAD_DIM, (h + 1) * MEM_HEAD_DIM)
            q = cq_ref[rs, cs].astype(BF16)
            k = k_ref[b, :, cs].astype(BF16)
            v = v_ref[b, :, cs].astype(BF16)
            s = lax.dot_general(q, k, (((1,), (1,)), ((), ())), preferred_element_type=F32) * scale
            p = jnp.exp(s - jnp.max(s, axis=-1, keepdims=True))
            a = p * (1.0 / jnp.sum(p, axis=-1, keepdims=True))
            o = jnp.dot(a.astype(BF16), v, preferred_element_type=F32)
            xc_ref[rs, cs] = (o * _silu(cz_ref[rs, cs])).astype(xc_ref.dtype)


def _mem_attn(proj, mem_k, mem_v, nb, seq_len, nseq, tm):
    assert nseq == 1 or tm == seq_len
    nt = seq_len // tm
    rows = nseq * tm
    t = nb * seq_len
    cqb, czb = COL_CQ // MEM_WIDTH, COL_CZ // MEM_WIDTH
    return pl.pallas_call(
        functools.partial(_attn_kernel, nseq=nseq, tm=tm),
        grid=(nb // nseq, nt),
        in_specs=[pl.BlockSpec((rows, MEM_WIDTH), lambda i, j: (i * nt + j, cqb)),
                  pl.BlockSpec((rows, MEM_WIDTH), lambda i, j: (i * nt + j, czb)),
                  pl.BlockSpec((nseq, MEM_TOKENS, MEM_WIDTH), lambda i, j: (i, 0, 0)),
                  pl.BlockSpec((nseq, MEM_TOKENS, MEM_WIDTH), lambda i, j: (i, 0, 0))],
        out_specs=pl.BlockSpec((rows, MEM_WIDTH), lambda i, j: (i * nt + j, 0)),
        out_shape=jax.ShapeDtypeStruct((t, MEM_WIDTH), BF16),
        compiler_params=_cparams(("parallel", "arbitrary")),
        name="mem_attn",
    )(proj, proj, mem_k, mem_v)


def _attn_short_weights(cq_ref, k_ref, b, tm):
    scale = MEM_HEAD_DIM ** -0.5
    nr, nc = tm * HEADS, MEM_TOKENS * HEADS
    row_head = lax.broadcasted_iota(jnp.int32, (nr, nc), 0) // tm
    col_head = lax.broadcasted_iota(jnp.int32, (nr, nc), 1) % HEADS
    rs = slice(b * tm, (b + 1) * tm)
    kf = k_ref[b].reshape(nc, MEM_HEAD_DIM).astype(BF16)
    q = jnp.concatenate([cq_ref[rs, h * MEM_HEAD_DIM:(h + 1) * MEM_HEAD_DIM] for h in range(HEADS)], axis=0)
    s = lax.dot_general(q.astype(BF16), kf, (((1,), (1,)), ((), ())), preferred_element_type=F32) * scale
    s = jnp.where(row_head == col_head, s, -jnp.inf)
    p = jnp.exp(s - jnp.max(s, axis=-1, keepdims=True))
    return (p * (1.0 / jnp.sum(p, axis=-1, keepdims=True))).astype(BF16)


def _attn_short_output(a, cz_ref, v_ref, xc_ref, b, tm):
    rs = slice(b * tm, (b + 1) * tm)
    vf = v_ref[b].reshape(MEM_TOKENS * HEADS, MEM_HEAD_DIM).astype(BF16)
    o = jnp.dot(a, vf, preferred_element_type=F32)
    for h in range(HEADS):
        cs = slice(h * MEM_HEAD_DIM, (h + 1) * MEM_HEAD_DIM)
        xc_ref[rs, cs] = o[h * tm:(h + 1) * tm, :] * _silu(cz_ref[rs, cs])


def _inproj_attn_kernel(h_ref, w_ref, cq_ref, cz_ref, k_ref, v_ref, o_ref, xc_ref, *, nseq, seq_len):
    half = w_ref.shape[0] // 2
    nt = (((1,), (1,)), ((), ()))
    a = [_attn_short_weights(cq_ref, k_ref, b, seq_len) for b in range(nseq)]
    o_ref[:, :half] = lax.dot_general(h_ref[...], w_ref[:half, :], nt, preferred_element_type=F32)
    for b in range(nseq):
        _attn_short_output(a[b], cz_ref, v_ref, xc_ref, b, seq_len)
    o_ref[:, half:] = lax.dot_general(h_ref[...], w_ref[half:, :], nt, preferred_element_type=F32)


def _inproj_with_attn(h, w_bf, tm, tn, proj_s, mem_k, mem_v, seq_len):
    t, d = h.shape
    n_cols = w_bf.shape[0]
    nb = mem_k.shape[0]
    n_inner = t // tm
    n_steps = (n_cols // tn) * n_inner
    nseq = nb // n_steps
    rows = nseq * seq_len
    assert n_cols % tn == 0 and t % tm == 0 and nseq * n_steps == nb and rows % SUBLANES == 0
    cqb, czb = COL_CQ // MEM_WIDTH, COL_CZ // MEM_WIDTH
    kv_spec = pl.BlockSpec((nseq, MEM_TOKENS, HEADS, MEM_HEAD_DIM), lambda j, i: (j * n_inner + i, 0, 0, 0))
    return pl.pallas_call(
        functools.partial(_inproj_attn_kernel, nseq=nseq, seq_len=seq_len),
        grid=(n_cols // tn, n_inner),
        in_specs=[pl.BlockSpec((tm, d), lambda j, i: (i, 0)),
                  pl.BlockSpec((tn, d), lambda j, i: (j, 0)),
                  pl.BlockSpec((rows, MEM_WIDTH), lambda j, i: (j * n_inner + i, cqb)),
                  pl.BlockSpec((rows, MEM_WIDTH), lambda j, i: (j * n_inner + i, czb)),
                  kv_spec, kv_spec],
        out_specs=[pl.BlockSpec((tm, tn), lambda j, i: (i, j)),
                   pl.BlockSpec((rows, MEM_WIDTH), lambda j, i: (j * n_inner + i, 0))],
        out_shape=[jax.ShapeDtypeStruct((t, n_cols), F32),
                   jax.ShapeDtypeStruct((nb * seq_len, MEM_WIDTH), F32)],
        compiler_params=_cparams(("arbitrary", "arbitrary")),
        name="inproj_attn",
    )(h, w_bf, proj_s, proj_s, mem_k, mem_v)


def _mlstm_intra(q_bf, ks_bf, v_bf, ig_row, b_row, b_col, m0_col, mask):
    dlog = jnp.where(mask, b_col - b_row + ig_row, -jnp.inf)
    inter = b_col + m0_col
    m_col = jnp.maximum(inter, jnp.max(dlog, axis=1, keepdims=True))
    dw = jnp.exp(dlog - m_col)
    inter_w = jnp.exp(inter - m_col)
    s = lax.dot_general(q_bf, ks_bf, (((1,), (1,)), ((), ())), preferred_element_type=F32) * dw
    num = jnp.dot(s.astype(BF16), v_bf, preferred_element_type=F32)
    den = jnp.sum(s, axis=1, keepdims=True)
    return num, den, m_col, inter_w


def _mlstm_prompt_kernel(q_ref, k_ref, v_ref, g_ref, gt_ref, hb_ref, c_ref, n_ref, mcol_ref, m_scr):
    @pl.when(pl.program_id(1) == 0)
    def _():
        c_ref[...] = jnp.zeros_like(c_ref)
        n_ref[...] = jnp.zeros_like(n_ref)
        m_scr[...] = jnp.zeros_like(m_scr)

    cl = PROMPT_CHUNK
    row = lax.broadcasted_iota(jnp.int32, (cl, cl), 0)
    col = lax.broadcasted_iota(jnp.int32, (cl, cl), 1)
    mask = col <= row
    kscale = MLSTM_HEAD_DIM ** -0.5
    mcol_ref[...] = jnp.zeros_like(mcol_ref)
    nt = (((1,), (1,)), ((), ()))
    gates = []
    for h in range(HEADS):
        ig_col = g_ref[:, h:h + 1]
        b_col = g_ref[:, HEADS + h:HEADS + h + 1]
        ig_row = gt_ref[h:h + 1, :]
        b_row = gt_ref[HEADS + h:HEADS + h + 1, :]
        m0 = m_scr[h:h + 1, 0:1]
        dlog = jnp.where(mask, b_col - b_row + ig_row, -jnp.inf)
        inter = b_col + m0
        m_col = jnp.maximum(inter, jnp.max(dlog, axis=1, keepdims=True))
        dw = jnp.exp(dlog - m_col)
        inter_w = jnp.exp(inter - m_col)
        m_new = m_col[cl - 1:cl, :]
        b_last = b_col[cl - 1:cl, :]
        w_col = jnp.exp(b_last - b_col + ig_col - m_new)
        decay = jnp.exp(b_last + m0 - m_new)
        m_scr[h:h + 1, :] = jnp.broadcast_to(m_new, (1, LANES))
        mcol_ref[:, h:h + 1] = m_col
        gates.append((m_col, dw, inter_w, w_col, decay))
    prods = []
    for h in range(HEADS):
        cs = slice(h * MLSTM_HEAD_DIM, (h + 1) * MLSTM_HEAD_DIM)
        w_col = gates[h][3]
        q_bf = q_ref[:, cs].astype(BF16)
        ks_bf = (k_ref[:, cs] * kscale).astype(BF16)
        qk = lax.dot_general(q_bf, ks_bf, nt, preferred_element_type=F32)
        inter_num = lax.dot_general(q_bf, c_ref[0, h].astype(BF16), nt, preferred_element_type=F32)
        dc = lax.dot_general((w_col * v_ref[:, cs]).astype(BF16), ks_bf, (((0,), (0,)), ((), ())),
                             preferred_element_type=F32)
        prods.append((qk, inter_num, dc))
    for h in range(HEADS):
        cs = slice(h * MLSTM_HEAD_DIM, (h + 1) * MLSTM_HEAD_DIM)
        m_col, dw, inter_w, w_col, decay = gates[h]
        qk, inter_num, dc = prods[h]
        n0 = n_ref[0, h:h + 1, :]
        s = qk * dw
        num = jnp.dot(s.astype(BF16), v_ref[:, cs].astype(BF16), preferred_element_type=F32) + inter_w * inter_num
        den = jnp.sum(s, axis=1, keepdims=True) + inter_w * jnp.sum(q_ref[:, cs] * n0, axis=1, keepdims=True)
        hb_ref[:, cs] = num * (1.0 / jnp.maximum(jnp.abs(den), jnp.exp(-m_col)))
        c_ref[0, h] = decay * c_ref[0, h] + dc
        n_ref[0, h:h + 1, :] = decay * n0 + jnp.sum(w_col * (k_ref[:, cs] * kscale), axis=0, keepdims=True)


def _mlstm_prompt(proj, g, gt, nb, seq_len):
    cl = PROMPT_CHUNK
    nc = seq_len // cl
    t = nb * seq_len
    w = MLSTM_WIDTH

    def colspec(col0):
        return pl.BlockSpec((cl, w), lambda b, c: (b * nc + c, col0 // w))

    return pl.pallas_call(
        _mlstm_prompt_kernel,
        grid=(nb, nc),
        in_specs=[colspec(COL_Q), colspec(COL_K), colspec(COL_V),
                  pl.BlockSpec((cl, LANES), lambda b, c: (b * nc + c, 0)),
                  pl.BlockSpec((8, cl), lambda b, c: (0, b * nc + c))],
        out_specs=[pl.BlockSpec((cl, w), lambda b, c: (b * nc + c, 0)),
                   pl.BlockSpec((1, HEADS, MLSTM_HEAD_DIM, MLSTM_HEAD_DIM), lambda b, c: (b, 0, 0, 0)),
                   pl.BlockSpec((1, HEADS, MLSTM_HEAD_DIM), lambda b, c: (b, 0, 0)),
                   pl.BlockSpec((cl, LANES), lambda b, c: (b * nc + c, 0))],
        out_shape=[jax.ShapeDtypeStruct((t, w), F32),
                   jax.ShapeDtypeStruct((nb, HEADS, MLSTM_HEAD_DIM, MLSTM_HEAD_DIM), F32),
                   jax.ShapeDtypeStruct((nb, HEADS, MLSTM_HEAD_DIM), F32),
                   jax.ShapeDtypeStruct((t, LANES), F32)],
        scratch_shapes=[pltpu.VMEM((8, LANES), F32)],
        compiler_params=_cparams(("arbitrary", "arbitrary")),
        name="mlstm_prompt",
    )(proj, proj, proj, g, gt)


STAT_DEN, STAT_IW, STAT_M = 0, HEADS, 2 * HEADS


def _mlstm_short_intra_kernel(q_ref, k_ref, v_ref, g_ref, gt_ref, num_ref, stat_ref, *, seq_len):
    tl = LANES
    kscale = MLSTM_HEAD_DIM ** -0.5
    shift = seq_len.bit_length() - 1
    row = lax.broadcasted_iota(jnp.int32, (tl, tl), 0)
    col = lax.broadcasted_iota(jnp.int32, (tl, tl), 1)
    mask = (col <= row) & (lax.shift_right_logical(col, shift) == lax.shift_right_logical(row, shift))
    stat_ref[...] = jnp.zeros_like(stat_ref)
    for h in range(HEADS):
        cs = slice(h * MLSTM_HEAD_DIM, (h + 1) * MLSTM_HEAD_DIM)
        b_col = g_ref[:, HEADS + h:HEADS + h + 1]
        m0_col = g_ref[:, 2 * HEADS + h:2 * HEADS + h + 1]
        ig_row = gt_ref[h:h + 1, :]
        b_row = gt_ref[HEADS + h:HEADS + h + 1, :]
        q_bf = q_ref[:, cs].astype(BF16)
        ks_bf = (k_ref[:, cs] * kscale).astype(BF16)
        num, den, m_col, inter_w = _mlstm_intra(q_bf, ks_bf, v_ref[:, cs].astype(BF16),
                                                ig_row, b_row, b_col, m0_col, mask)
        num_ref[:, cs] = num
        stat_ref[:, STAT_DEN + h:STAT_DEN + h + 1] = den
        stat_ref[:, STAT_IW + h:STAT_IW + h + 1] = inter_w
        stat_ref[:, STAT_M + h:STAT_M + h + 1] = m_col


def _mlstm_short_intra(proj, g, gt, seq_len):
    t = proj.shape[0]
    tl = LANES
    w = MLSTM_WIDTH
    assert tl % seq_len == 0 and t % tl == 0

    def colspec(col0):
        return pl.BlockSpec((tl, w), lambda i: (i, col0 // w))

    return pl.pallas_call(
        functools.partial(_mlstm_short_intra_kernel, seq_len=seq_len),
        grid=(t // tl,),
        in_specs=[colspec(COL_Q), colspec(COL_K), colspec(COL_V),
                  pl.BlockSpec((tl, LANES), lambda i: (i, 0)),
                  pl.BlockSpec((8, tl), lambda i: (0, i))],
        out_specs=[pl.BlockSpec((tl, w), lambda i: (i, 0)),
                   pl.BlockSpec((tl, LANES), lambda i: (i, 0))],
        out_shape=[jax.ShapeDtypeStruct((t, w), F32),
                   jax.ShapeDtypeStruct((t, LANES), F32)],
        compiler_params=_cparams(("parallel",)),
        name="mlstm_short_intra",
    )(proj, proj, proj, g, gt)


def _mlstm_state_head(h, q_ref, k_ref, v_ref, num_ref, g_ref, stat_ref, c0_ref, n0_ref, hb_ref, c_ref, n_ref):
    kscale = MLSTM_HEAD_DIM ** -0.5
    seq_len = q_ref.shape[0]
    last = slice(seq_len - 1, seq_len)
    cs = slice(h * MLSTM_HEAD_DIM, (h + 1) * MLSTM_HEAD_DIM)
    qf = q_ref[:, cs]
    kf = k_ref[:, cs] * kscale
    vf = v_ref[:, cs]
    ig_col = g_ref[:, h:h + 1]
    b_col = g_ref[:, HEADS + h:HEADS + h + 1]
    b_last = g_ref[last, HEADS + h:HEADS + h + 1]
    m0 = g_ref[last, 2 * HEADS + h:2 * HEADS + h + 1]
    den_intra = stat_ref[:, STAT_DEN + h:STAT_DEN + h + 1]
    inter_w = stat_ref[:, STAT_IW + h:STAT_IW + h + 1]
    m_col = stat_ref[:, STAT_M + h:STAT_M + h + 1]
    m_new = stat_ref[last, STAT_M + h:STAT_M + h + 1]
    c0 = c0_ref[0, 0]
    n0 = n0_ref[0, h:h + 1, :]
    num = num_ref[:, cs] + inter_w * lax.dot_general(
        qf.astype(BF16), c0.astype(BF16), (((1,), (1,)), ((), ())), preferred_element_type=F32)
    den = den_intra + inter_w * jnp.sum(qf * n0, axis=1, keepdims=True)
    hval = num * (1.0 / jnp.maximum(jnp.abs(den), jnp.exp(-m_col)))
    hb_ref[:, cs] = hval
    w_col = jnp.exp(b_last - b_col + ig_col - m_new)
    decay = jnp.exp(b_last + m0 - m_new)
    dc = lax.dot_general((w_col * vf).astype(BF16), kf.astype(BF16), (((0,), (0,)), ((), ())),
                         preferred_element_type=F32)
    c_ref[0, h] = decay * c0 + dc
    n_ref[0, h:h + 1, :] = decay * n0 + jnp.sum(w_col * kf, axis=0, keepdims=True)


def _inproj_state_kernel(h_ref, w_ref, *refs):
    tok_refs, c0_refs, n0_ref = refs[:6], refs[6:6 + HEADS], refs[6 + HEADS]
    o_ref, out_state_refs = refs[7 + HEADS], refs[8 + HEADS:]
    half = w_ref.shape[0] // 2
    nt = (((1,), (1,)), ((), ()))
    o_ref[:, :half] = lax.dot_general(h_ref[...], w_ref[:half, :], nt, preferred_element_type=F32)
    for h in range(HEADS):
        _mlstm_state_head(h, *tok_refs, c0_refs[h], n0_ref, *out_state_refs)
    o_ref[:, half:] = lax.dot_general(h_ref[...], w_ref[half:, :], nt, preferred_element_type=F32)


def _inproj_with_state(h, w_bf, tm, tn, proj_s, num_s, g_s, stat_s, c0, n0, seq_len):
    t, d = h.shape
    n_cols = w_bf.shape[0]
    n_seq = c0.shape[0]
    n_inner = t // tm
    assert n_cols % tn == 0 and t % tm == 0 and (n_cols // tn) * n_inner == n_seq
    w = MLSTM_WIDTH
    hd = MLSTM_HEAD_DIM

    def seq(j, i):
        return j * n_inner + i

    def colspec(col0):
        return pl.BlockSpec((seq_len, w), lambda j, i: (seq(j, i), col0 // w))

    rowspec = pl.BlockSpec((seq_len, w), lambda j, i: (seq(j, i), 0))
    lanespec = pl.BlockSpec((seq_len, LANES), lambda j, i: (seq(j, i), 0))
    state4 = pl.BlockSpec((1, HEADS, hd, hd), lambda j, i: (seq(j, i), 0, 0, 0))
    state3 = pl.BlockSpec((1, HEADS, hd), lambda j, i: (seq(j, i), 0, 0))
    return pl.pallas_call(
        _inproj_state_kernel,
        grid=(n_cols // tn, n_inner),
        in_specs=[pl.BlockSpec((tm, d), lambda j, i: (i, 0)),
                  pl.BlockSpec((tn, d), lambda j, i: (j, 0)),
                  colspec(COL_Q), colspec(COL_K), colspec(COL_V),
                  rowspec, lanespec, lanespec,
                  *[pl.BlockSpec((1, 1, hd, hd), functools.partial(lambda j, i, hh: (seq(j, i), hh, 0, 0), hh=hh))
                    for hh in range(HEADS)],
                  state3],
        out_specs=[pl.BlockSpec((tm, tn), lambda j, i: (i, j)), rowspec, state4, state3],
        out_shape=[jax.ShapeDtypeStruct((t, n_cols), F32),
                   jax.ShapeDtypeStruct((n_seq * seq_len, w), F32),
                   jax.ShapeDtypeStruct((n_seq, HEADS, hd, hd), F32),
                   jax.ShapeDtypeStruct((n_seq, HEADS, hd), F32)],
        compiler_params=_cparams(("arbitrary", "arbitrary")),
        name="inproj_state",
    )(h, w_bf, proj_s, proj_s, proj_s, num_s, g_s, stat_s, *([c0] * HEADS), n0)


def _resident(shape):
    return pl.BlockSpec(shape, lambda i: (0, 0), pipeline_mode=pl.Buffered(1))


def _merge_kernel(xa_ref, hb_ref, o_ref, z_ref, xc_ref, ga_ref, gb_ref, gc_ref, wa_ref, wb_ref, wc_ref, m_ref):
    ya = jnp.dot(xa_ref[...].astype(BF16), wa_ref[...], preferred_element_type=F32)
    merged = _sigmoid(ga_ref[...]) * ya
    xb = _sigmoid(o_ref[...]) * hb_ref[...] * _silu(z_ref[...])
    yb = jnp.dot(xb.astype(BF16), wb_ref[...], preferred_element_type=F32)
    merged = merged + _sigmoid(gb_ref[...]) * yb
    yc = jnp.dot(xc_ref[...].astype(BF16), wc_ref[...], preferred_element_type=F32)
    merged = merged + _sigmoid(gc_ref[...]) * yc
    m_ref[...] = merged.astype(m_ref.dtype)


def _out_kernel(m_ref, x_ref, gpost_ref, wo_ref, y_ref):
    out = jnp.dot(m_ref[...], wo_ref[...], preferred_element_type=F32)
    ms = jnp.mean(out * out, axis=-1, keepdims=True)
    y_ref[...] = x_ref[...] + out * lax.rsqrt(ms + EPS) * gpost_ref[...]


def _final(xa, hb, xc, proj_a, proj_b, x2d, gpost_row, wa, wb, wc, wo, tm_merge, tm_out):
    t, d = x2d.shape

    def rowspec(tm, width, colblk=0):
        return pl.BlockSpec((tm, width), lambda i: (i, colblk))

    tm = tm_merge
    merged = pl.pallas_call(
        _merge_kernel,
        grid=(t // tm,),
        in_specs=[rowspec(tm, POOL_WIDTH), rowspec(tm, MLSTM_WIDTH),
                  rowspec(tm, MLSTM_WIDTH, COL_O // MLSTM_WIDTH), rowspec(tm, MLSTM_WIDTH, COL_Z // MLSTM_WIDTH),
                  rowspec(tm, MEM_WIDTH),
                  rowspec(tm, d, COL_GA // d), rowspec(tm, d, COL_GB // d), rowspec(tm, d, COL_GC // d),
                  _resident(wa.shape), _resident(wb.shape), _resident(wc.shape)],
        out_specs=rowspec(tm, d),
        out_shape=jax.ShapeDtypeStruct((t, d), BF16),
        compiler_params=_cparams(("parallel",)),
        name="merge",
    )(xa, hb, proj_a, proj_a, xc, proj_b, proj_b, proj_b, wa, wb, wc)
    tm = tm_out
    return pl.pallas_call(
        _out_kernel,
        grid=(t // tm,),
        in_specs=[rowspec(tm, d), rowspec(tm, d), _resident((1, d)), _resident(wo.shape)],
        out_specs=rowspec(tm, d),
        out_shape=jax.ShapeDtypeStruct((t, d), F32),
        compiler_params=_cparams(("parallel",)),
        name="out_proj",
    )(merged, x2d, gpost_row, wo)


B_ROW0 = GATE_COL0 + N_GATE_COLS


def _pool_and_merge(x2d, nb, seq_len, proj_a, proj_b, xb, xc, pool_buf, wts, start):
    if seq_len >= 512:
        xa, new_buf = _pool_branch(proj_a, pool_buf, wts["w_pool_grp"], wts["pool_scale"], nb, seq_len,
                                   nseq=1, tm=512, start=start)
    else:
        xa, new_buf = _pool_branch(proj_a, pool_buf, wts["w_pool_grp"], wts["pool_scale"], nb, seq_len,
                                   nseq=32, tm=seq_len, start=start)
    y = _final(xa, xb, xc, proj_a, proj_b, x2d, wts["g_post"], wts["w_br_pool"], wts["w_br_mlstm"],
               wts["w_br_mem"], wts["w_out"], tm_merge=256, tm_out=512)
    return y.reshape(nb, seq_len, x2d.shape[1]), new_buf


def kernel(x_prompt, x_sample, state_pool, state_mlstm_C, state_mlstm_n, state_mlstm_m, cache_mem_k,
           cache_mem_v, mem_prompt, g_pre, g_post, w_in, b_mlstm_i, b_mlstm_f, w_pool_grp, pool_scale,
           g_mem, w_mem_kv, w_br_pool, w_br_mlstm, w_br_mem, w_out):
    nbp = x_prompt.shape[0]
    assert w_in.shape[0] == 1, "single-layer problem: the kernels index layer 0 of the stacked weights"
    l = 0
    wts = {
        "g_pre": g_pre[l][None, :],
        "g_post": g_post[l][None, :],
        "w_in": jnp.transpose(w_in[l]),
        "gate_bias": jnp.pad(jnp.concatenate([b_mlstm_i[l], b_mlstm_f[l]]), (0, LANES - N_GATE_COLS))[None, :],
        "w_pool_grp": w_pool_grp[l].astype(BF16),
        "pool_scale": pool_scale[l][None, :],
        "w_br_pool": w_br_pool[l].astype(BF16),
        "w_br_mlstm": w_br_mlstm[l].astype(BF16),
        "w_br_mem": w_br_mem[l].astype(BF16),
        "w_out": w_out[l].astype(BF16),
    }
    mk2d, mv2d = _mem_kv(mem_prompt.reshape(nbp * MEM_TOKENS, D_MODEL), g_mem[l][None, :],
                         w_mem_kv[l].astype(BF16), tm=256)
    mk3, mv3 = mk2d.reshape(nbp, MEM_TOKENS, MEM_WIDTH), mv2d.reshape(nbp, MEM_TOKENS, MEM_WIDTH)
    w_t = wts["w_in"]

    nbs, ls, d = x_sample.shape
    ts = nbs * ls
    xs2d = x_sample.reshape(ts, d)
    m0_rows = jnp.pad(jnp.repeat(state_mlstm_m[l], ls, axis=0), ((0, 0), (2 * HEADS, LANES - 3 * HEADS)))
    hs, gs, gts = _norm_gate_prep(xs2d, wts["g_pre"], w_t, wts["gate_bias"], m0_rows, ls)
    proj_a_s, w_a_bf = _inproj(hs, w_t, 0, N_PROJ_A, tm=ts, tn=1024)
    proj_b_s, w_b_bf = _inproj(hs, w_t, B_ROW0, N_PROJ_B, tm=ts, tn=1024)
    num_s, stat_s = _mlstm_short_intra(proj_a_s, gs, gts, ls)
    m_s = stat_s.reshape(nbs, ls, LANES)[:, ls - 1, STAT_M:STAT_M + HEADS]

    _, lp, _ = x_prompt.shape
    tp = nbp * lp
    xp2d = x_prompt.reshape(tp, d)
    hp, gp, gtp = _norm_gate_prep(xp2d, wts["g_pre"], w_t, wts["gate_bias"], jnp.zeros((tp, LANES), F32),
                                  PROMPT_CHUNK)
    proj_a_p, xb_s, c_s, n_s = _inproj_with_state(hp, w_a_bf, 256, 3072, proj_a_s, num_s, gs, stat_s,
                                                  state_mlstm_C[l], state_mlstm_n[l], ls)
    proj_b_p, xc_s = _inproj_with_attn(hp, w_b_bf, 512, 2048, proj_b_s, cache_mem_k[l], cache_mem_v[l], ls)
    xb_p, c_p, n_p, mcol_p = _mlstm_prompt(proj_a_p, gp, gtp, nbp, lp)
    m_p = mcol_p.reshape(nbp, lp, LANES)[:, lp - 1, :HEADS]
    xc_p = _mem_attn(proj_b_p, mk3, mv3, nbp, lp, nseq=1, tm=512)

    yp, pool_p = _pool_and_merge(xp2d, nbp, lp, proj_a_p, proj_b_p, xb_p, xc_p,
                                 jnp.zeros((nbp, POOL_BUF, POOL_WIDTH), F32), wts, 0)
    ys, pool_s = _pool_and_merge(xs2d, nbs, ls, proj_a_s, proj_b_s, xb_s, xc_s, state_pool[l], wts, PAST_LEN)
    kv_shape = (1, nbp, MEM_TOKENS, HEADS, MEM_HEAD_DIM)
    return (yp, ys, pool_p[None], c_p[None], n_p[None], m_p[None],
            mk2d.reshape(kv_shape), mv2d.reshape(kv_shape),
            pool_s[None], c_s[None], n_s[None], m_s[None])
```

```python
import functools

import jax
import jax.numpy as jnp
from jax import lax
from jax.experimental import pallas as pl
from jax.experimental.pallas import tpu as pltpu

F32 = jnp.float32
BF16 = jnp.bfloat16

D_MODEL = 2048
PAST_LEN = 16384
POOL_WINDOWS = (2, 4, 8, 16)
POOL_GROUP_WIDTH = 256
POOL_WIDTH = 1024
POOL_BUF = 15
HEADS = 4
MLSTM_HEAD_DIM = 512
MLSTM_WIDTH = 2048
PROMPT_CHUNK = 512
MEM_TOKENS = 256
MEM_HEAD_DIM = 256
MEM_WIDTH = 1024
EPS = 1e-6
N_GATE_COLS = 2 * HEADS
GATE_COL0 = 2 * POOL_WIDTH + 5 * MLSTM_WIDTH
LANES = 128
SUBLANES = 8
COL_PV, COL_PZ = 0, 1024
COL_Q, COL_K, COL_V, COL_O, COL_Z = 2048, 4096, 6144, 8192, 10240
COL_CQ, COL_CZ = 0, 1024
COL_GA, COL_GB, COL_GC = 2048, 4096, 6144
N_PROJ_A = GATE_COL0
N_PROJ_B = 2 * MEM_WIDTH + 3 * D_MODEL
VMEM_LIMIT = 56 * 1024 * 1024


def _cparams(sem):
    return pltpu.CompilerParams(dimension_semantics=sem, vmem_limit_bytes=VMEM_LIMIT)


def _sigmoid(x):
    return 0.5 * jnp.tanh(0.5 * x) + 0.5


def _silu(x):
    return x * _sigmoid(x)


def _inproj_kernel(h_ref, w_ref, o_ref, wbf_ref):
    @pl.when(pl.program_id(1) == 0)
    def _():
        wbf_ref[...] = w_ref[...].astype(BF16)

    o_ref[...] = lax.dot_general(h_ref[...], wbf_ref[...], (((1,), (1,)), ((), ())), preferred_element_type=F32)


def _inproj(h, w_t, row0, n_cols, tm, tn):
    t, d = h.shape
    assert row0 % SUBLANES == 0 and n_cols % tn == 0 and t % tm == 0 and row0 + n_cols <= w_t.shape[0]
    return pl.pallas_call(
        _inproj_kernel,
        grid=(n_cols // tn, t // tm),
        in_specs=[pl.BlockSpec((tm, d), lambda j, i: (i, 0)),
                  pl.BlockSpec((pl.Element(tn), pl.Element(d)),
                               lambda j, i: (pl.multiple_of(row0 + j * tn, SUBLANES), 0))],
        out_specs=[pl.BlockSpec((tm, tn), lambda j, i: (i, j)),
                   pl.BlockSpec((tn, d), lambda j, i: (j, 0))],
        out_shape=[jax.ShapeDtypeStruct((t, n_cols), F32),
                   jax.ShapeDtypeStruct((n_cols, d), BF16)],
        compiler_params=_cparams(("arbitrary", "arbitrary")),
        name="inproj",
    )(h, w_t)


GATE_TILE = 512


def _gate_kernel(x_ref, gpre_ref, wg_ref, bias_ref, m0_ref, h_ref, g_ref, gt_ref, *, seg, ct):
    x = x_ref[...]
    ms = jnp.mean(x * x, axis=-1, keepdims=True)
    h_ref[...] = (x * lax.rsqrt(ms + EPS) * gpre_ref[...]).astype(h_ref.dtype)
    src = lax.broadcasted_iota(jnp.int32, (ct, ct), 0)
    dst = lax.broadcasted_iota(jnp.int32, (ct, ct), 1)
    upto = src <= dst
    if seg < ct:
        shift = seg.bit_length() - 1
        upto = upto & (lax.shift_right_logical(src, shift) == lax.shift_right_logical(dst, shift))
    upto = upto.astype(F32)
    lane = lax.broadcasted_iota(jnp.int32, (ct, LANES), 1)
    gate_row = lax.broadcasted_iota(jnp.int32, (2 * HEADS, ct), 0)
    wg = wg_ref[...].astype(BF16)
    for r in range(GATE_TILE // ct):
        rs = slice(r * ct, (r + 1) * ct)
        x = lax.dot_general(h_ref[rs, :], wg, (((1,), (1,)), ((), ())), preferred_element_type=F32) + bias_ref[...]
        xt = x.T[:2 * HEADS, :]
        lf = jnp.minimum(xt, 0.0) - jnp.log1p(jnp.exp(-jnp.abs(xt)))
        cums = jnp.dot(lf, upto, precision=lax.Precision.HIGHEST, preferred_element_type=F32)
        gt_ref[:, rs] = jnp.where(gate_row < HEADS, xt, cums)
        cums_col = jnp.concatenate([cums, jnp.zeros((LANES - 2 * HEADS, ct), F32)], axis=0).T
        g_ref[rs, :] = jnp.where(lane < HEADS, x, jnp.where(lane < 2 * HEADS, cums_col, m0_ref[rs, :]))


def _norm_gate_prep(x2d, gpre_row, w_t, bias_row, m0_rows, seg):
    t, d = x2d.shape
    tg = GATE_TILE
    ct = max(seg, LANES)
    assert GATE_COL0 % LANES == 0 and tg % ct == 0 and ct % seg == 0 and seg & (seg - 1) == 0
    return pl.pallas_call(
        functools.partial(_gate_kernel, seg=seg, ct=ct),
        grid=(t // tg,),
        in_specs=[pl.BlockSpec((tg, d), lambda i: (i, 0)),
                  pl.BlockSpec((1, d), lambda i: (0, 0)),
                  pl.BlockSpec((LANES, d), lambda i: (GATE_COL0 // LANES, 0)),
                  pl.BlockSpec((1, LANES), lambda i: (0, 0)),
                  pl.BlockSpec((tg, LANES), lambda i: (i, 0))],
        out_specs=[pl.BlockSpec((tg, d), lambda i: (i, 0)),
                   pl.BlockSpec((tg, LANES), lambda i: (i, 0)),
                   pl.BlockSpec((8, tg), lambda i: (0, i))],
        out_shape=[jax.ShapeDtypeStruct((t, d), BF16),
                   jax.ShapeDtypeStruct((t, LANES), F32),
                   jax.ShapeDtypeStruct((8, t), F32)],
        compiler_params=_cparams(("parallel",)),
        name="norm_gate_prep",
    )(x2d, gpre_row, w_t, bias_row, m0_rows)


def _pool_kernel(pv_ref, pz_ref, buf_ref, wg_ref, scale_ref, xa_ref, nbuf_ref, s_ref, p_ref,
                 *, nseq, tm, nt, start):
    j = pl.program_id(1)

    def one_seq(b, r0):
        if nt > 1:
            @pl.when(j == 0)
            def _():
                s_ref[1:16, :] = buf_ref[b]
        else:
            s_ref[1:16, :] = buf_ref[b]
        s_ref[16:16 + tm, :] = pv_ref[pl.ds(r0, tm), :]
        pos = start + j * tm + lax.broadcasted_iota(jnp.int32, (tm, 1), 0)
        for g, w in enumerate(POOL_WINDOWS):
            cs = slice(g * POOL_GROUP_WIDTH, (g + 1) * POOL_GROUP_WIDTH)
            acc = s_ref[16:16 + tm, cs]
            for i in range(1, w):
                acc = acc + s_ref[16 - i:16 - i + tm, cs]
            inv_cnt = 1.0 / jnp.minimum(w, pos + 1).astype(F32)
            p_ref[pl.ds(r0, tm), cs] = acc * inv_cnt - s_ref[16:16 + tm, cs]
        tail = s_ref[tm + 1:tm + 16, :]
        nbuf_ref[b] = tail
        if nt > 1:
            s_ref[1:16, :] = tail

    if nseq == 1:
        one_seq(0, 0)
    else:
        def body(b, carry):
            one_seq(b, pl.multiple_of(b * tm, 8))
            return carry
        lax.fori_loop(0, nseq, body, 0)

    for g in range(len(POOL_WINDOWS)):
        cs = slice(g * POOL_GROUP_WIDTH, (g + 1) * POOL_GROUP_WIDTH)
        pa = jnp.dot(p_ref[:, cs].astype(BF16), wg_ref[g], preferred_element_type=F32) * scale_ref[:, cs]
        xa_ref[:, cs] = (pa * _silu(pz_ref[:, cs])).astype(xa_ref.dtype)


def _pool_branch(proj, buf, w_grp, scale_row, nb, seq_len, nseq, tm, start):
    assert nseq == 1 or tm == seq_len
    nt = seq_len // tm
    rows = nseq * tm
    t = nb * seq_len
    pvb, pzb = COL_PV // POOL_WIDTH, COL_PZ // POOL_WIDTH
    return pl.pallas_call(
        functools.partial(_pool_kernel, nseq=nseq, tm=tm, nt=nt, start=start),
        grid=(nb // nseq, nt),
        in_specs=[pl.BlockSpec((rows, POOL_WIDTH), lambda i, j: (i * nt + j, pvb)),
                  pl.BlockSpec((rows, POOL_WIDTH), lambda i, j: (i * nt + j, pzb)),
                  pl.BlockSpec((nseq, POOL_BUF, POOL_WIDTH), lambda i, j: (i, 0, 0)),
                  pl.BlockSpec((4, POOL_GROUP_WIDTH, POOL_GROUP_WIDTH), lambda i, j: (0, 0, 0)),
                  pl.BlockSpec((1, POOL_WIDTH), lambda i, j: (0, 0))],
        out_specs=[pl.BlockSpec((rows, POOL_WIDTH), lambda i, j: (i * nt + j, 0)),
                   pl.BlockSpec((nseq, POOL_BUF, POOL_WIDTH), lambda i, j: (i, 0, 0))],
        out_shape=[jax.ShapeDtypeStruct((t, POOL_WIDTH), BF16),
                   jax.ShapeDtypeStruct((nb, POOL_BUF, POOL_WIDTH), F32)],
        scratch_shapes=[pltpu.VMEM((16 + tm, POOL_WIDTH), F32),
                        pltpu.VMEM((rows, POOL_WIDTH), F32)],
        compiler_params=_cparams(("arbitrary", "arbitrary")),
        name="pool_branch",
    )(proj, proj, buf, w_grp, scale_row)


def _memkv_kernel(m_ref, g_ref, w_ref, k_ref, v_ref):
    x = m_ref[...]
    ms = jnp.mean(x * x, axis=-1, keepdims=True)
    h = (x * lax.rsqrt(ms + EPS) * g_ref[...]).astype(BF16)
    kv = jnp.dot(h, w_ref[...], preferred_element_type=F32)
    k_ref[...] = kv[:, :MEM_WIDTH]
    v_ref[...] = kv[:, MEM_WIDTH:]


def _mem_kv(mem2d, g_row, w_bf, tm):
    t, d = mem2d.shape
    return pl.pallas_call(
        _memkv_kernel,
        grid=(t // tm,),
        in_specs=[pl.BlockSpec((tm, d), lambda i: (i, 0)),
                  pl.BlockSpec((1, d), lambda i: (0, 0)),
                  pl.BlockSpec((d, 2 * MEM_WIDTH), lambda i: (0, 0))],
        out_specs=[pl.BlockSpec((tm, MEM_WIDTH), lambda i: (i, 0)),
                   pl.BlockSpec((tm, MEM_WIDTH), lambda i: (i, 0))],
        out_shape=[jax.ShapeDtypeStruct((t, MEM_WIDTH), F32),
                   jax.ShapeDtypeStruct((t, MEM_WIDTH), F32)],
        compiler_params=_cparams(("parallel",)),
        name="mem_kv",
    )(mem2d, g_row, w_bf)


def _attn_kernel(cq_ref, cz_ref, k_ref, v_ref, xc_ref, *, nseq, tm):
    scale = MEM_HEAD_DIM ** -0.5
    for b in range(nseq):
        rs = slice(b * tm, (b + 1) * tm)
        for h in range(HEADS):
            cs = slice(h * MEM_HEAD_DIM, (h + 1) * MEM_HEAD_DIM)
            q = cq_ref[rs, cs].astype(BF16)
            k = k_ref[b, :, cs].astype(BF16)
            v = v_ref[b, :, cs].astype(BF16)
            s = lax.dot_general(q, k, (((1,), (1,)), ((), ())), preferred_element_type=F32) * scale
            p = jnp.exp(s - jnp.max(s, axis=-1, keepdims=True))
            a = p * (1.0 / jnp.sum(p, axis=-1, keepdims=True))
            o = jnp.dot(a.astype(BF16), v, preferred_element_type=F32)
            xc_ref[rs, cs] = (o * _silu(cz_ref[rs, cs])).astype(xc_ref.dtype)


def _mem_attn(proj, mem_k, mem_v, nb, seq_len, nseq, tm):
    assert nseq == 1 or tm == seq_len
    nt = seq_len // tm
    rows = nseq * tm
    t = nb * seq_len
    cqb, czb = COL_CQ // MEM_WIDTH, COL_CZ // MEM_WIDTH
    return pl.pallas_call(
        functools.partial(_attn_kernel, nseq=nseq, tm=tm),
        grid=(nb // nseq, nt),
        in_specs=[pl.BlockSpec((rows, MEM_WIDTH), lambda i, j: (i * nt + j, cqb)),
                  pl.BlockSpec((rows, MEM_WIDTH), lambda i, j: (i * nt + j, czb)),
                  pl.BlockSpec((nseq, MEM_TOKENS, MEM_WIDTH), lambda i, j: (i, 0, 0)),
                  pl.BlockSpec((nseq, MEM_TOKENS, MEM_WIDTH), lambda i, j: (i, 0, 0))],
        out_specs=pl.BlockSpec((rows, MEM_WIDTH), lambda i, j: (i * nt + j, 0)),
        out_shape=jax.ShapeDtypeStruct((t, MEM_WIDTH), BF16),
        compiler_params=_cparams(("parallel", "arbitrary")),
        name="mem_attn",
    )(proj, proj, mem_k, mem_v)


def _attn_short_weights(cq_ref, k_ref, b, tm):
    scale = MEM_HEAD_DIM ** -0.5
    nr, nc = tm * HEADS, MEM_TOKENS * HEADS
    row_head = lax.broadcasted_iota(jnp.int32, (nr, nc), 0) // tm
    col_head = lax.broadcasted_iota(jnp.int32, (nr, nc), 1) % HEADS
    rs = slice(b * tm, (b + 1) * tm)
    kf = k_ref[b].reshape(nc, MEM_HEAD_DIM).astype(BF16)
    q = jnp.concatenate([cq_ref[rs, h * MEM_HEAD_DIM:(h + 1) * MEM_HEAD_DIM] for h in range(HEADS)], axis=0)
    s = lax.dot_general(q.astype(BF16), kf, (((1,), (1,)), ((), ())), preferred_element_type=F32) * scale
    s = jnp.where(row_head == col_head, s, -jnp.inf)
    p = jnp.exp(s - jnp.max(s, axis=-1, keepdims=True))
    return (p * (1.0 / jnp.sum(p, axis=-1, keepdims=True))).astype(BF16)


def _attn_short_output(a, cz_ref, v_ref, xc_ref, b, tm):
    rs = slice(b * tm, (b + 1) * tm)
    vf = v_ref[b].reshape(MEM_TOKENS * HEADS, MEM_HEAD_DIM).astype(BF16)
    o = jnp.dot(a, vf, preferred_element_type=F32)
    for h in range(HEADS):
        cs = slice(h * MEM_HEAD_DIM, (h + 1) * MEM_HEAD_DIM)
        xc_ref[rs, cs] = o[h * tm:(h + 1) * tm, :] * _silu(cz_ref[rs, cs])


def _inproj_attn_kernel(h_ref, w_ref, cq_ref, cz_ref, k_ref, v_ref, o_ref, xc_ref, *, nseq, seq_len):
    half = w_ref.shape[0] // 2
    nt = (((1,), (1,)), ((), ()))
    a = [_attn_short_weights(cq_ref, k_ref, b, seq_len) for b in range(nseq)]
    o_ref[:, :half] = lax.dot_general(h_ref[...], w_ref[:half, :], nt, preferred_element_type=F32)
    for b in range(nseq):
        _attn_short_output(a[b], cz_ref, v_ref, xc_ref, b, seq_len)
    o_ref[:, half:] = lax.dot_general(h_ref[...], w_ref[half:, :], nt, preferred_element_type=F32)


def _inproj_with_attn(h, w_bf, tm, tn, proj_s, mem_k, mem_v, seq_len):
    t, d = h.shape
    n_cols = w_bf.shape[0]
    nb = mem_k.shape[0]
    n_inner = t // tm
    n_steps = (n_cols // tn) * n_inner
    nseq = nb // n_steps
    rows = nseq * seq_len
    assert n_cols % tn == 0 and t % tm == 0 and nseq * n_steps == nb and rows % SUBLANES == 0
    cqb, czb = COL_CQ // MEM_WIDTH, COL_CZ // MEM_WIDTH
    kv_spec = pl.BlockSpec((nseq, MEM_TOKENS, HEADS, MEM_HEAD_DIM), lambda j, i: (j * n_inner + i, 0, 0, 0))
    return pl.pallas_call(
        functools.partial(_inproj_attn_kernel, nseq=nseq, seq_len=seq_len),
        grid=(n_cols // tn, n_inner),
        in_specs=[pl.BlockSpec((tm, d), lambda j, i: (i, 0)),
                  pl.BlockSpec((tn, d), lambda j, i: (j, 0)),
                  pl.BlockSpec((rows, MEM_WIDTH), lambda j, i: (j * n_inner + i, cqb)),
                  pl.BlockSpec((rows, MEM_WIDTH), lambda j, i: (j * n_inner + i, czb)),
                  kv_spec, kv_spec],
        out_specs=[pl.BlockSpec((tm, tn), lambda j, i: (i, j)),
                   pl.BlockSpec((rows, MEM_WIDTH), lambda j, i: (j * n_inner + i, 0))],
        out_shape=[jax.ShapeDtypeStruct((t, n_cols), F32),
                   jax.ShapeDtypeStruct((nb * seq_len, MEM_WIDTH), F32)],
        compiler_params=_cparams(("arbitrary", "arbitrary")),
        name="inproj_attn",
    )(h, w_bf, proj_s, proj_s, mem_k, mem_v)


def _mlstm_intra(q_bf, ks_bf, v_bf, ig_row, b_row, b_col, m0_col, mask):
    dlog = jnp.where(mask, b_col - b_row + ig_row, -jnp.inf)
    inter = b_col + m0_col
    m_col = jnp.maximum(inter, jnp.max(dlog, axis=1, keepdims=True))
    dw = jnp.exp(dlog - m_col)
    inter_w = jnp.exp(inter - m_col)
    s = lax.dot_general(q_bf, ks_bf, (((1,), (1,)), ((), ())), preferred_element_type=F32) * dw
    num = jnp.dot(s.astype(BF16), v_bf, preferred_element_type=F32)
    den = jnp.sum(s, axis=1, keepdims=True)
    return num, den, m_col, inter_w


def _mlstm_prompt_kernel(q_ref, k_ref, v_ref, g_ref, gt_ref, hb_ref, c_ref, n_ref, mcol_ref, m_scr):
    @pl.when(pl.program_id(1) == 0)
    def _():
        c_ref[...] = jnp.zeros_like(c_ref)
        n_ref[...] = jnp.zeros_like(n_ref)
        m_scr[...] = jnp.zeros_like(m_scr)

    cl = PROMPT_CHUNK
    row = lax.broadcasted_iota(jnp.int32, (cl, cl), 0)
    col = lax.broadcasted_iota(jnp.int32, (cl, cl), 1)
    mask = col <= row
    kscale = MLSTM_HEAD_DIM ** -0.5
    mcol_ref[...] = jnp.zeros_like(mcol_ref)
    nt = (((1,), (1,)), ((), ()))
    gates = []
    for h in range(HEADS):
        ig_col = g_ref[:, h:h + 1]
        b_col = g_ref[:, HEADS + h:HEADS + h + 1]
        ig_row = gt_ref[h:h + 1, :]
        b_row = gt_ref[HEADS + h:HEADS + h + 1, :]
        m0 = m_scr[h:h + 1, 0:1]
        dlog = jnp.where(mask, b_col - b_row + ig_row, -jnp.inf)
        inter = b_col + m0
        m_col = jnp.maximum(inter, jnp.max(dlog, axis=1, keepdims=True))
        dw = jnp.exp(dlog - m_col)
        inter_w = jnp.exp(inter - m_col)
        m_new = m_col[cl - 1:cl, :]
        b_last = b_col[cl - 1:cl, :]
        w_col = jnp.exp(b_last - b_col + ig_col - m_new)
        decay = jnp.exp(b_last + m0 - m_new)
        m_scr[h:h + 1, :] = jnp.broadcast_to(m_new, (1, LANES))
        mcol_ref[:, h:h + 1] = m_col
        gates.append((m_col, dw, inter_w, w_col, decay))
    prods = []
    for h in range(HEADS):
        cs = slice(h * MLSTM_HEAD_DIM, (h + 1) * MLSTM_HEAD_DIM)
        w_col = gates[h][3]
        q_bf = q_ref[:, cs].astype(BF16)
        ks_bf = (k_ref[:, cs] * kscale).astype(BF16)
        qk = lax.dot_general(q_bf, ks_bf, nt, preferred_element_type=F32)
        inter_num = lax.dot_general(q_bf, c_ref[0, h].astype(BF16), nt, preferred_element_type=F32)
        dc = lax.dot_general((w_col * v_ref[:, cs]).astype(BF16), ks_bf, (((0,), (0,)), ((), ())),
                             preferred_element_type=F32)
        prods.append((qk, inter_num, dc))
    for h in range(HEADS):
        cs = slice(h * MLSTM_HEAD_DIM, (h + 1) * MLSTM_HEAD_DIM)
        m_col, dw, inter_w, w_col, decay = gates[h]
        qk, inter_num, dc = prods[h]
        n0 = n_ref[0, h:h + 1, :]
        s = qk * dw
        num = jnp.dot(s.astype(BF16), v_ref[:, cs].astype(BF16), preferred_element_type=F32) + inter_w * inter_num
        den = jnp.sum(s, axis=1, keepdims=True) + inter_w * jnp.sum(q_ref[:, cs] * n0, axis=1, keepdims=True)
        hb_ref[:, cs] = num * (1.0 / jnp.maximum(jnp.abs(den), jnp.exp(-m_col)))
        c_ref[0, h] = decay * c_ref[0, h] + dc
        n_ref[0, h:h + 1, :] = decay * n0 + jnp.sum(w_col * (k_ref[:, cs] * kscale), axis=0, keepdims=True)


def _mlstm_prompt(proj, g, gt, nb, seq_len):
    cl = PROMPT_CHUNK
    nc = seq_len // cl
    t = nb * seq_len
    w = MLSTM_WIDTH

    def colspec(col0):
        return pl.BlockSpec((cl, w), lambda b, c: (b * nc + c, col0 // w))

    return pl.pallas_call(
        _mlstm_prompt_kernel,
        grid=(nb, nc),
        in_specs=[colspec(COL_Q), colspec(COL_K), colspec(COL_V),
                  pl.BlockSpec((cl, LANES), lambda b, c: (b * nc + c, 0)),
                  pl.BlockSpec((8, cl), lambda b, c: (0, b * nc + c))],
        out_specs=[pl.BlockSpec((cl, w), lambda b, c: (b * nc + c, 0)),
                   pl.BlockSpec((1, HEADS, MLSTM_HEAD_DIM, MLSTM_HEAD_DIM), lambda b, c: (b, 0, 0, 0)),
                   pl.BlockSpec((1, HEADS, MLSTM_HEAD_DIM), lambda b, c: (b, 0, 0)),
                   pl.BlockSpec((cl, LANES), lambda b, c: (b * nc + c, 0))],
        out_shape=[jax.ShapeDtypeStruct((t, w), F32),
                   jax.ShapeDtypeStruct((nb, HEADS, MLSTM_HEAD_DIM, MLSTM_HEAD_DIM), F32),
                   jax.ShapeDtypeStruct((nb, HEADS, MLSTM_HEAD_DIM), F32),
                   jax.ShapeDtypeStruct((t, LANES), F32)],
        scratch_shapes=[pltpu.VMEM((8, LANES), F32)],
        compiler_params=_cparams(("arbitrary", "arbitrary")),
        name="mlstm_prompt",
    )(proj, proj, proj, g, gt)


STAT_DEN, STAT_IW, STAT_M = 0, HEADS, 2 * HEADS


def _mlstm_short_intra_kernel(q_ref, k_ref, v_ref, g_ref, gt_ref, num_ref, stat_ref, *, seq_len):
    tl = LANES
    kscale = MLSTM_HEAD_DIM ** -0.5
    shift = seq_len.bit_length() - 1
    row = lax.broadcasted_iota(jnp.int32, (tl, tl), 0)
    col = lax.broadcasted_iota(jnp.int32, (tl, tl), 1)
    mask = (col <= row) & (lax.shift_right_logical(col, shift) == lax.shift_right_logical(row, shift))
    stat_ref[...] = jnp.zeros_like(stat_ref)
    for h in range(HEADS):
        cs = slice(h * MLSTM_HEAD_DIM, (h + 1) * MLSTM_HEAD_DIM)
        b_col = g_ref[:, HEADS + h:HEADS + h + 1]
        m0_col = g_ref[:, 2 * HEADS + h:2 * HEADS + h + 1]
        ig_row = gt_ref[h:h + 1, :]
        b_row = gt_ref[HEADS + h:HEADS + h + 1, :]
        q_bf = q_ref[:, cs].astype(BF16)
        ks_bf = (k_ref[:, cs] * kscale).astype(BF16)
        num, den, m_col, inter_w = _mlstm_intra(q_bf, ks_bf, v_ref[:, cs].astype(BF16),
                                                ig_row, b_row, b_col, m0_col, mask)
        num_ref[:, cs] = num
        stat_ref[:, STAT_DEN + h:STAT_DEN + h + 1] = den
        stat_ref[:, STAT_IW + h:STAT_IW + h + 1] = inter_w
        stat_ref[:, STAT_M + h:STAT_M + h + 1] = m_col


def _mlstm_short_intra(proj, g, gt, seq_len):
    t = proj.shape[0]
    tl = LANES
    w = MLSTM_WIDTH
    assert tl % seq_len == 0 and t % tl == 0

    def colspec(col0):
        return pl.BlockSpec((tl, w), lambda i: (i, col0 // w))

    return pl.pallas_call(
        functools.partial(_mlstm_short_intra_kernel, seq_len=seq_len),
        grid=(t // tl,),
        in_specs=[colspec(COL_Q), colspec(COL_K), colspec(COL_V),
                  pl.BlockSpec((tl, LANES), lambda i: (i, 0)),
                  pl.BlockSpec((8, tl), lambda i: (0, i))],
        out_specs=[pl.BlockSpec((tl, w), lambda i: (i, 0)),
                   pl.BlockSpec((tl, LANES), lambda i: (i, 0))],
        out_shape=[jax.ShapeDtypeStruct((t, w), F32),
                   jax.ShapeDtypeStruct((t, LANES), F32)],
        compiler_params=_cparams(("parallel",)),
        name="mlstm_short_intra",
    )(proj, proj, proj, g, gt)


def _mlstm_state_head(h, q_ref, k_ref, v_ref, num_ref, g_ref, stat_ref, c0_ref, n0_ref, hb_ref, c_ref, n_ref):
    kscale = MLSTM_HEAD_DIM ** -0.5
    seq_len = q_ref.shape[0]
    last = slice(seq_len - 1, seq_len)
    cs = slice(h * MLSTM_HEAD_DIM, (h + 1) * MLSTM_HEAD_DIM)
    qf = q_ref[:, cs]
    kf = k_ref[:, cs] * kscale
    vf = v_ref[:, cs]
    ig_col = g_ref[:, h:h + 1]
    b_col = g_ref[:, HEADS + h:HEADS + h + 1]
    b_last = g_ref[last, HEADS + h:HEADS + h + 1]
    m0 = g_ref[last, 2 * HEADS + h:2 * HEADS + h + 1]
    den_intra = stat_ref[:, STAT_DEN + h:STAT_DEN + h + 1]
    inter_w = stat_ref[:, STAT_IW + h:STAT_IW + h + 1]
    m_col = stat_ref[:, STAT_M + h:STAT_M + h + 1]
    m_new = stat_ref[last, STAT_M + h:STAT_M + h + 1]
    c0 = c0_ref[0, 0]
    n0 = n0_ref[0, h:h + 1, :]
    num = num_ref[:, cs] + inter_w * lax.dot_general(
        qf.astype(BF16), c0.astype(BF16), (((1,), (1,)), ((), ())), preferred_element_type=F32)
    den = den_intra + inter_w * jnp.sum(qf * n0, axis=1, keepdims=True)
    hval = num * (1.0 / jnp.maximum(jnp.abs(den), jnp.exp(-m_col)))
    hb_ref[:, cs] = hval
    w_col = jnp.exp(b_last - b_col + ig_col - m_new)
    decay = jnp.exp(b_last + m0 - m_new)
    dc = lax.dot_general((w_col * vf).astype(BF16), kf.astype(BF16), (((0,), (0,)), ((), ())),
                         preferred_element_type=F32)
    c_ref[0, h] = decay * c0 + dc
    n_ref[0, h:h + 1, :] = decay * n0 + jnp.sum(w_col * kf, axis=0, keepdims=True)


def _inproj_state_kernel(h_ref, w_ref, *refs):
    tok_refs, c0_refs, n0_ref = refs[:6], refs[6:6 + HEADS], refs[6 + HEADS]
    o_ref, out_state_refs = refs[7 + HEADS], refs[8 + HEADS:]
    half = w_ref.shape[0] // 2
    nt = (((1,), (1,)), ((), ()))
    o_ref[:, :half] = lax.dot_general(h_ref[...], w_ref[:half, :], nt, preferred_element_type=F32)
    for h in range(HEADS):
        _mlstm_state_head(h, *tok_refs, c0_refs[h], n0_ref, *out_state_refs)
    o_ref[:, half:] = lax.dot_general(h_ref[...], w_ref[half:, :], nt, preferred_element_type=F32)


def _inproj_with_state(h, w_bf, tm, tn, proj_s, num_s, g_s, stat_s, c0, n0, seq_len):
    t, d = h.shape
    n_cols = w_bf.shape[0]
    n_seq = c0.shape[0]
    n_inner = t // tm
    assert n_cols % tn == 0 and t % tm == 0 and (n_cols // tn) * n_inner == n_seq
    w = MLSTM_WIDTH
    hd = MLSTM_HEAD_DIM

    def seq(j, i):
        return j * n_inner + i

    def colspec(col0):
        return pl.BlockSpec((seq_len, w), lambda j, i: (seq(j, i), col0 // w))

    rowspec = pl.BlockSpec((seq_len, w), lambda j, i: (seq(j, i), 0))
    lanespec = pl.BlockSpec((seq_len, LANES), lambda j, i: (seq(j, i), 0))
    state4 = pl.BlockSpec((1, HEADS, hd, hd), lambda j, i: (seq(j, i), 0, 0, 0))
    state3 = pl.BlockSpec((1, HEADS, hd), lambda j, i: (seq(j, i), 0, 0))
    return pl.pallas_call(
        _inproj_state_kernel,
        grid=(n_cols // tn, n_inner),
        in_specs=[pl.BlockSpec((tm, d), lambda j, i: (i, 0)),
                  pl.BlockSpec((tn, d), lambda j, i: (j, 0)),
                  colspec(COL_Q), colspec(COL_K), colspec(COL_V),
                  rowspec, lanespec, lanespec,
                  *[pl.BlockSpec((1, 1, hd, hd), functools.partial(lambda j, i, hh: (seq(j, i), hh, 0, 0), hh=hh))
                    for hh in range(HEADS)],
                  state3],
        out_specs=[pl.BlockSpec((tm, tn), lambda j, i: (i, j)), rowspec, state4, state3],
        out_shape=[jax.ShapeDtypeStruct((t, n_cols), F32),
                   jax.ShapeDtypeStruct((n_seq * seq_len, w), F32),
                   jax.ShapeDtypeStruct((n_seq, HEADS, hd, hd), F32),
                   jax.ShapeDtypeStruct((n_seq, HEADS, hd), F32)],
        compiler_params=_cparams(("arbitrary", "arbitrary")),
        name="inproj_state",
    )(h, w_bf, proj_s, proj_s, proj_s, num_s, g_s, stat_s, *([c0] * HEADS), n0)


def _resident(shape):
    return pl.BlockSpec(shape, lambda i: (0, 0), pipeline_mode=pl.Buffered(1))


def _merge_kernel(xa_ref, hb_ref, o_ref, z_ref, xc_ref, ga_ref, gb_ref, gc_ref, wa_ref, wb_ref, wc_ref, m_ref):
    ya = jnp.dot(xa_ref[...].astype(BF16), wa_ref[...], preferred_element_type=F32)
    merged = _sigmoid(ga_ref[...]) * ya
    xb = _sigmoid(o_ref[...]) * hb_ref[...] * _silu(z_ref[...])
    yb = jnp.dot(xb.astype(BF16), wb_ref[...], preferred_element_type=F32)
    merged = merged + _sigmoid(gb_ref[...]) * yb
    yc = jnp.dot(xc_ref[...].astype(BF16), wc_ref[...], preferred_element_type=F32)
    merged = merged + _sigmoid(gc_ref[...]) * yc
    m_ref[...] = merged.astype(m_ref.dtype)


def _out_kernel(m_ref, x_ref, gpost_ref, wo_ref, y_ref):
    out = jnp.dot(m_ref[...], wo_ref[...], preferred_element_type=F32)
    ms = jnp.mean(out * out, axis=-1, keepdims=True)
    y_ref[...] = x_ref[...] + out * lax.rsqrt(ms + EPS) * gpost_ref[...]


def _final(xa, hb, xc, proj_a, proj_b, x2d, gpost_row, wa, wb, wc, wo, tm_merge, tm_out):
    t, d = x2d.shape

    def rowspec(tm, width, colblk=0):
        return pl.BlockSpec((tm, width), lambda i: (i, colblk))

    tm = tm_merge
    merged = pl.pallas_call(
        _merge_kernel,
        grid=(t // tm,),
        in_specs=[rowspec(tm, POOL_WIDTH), rowspec(tm, MLSTM_WIDTH),
                  rowspec(tm, MLSTM_WIDTH, COL_O // MLSTM_WIDTH), rowspec(tm, MLSTM_WIDTH, COL_Z // MLSTM_WIDTH),
                  rowspec(tm, MEM_WIDTH),
                  rowspec(tm, d, COL_GA // d), rowspec(tm, d, COL_GB // d), rowspec(tm, d, COL_GC // d),
                  _resident(wa.shape), _resident(wb.shape), _resident(wc.shape)],
        out_specs=rowspec(tm, d),
        out_shape=jax.ShapeDtypeStruct((t, d), BF16),
        compiler_params=_cparams(("parallel",)),
        name="merge",
    )(xa, hb, proj_a, proj_a, xc, proj_b, proj_b, proj_b, wa, wb, wc)
    tm = tm_out
    return pl.pallas_call(
        _out_kernel,
        grid=(t // tm,),
        in_specs=[rowspec(tm, d), rowspec(tm, d), _resident((1, d)), _resident(wo.shape)],
        out_specs=rowspec(tm, d),
        out_shape=jax.ShapeDtypeStruct((t, d), F32),
        compiler_params=_cparams(("parallel",)),
        name="out_proj",
    )(merged, x2d, gpost_row, wo)


B_ROW0 = GATE_COL0 + N_GATE_COLS


def _pool_and_merge(x2d, nb, seq_len, proj_a, proj_b, xb, xc, pool_buf, wts, start):
    if seq_len >= 512:
        xa, new_buf = _pool_branch(proj_a, pool_buf, wts["w_pool_grp"], wts["pool_scale"], nb, seq_len,
                                   nseq=1, tm=512, start=start)
    else:
        xa, new_buf = _pool_branch(proj_a, pool_buf, wts["w_pool_grp"], wts["pool_scale"], nb, seq_len,
                                   nseq=32, tm=seq_len, start=start)
    y = _final(xa, xb, xc, proj_a, proj_b, x2d, wts["g_post"], wts["w_br_pool"], wts["w_br_mlstm"],
               wts["w_br_mem"], wts["w_out"], tm_merge=256, tm_out=512)
    return y.reshape(nb, seq_len, x2d.shape[1]), new_buf


def kernel(x_prompt, x_sample, state_pool, state_mlstm_C, state_mlstm_n, state_mlstm_m, cache_mem_k,
           cache_mem_v, mem_prompt, g_pre, g_post, w_in, b_mlstm_i, b_mlstm_f, w_pool_grp, pool_scale,
           g_mem, w_mem_kv, w_br_pool, w_br_mlstm, w_br_mem, w_out):
    nbp = x_prompt.shape[0]
    assert w_in.shape[0] == 1, "single-layer problem: the kernels index layer 0 of the stacked weights"
    l = 0
    wts = {
        "g_pre": g_pre[l][None, :],
        "g_post": g_post[l][None, :],
        "w_in": jnp.transpose(w_in[l]),
        "gate_bias": jnp.pad(jnp.concatenate([b_mlstm_i[l], b_mlstm_f[l]]), (0, LANES - N_GATE_COLS))[None, :],
        "w_pool_grp": w_pool_grp[l].astype(BF16),
        "pool_scale": pool_scale[l][None, :],
        "w_br_pool": w_br_pool[l].astype(BF16),
        "w_br_mlstm": w_br_mlstm[l].astype(BF16),
        "w_br_mem": w_br_mem[l].astype(BF16),
        "w_out": w_out[l].astype(BF16),
    }
    mk2d, mv2d = _mem_kv(mem_prompt.reshape(nbp * MEM_TOKENS, D_MODEL), g_mem[l][None, :],
                         w_mem_kv[l].astype(BF16), tm=256)
    mk3, mv3 = mk2d.reshape(nbp, MEM_TOKENS, MEM_WIDTH), mv2d.reshape(nbp, MEM_TOKENS, MEM_WIDTH)
    w_t = wts["w_in"]

    nbs, ls, d = x_sample.shape
    ts = nbs * ls
    xs2d = x_sample.reshape(ts, d)
    m0_rows = jnp.pad(jnp.repeat(state_mlstm_m[l], ls, axis=0), ((0, 0), (2 * HEADS, LANES - 3 * HEADS)))
    hs, gs, gts = _norm_gate_prep(xs2d, wts["g_pre"], w_t, wts["gate_bias"], m0_rows, ls)
    proj_a_s, w_a_bf = _inproj(hs, w_t, 0, N_PROJ_A, tm=ts, tn=1024)
    proj_b_s, w_b_bf = _inproj(hs, w_t, B_ROW0, N_PROJ_B, tm=ts, tn=1024)
    num_s, stat_s = _mlstm_short_intra(proj_a_s, gs, gts, ls)
    m_s = stat_s.reshape(nbs, ls, LANES)[:, ls - 1, STAT_M:STAT_M + HEADS]

    _, lp, _ = x_prompt.shape
    tp = nbp * lp
    xp2d = x_prompt.reshape(tp, d)
    hp, gp, gtp = _norm_gate_prep(xp2d, wts["g_pre"], w_t, wts["gate_bias"], jnp.zeros((tp, LANES), F32),
                                  PROMPT_CHUNK)
    proj_a_p, xb_s, c_s, n_s = _inproj_with_state(hp, w_a_bf, 256, 3072, proj_a_s, num_s, gs, stat_s,
                                                  state_mlstm_C[l], state_mlstm_n[l], ls)
    proj_b_p, xc_s = _inproj_with_attn(hp, w_b_bf, 512, 2048, proj_b_s, cache_mem_k[l], cache_mem_v[l], ls)
    xb_p, c_p, n_p, mcol_p = _mlstm_prompt(proj_a_p, gp, gtp, nbp, lp)
    m_p = mcol_p.reshape(nbp, lp, LANES)[:, lp - 1, :HEADS]
    xc_p = _mem_attn(proj_b_p, mk3, mv3, nbp, lp, nseq=1, tm=1024)

    yp, pool_p = _pool_and_merge(xp2d, nbp, lp, proj_a_p, proj_b_p, xb_p, xc_p,
                                 jnp.zeros((nbp, POOL_BUF, POOL_WIDTH), F32), wts, 0)
    ys, pool_s = _pool_and_merge(xs2d, nbs, ls, proj_a_s, proj_b_s, xb_s, xc_s, state_pool[l], wts, PAST_LEN)
    kv_shape = (1, nbp, MEM_TOKENS, HEADS, MEM_HEAD_DIM)
    return (yp, ys, pool_p[None], c_p[None], n_p[None], m_p[None],
            mk2d.reshape(kv_shape), mv2d.reshape(kv_shape),
            pool_s[None], c_s[None], n_s[None], m_s[None])
```
